```python
import jax
import jax.numpy as jnp
from jax import lax
import numpy as np

D_MODEL = 2048
BATCH = 16
SEQ = 256
DEPTH = 2
DEC_BATCH = 8
DEC_SEQ = 1024
PAST_LEN = 512

GRID_W = 64
N_BRANCH = 4
BRANCH_W = D_MODEL // N_BRANCH
NA_HEAD_DIM = 64
NA_HEADS = BRANCH_W // NA_HEAD_DIM
NA_WIN_ROWS = 8
NA_WIN_COLS = 16
RPB_ROWS = 2 * NA_WIN_ROWS - 1
RPB_COLS = 2 * NA_WIN_COLS - 1
ROPE_BASE = 10000.0
ATTN_BLOCK = 128
GMLP_CHUNK = 128
GMLP_GROUPS = 4
GMLP_GROUP_CH = BRANCH_W // GMLP_GROUPS
FNET_GROUPS = 4
FNET_GROUP_CH = BRANCH_W // FNET_GROUPS
POOL_WINDOWS = (2, 4, 8, 16)
POOL_GROUP_CH = BRANCH_W // len(POOL_WINDOWS)
MIX_WIDTH = N_BRANCH * BRANCH_W
D_FF = 4 * D_MODEL
EPS = 1e-6
NEG_INF = -1e30
IN_COLS = 7 * BRANCH_W + N_BRANCH * D_MODEL
SPLIT_POINTS = tuple(BRANCH_W * i for i in range(1, 8))

kernel_name = 'hybrid_diffusion_trunk_step'


def rmsnorm(x, g):
    xf = x.astype(jnp.float32)
    y = xf * lax.rsqrt(jnp.mean(xf * xf, axis=-1, keepdims=True) + EPS)
    return (y * g.astype(jnp.float32)).astype(x.dtype)


def axial_angles(n):
    t = jnp.arange(n)
    pos = jnp.stack([t // GRID_W, t % GRID_W], axis=-1).astype(jnp.float32)
    half = NA_HEAD_DIM // 2
    inv = 1.0 / (ROPE_BASE ** (jnp.arange(0, half, 2, dtype=jnp.float32) / half))
    return pos[:, :, None] * inv


def apply_axial_rope(x, ang):
    B, N, H, Dh = x.shape
    xs = x.astype(jnp.float32).reshape(B, N, H, 2, 2, Dh // 4)
    cos = jnp.cos(ang)[None, :, None]
    sin = jnp.sin(ang)[None, :, None]
    x1 = xs[..., 0, :]
    x2 = xs[..., 1, :]
    out = jnp.stack([x1 * cos - x2 * sin, x1 * sin + x2 * cos], axis=-2)
    return out.reshape(x.shape).astype(x.dtype)


def context_attention(q, k, v):
    B, S, H, Dh = q.shape
    nb = S // ATTN_BLOCK
    scale = Dh ** -0.5
    kt = k.transpose(0, 2, 1, 3)
    vt = v.transpose(0, 2, 1, 3)
    qb = q.reshape(B, nb, ATTN_BLOCK, H, Dh).transpose(1, 0, 2, 3, 4)

    def block(qi):
        s = jnp.einsum('bqhd,bhkd->bhqk', qi, kt).astype(jnp.float32) * scale
        p = jax.nn.softmax(s, axis=-1).astype(vt.dtype)
        return jnp.einsum('bhqk,bhkd->bqhd', p, vt)

    o = lax.map(block, qb)
    return o.transpose(1, 0, 2, 3, 4).reshape(B, S, H * Dh), kt, vt


def latent_neighbourhood_attention(q, k, v, k_ctx, v_ctx, rpb):
    B, N, H, Dh = q.shape
    rows = N // GRID_W
    wr = min(NA_WIN_ROWS, rows)
    scale = Dh ** -0.5
    ang = axial_angles(N)
    qr = apply_axial_rope(q, ang).reshape(B, rows, GRID_W, H, Dh)
    kr = apply_axial_rope(k, ang).reshape(B, rows, GRID_W, H, Dh)
    qp = q.reshape(B, rows, GRID_W, H, Dh)
    vg = v.reshape(B, rows, GRID_W, H, Dh)
    cq = jnp.arange(GRID_W)
    cs = jnp.clip(cq - NA_WIN_COLS // 2, 0, GRID_W - NA_WIN_COLS)
    kc = jnp.arange(GRID_W)
    col_ok = (kc[None, :] >= cs[:, None]) & (kc[None, :] < cs[:, None] + NA_WIN_COLS)
    col_mask = jnp.where(col_ok, 0.0, NEG_INF).astype(jnp.float32)
    dc_idx = jnp.clip(kc[None, :] - cq[:, None], -(NA_WIN_COLS - 1), NA_WIN_COLS - 1) + NA_WIN_COLS - 1
    rpb_c = rpb[:, :, dc_idx]

    def row_block(r):
        rs = jnp.clip(r - NA_WIN_ROWS // 2, 0, rows - wr)
        q_r = lax.dynamic_index_in_dim(qr, r, axis=1, keepdims=False)
        qp_r = lax.dynamic_index_in_dim(qp, r, axis=1, keepdims=False)
        k_b = lax.dynamic_slice_in_dim(kr, rs, wr, axis=1)
        v_b = lax.dynamic_slice_in_dim(vg, rs, wr, axis=1)
        dr_idx = rs + jnp.arange(wr) - r + NA_WIN_ROWS - 1
        bias = jnp.take(rpb_c, dr_idx, axis=1).transpose(0, 2, 1, 3).astype(jnp.float32)
        bias = bias + col_mask[None, :, None, :]
        s_loc = jnp.einsum('bqhd,bwkhd->bhqwk', q_r, k_b).astype(jnp.float32) * scale + bias[None]
        s_ctx = jnp.einsum('bqhd,bhld->bhql', qp_r, k_ctx).astype(jnp.float32) * scale
        s = jnp.concatenate([s_loc.reshape(B, H, GRID_W, wr * GRID_W), s_ctx], axis=-1)
        p = jax.nn.softmax(s, axis=-1).astype(v.dtype)
        p_loc = p[..., :wr * GRID_W].reshape(B, H, GRID_W, wr, GRID_W)
        p_ctx = p[..., wr * GRID_W:]
        return (jnp.einsum('bhqwk,bwkhd->bqhd', p_loc, v_b)
                + jnp.einsum('bhql,bhld->bqhd', p_ctx, v_ctx))

    o = lax.map(row_block, jnp.arange(rows))
    return o.transpose(1, 0, 2, 3, 4).reshape(B, N, H * Dh)


def spatial_gating(u, vg, norm_g, w_s, b_s):
    B, N, _ = u.shape
    u = jax.nn.gelu(u)
    vg = rmsnorm(jax.nn.gelu(vg), norm_g)
    vc = vg.reshape(B, N // GMLP_CHUNK, GMLP_CHUNK, GMLP_GROUPS, GMLP_GROUP_CH)
    mixed = jnp.einsum('gpq,bnqgc->bnpgc', w_s, vc) + b_s.T[None, None, :, :, None]
    return u * mixed.reshape(B, N, BRANCH_W)


def fourier_mix(xf):
    B, N, _ = xf.shape
    z = xf.astype(jnp.float32).reshape(B, N, FNET_GROUPS, FNET_GROUP_CH)
    y = jnp.fft.fft2(z, axes=(1, 3), norm='ortho').real
    return y.reshape(B, N, BRANCH_W).astype(xf.dtype)


def multiscale_pool(xp, w_pool, pool_scale):
    B, N, _ = xp.shape
    xf = xp.astype(jnp.float32).reshape(B, N, len(POOL_WINDOWS), POOL_GROUP_CH)
    csum = jnp.concatenate([jnp.zeros_like(xf[:, :1]), jnp.cumsum(xf, axis=1)], axis=1)
    t = jnp.arange(N)
    means = []
    for gi, w in enumerate(POOL_WINDOWS):
        lo = jnp.clip(t - w // 2, 0, N - 1)
        hi = jnp.clip(t + w // 2 - 1, 0, N - 1)
        cg = csum[:, :, gi]
        s = jnp.take(cg, hi + 1, axis=1) - jnp.take(cg, lo, axis=1)
        means.append(s / (hi - lo + 1).astype(jnp.float32)[None, :, None])
    pooled = (jnp.stack(means, axis=2) - xf).astype(xp.dtype)
    y = jnp.einsum('bngc,gcd->bngd', pooled, w_pool).reshape(B, N, BRANCH_W)
    return y * pool_scale


def token_mixers(h, ctx_kv, w_in, b_gate, q_norm_g, k_norm_g, rpb, gmlp_norm_g,
                 w_spatial, b_spatial, w_pool, pool_scale, w_branch, w_out):
    B, N, _ = h.shape
    q, k, v, u, vg, xf, xp, gl = jnp.split(h @ w_in, SPLIT_POINTS, axis=-1)
    q = rmsnorm(q.reshape(B, N, NA_HEADS, NA_HEAD_DIM), q_norm_g)
    k = rmsnorm(k.reshape(B, N, NA_HEADS, NA_HEAD_DIM), k_norm_g)
    v = v.reshape(B, N, NA_HEADS, NA_HEAD_DIM)
    if ctx_kv is None:
        a, kt, vt = context_attention(q, k, v)
        kv = (kt, vt)
    else:
        a = latent_neighbourhood_attention(q, k, v, ctx_kv[0], ctx_kv[1], rpb)
        kv = None
    branches = (a,
                spatial_gating(u, vg, gmlp_norm_g, w_spatial, b_spatial),
                fourier_mix(xf),
                multiscale_pool(xp, w_pool, pool_scale))
    gates = jax.nn.sigmoid(gl + b_gate).reshape(B, N, N_BRANCH, D_MODEL)
    merged = gates[:, :, 0] * (branches[0] @ w_branch[0:BRANCH_W])
    for i in range(1, N_BRANCH):
        merged = merged + gates[:, :, i] * (branches[i] @ w_branch[i * BRANCH_W:(i + 1) * BRANCH_W])
    return merged @ w_out, kv


def trunk_layer(x, cond, ctx_kv, norm1_g, w_in, b_gate, q_norm_g, k_norm_g, rpb, gmlp_norm_g,
                w_spatial, b_spatial, w_pool, pool_scale, w_branch, w_out, norm2_g,
                w_mlp1, w_mlp2, w_ada, b_ada):
    m = jax.nn.silu(cond) @ w_ada + b_ada
    sh1, sc1, g1, sh2, sc2, g2 = [t[:, None, :] for t in jnp.split(m, 6, axis=-1)]
    h = rmsnorm(x, norm1_g) * (1 + sc1) + sh1
    mix, kv = token_mixers(h, ctx_kv, w_in, b_gate, q_norm_g, k_norm_g, rpb, gmlp_norm_g,
                           w_spatial, b_spatial, w_pool, pool_scale, w_branch, w_out)
    x = x + g1 * mix
    h2 = rmsnorm(x, norm2_g) * (1 + sc2) + sh2
    x = x + g2 * (jnp.square(jax.nn.relu(h2 @ w_mlp1)) @ w_mlp2)
    return x, kv


def setup_inputs(seed: int = 0) -> dict:
    key = jax.random.key(seed)
    ks = jax.random.split(key, 24)
    f32 = jnp.float32
    nrm = lambda k, shape, s: jax.random.normal(k, shape, f32) * s
    kv_shape = (DEC_BATCH, DEPTH, NA_HEADS, PAST_LEN, NA_HEAD_DIM)
    return {
        'x_prompt': nrm(ks[0], (BATCH, SEQ, D_MODEL), 1.0),
        'x_sample': nrm(ks[1], (DEC_BATCH, DEC_SEQ, D_MODEL), 1.0),
        'cache_k': nrm(ks[2], kv_shape, 1.0),
        'cache_v': nrm(ks[3], kv_shape, 1.0),
        'c': nrm(ks[4], (DEC_BATCH, D_MODEL), 1.0),
        'c_ctx': nrm(ks[5], (D_MODEL,), 1.0),
        'norm1_g': 1.0 + nrm(ks[6], (DEPTH, D_MODEL), 0.02),
        'w_in': nrm(ks[7], (DEPTH, D_MODEL, IN_COLS), D_MODEL ** -0.5),
        'b_gate': nrm(ks[8], (DEPTH, N_BRANCH * D_MODEL), 0.02),
        'q_norm_g': 1.0 + nrm(ks[9], (DEPTH, NA_HEAD_DIM), 0.02),
        'k_norm_g': 1.0 + nrm(ks[10], (DEPTH, NA_HEAD_DIM), 0.02),
        'rpb': nrm(ks[11], (DEPTH, NA_HEADS, RPB_ROWS, RPB_COLS), 0.1),
        'gmlp_norm_g': 1.0 + nrm(ks[12], (DEPTH, BRANCH_W), 0.02),
        'w_spatial': nrm(ks[13], (DEPTH, GMLP_GROUPS, GMLP_CHUNK, GMLP_CHUNK), GMLP_CHUNK ** -0.5),
        'b_spatial': 1.0 + nrm(ks[14], (DEPTH, GMLP_GROUPS, GMLP_CHUNK), 0.02),
        'w_pool': nrm(ks[15], (DEPTH, len(POOL_WINDOWS), POOL_GROUP_CH, POOL_GROUP_CH), POOL_GROUP_CH ** -0.5),
        'pool_scale': 1.0 + nrm(ks[16], (DEPTH, BRANCH_W), 0.02),
        'w_branch': nrm(ks[17], (DEPTH, MIX_WIDTH, D_MODEL), BRANCH_W ** -0.5),
        'w_out': nrm(ks[18], (DEPTH, D_MODEL, D_MODEL), D_MODEL ** -0.5),
        'norm2_g': 1.0 + nrm(ks[19], (DEPTH, D_MODEL), 0.02),
        'w_mlp1': nrm(ks[20], (DEPTH, D_MODEL, D_FF), D_MODEL ** -0.5),
        'w_mlp2': nrm(ks[21], (DEPTH, D_FF, D_MODEL), D_FF ** -0.5),
        'w_ada': nrm(ks[22], (DEPTH, D_MODEL, 6 * D_MODEL), D_MODEL ** -0.5),
        'b_ada': nrm(ks[23], (DEPTH, 6 * D_MODEL), 0.02),
    }


def reference(x_prompt, x_sample, cache_k, cache_v, c, c_ctx, norm1_g, w_in, b_gate,
              q_norm_g, k_norm_g, rpb, gmlp_norm_g, w_spatial, b_spatial, w_pool, pool_scale,
              w_branch, w_out, norm2_g, w_mlp1, w_mlp2, w_ada, b_ada):
    layer_weights = (norm1_g, w_in, b_gate, q_norm_g, k_norm_g, rpb, gmlp_norm_g, w_spatial,
                     b_spatial, w_pool, pool_scale, w_branch, w_out, norm2_g, w_mlp1, w_mlp2,
                     w_ada, b_ada)
    yp = x_prompt
    new_k = []
    new_v = []
    for l in range(DEPTH):
        yp, (k_l, v_l) = trunk_layer(yp, c_ctx[None, :], None, *[w[l] for w in layer_weights])
        new_k.append(k_l)
        new_v.append(v_l)
    ys = x_sample
    for l in range(DEPTH):
        ys, _ = trunk_layer(ys, c, (cache_k[:, l], cache_v[:, l]), *[w[l] for w in layer_weights])
    new_cache_k = jnp.stack(new_k, axis=1)
    new_cache_v = jnp.stack(new_v, axis=1)
    return (yp, ys, new_cache_k, new_cache_v)
```

```python
import functools

import numpy as np
import jax
import jax.numpy as jnp
from jax import lax
from jax.experimental import pallas as pl
from jax.experimental.pallas import tpu as pltpu

F32 = jnp.float32
BF16 = jnp.bfloat16

GRID_W = 64
HEAD_DIM = 64
WIN_ROWS = 8
WIN_COLS = 16
ROPE_BASE = 10000.0
CHUNK = 128
POOL_WINDOWS = (2, 4, 8, 16)
EPS = 1e-6
NEG_INF = -1e30

LANES = 128
MIX_ROWS = 1024
VMEM_LIMIT = 56 * 1024 * 1024


def _cparams(sem):
    return pltpu.CompilerParams(dimension_semantics=sem, vmem_limit_bytes=VMEM_LIMIT)


def _mod_row(i, tm, n_ctx_rows, dec_seq):
    nc = n_ctx_rows // tm
    per = dec_seq // tm
    return jnp.where(i < nc, 0, 1 + (i - nc) // per)


def _ada_kernel(c_ref, w_ref, b_ref, o_ref):
    s = jax.nn.silu(c_ref[...]).astype(BF16)
    o_ref[...] = jnp.dot(s, w_ref[...].astype(BF16), preferred_element_type=F32) + b_ref[...]


def _ada(cond, w_ada, b_ada):
    n_layers, d, n = w_ada.shape
    rows = cond.shape[0]
    tn = 1024
    return pl.pallas_call(
        _ada_kernel,
        grid=(n_layers, n // tn),
        in_specs=[
            pl.BlockSpec((rows, d), lambda l, j: (0, 0)),
            pl.BlockSpec((None, d, tn), lambda l, j: (l, 0, j)),
            pl.BlockSpec((None, 1, tn), lambda l, j: (l, 0, j)),
        ],
        out_specs=pl.BlockSpec((None, rows, tn), lambda l, j: (l, 0, j)),
        out_shape=jax.ShapeDtypeStruct((n_layers, rows, n), F32),
        compiler_params=_cparams(("parallel", "parallel")),
        name="ada",
    )(cond, w_ada, b_ada.reshape(n_layers, 1, n))


def _modulated_norm(x, g, shift, scale):
    ms = jnp.mean(x * x, axis=-1, keepdims=True)
    return (x * lax.rsqrt(ms + EPS) * g) * (1.0 + scale) + shift


def _inproj_kernel(x_ref, mod_ref, g_ref, w_ref, o_ref, h_ref):
    @pl.when(pl.program_id(1) == 0)
    def _():
        h = _modulated_norm(x_ref[...], g_ref[...], mod_ref[0:1, :], mod_ref[1:2, :])
        h_ref[...] = h.astype(BF16)

    o_ref[...] = jnp.dot(h_ref[...], w_ref[...], preferred_element_type=F32).astype(BF16)


def _inproj(x, mod, norm_g, w, layer, n_ctx_rows, dec_seq):
    m, d = x.shape
    n = w.shape[2]
    tm, tn = 1024, 512
    row = functools.partial(_mod_row, tm=tm, n_ctx_rows=n_ctx_rows, dec_seq=dec_seq)
    return pl.pallas_call(
        _inproj_kernel,
        grid=(m // tm, n // tn),
        in_specs=[
            pl.BlockSpec((tm, d), lambda i, j: (i, 0)),
            pl.BlockSpec((None, None, 6, d), lambda i, j: (layer, row(i), 0, 0)),
            pl.BlockSpec((None, 1, d), lambda i, j: (layer, 0, 0)),
            pl.BlockSpec((None, d, tn), lambda i, j: (layer, 0, j)),
        ],
        out_specs=pl.BlockSpec((tm, tn), lambda i, j: (i, j)),
        out_shape=jax.ShapeDtypeStruct((m, n), BF16),
        scratch_shapes=[pltpu.VMEM((tm, d), BF16)],
        compiler_params=_cparams(("parallel", "arbitrary")),
        name="inproj",
    )(x, mod, norm_g, w)


def _head_norm(x, g, lane):
    low = lane < HEAD_DIM
    x2 = x * x
    s_lo = jnp.sum(jnp.where(low, x2, 0.0), axis=-1, keepdims=True)
    s_hi = jnp.sum(jnp.where(low, 0.0, x2), axis=-1, keepdims=True)
    ss = jnp.where(low, s_lo, s_hi)
    return x * lax.rsqrt(ss * (1.0 / HEAD_DIM) + EPS) * g


def _rope(x, cos, sin_signed, lane):
    first = (lane & 16) == 0
    partner = jnp.where(first, pltpu.roll(x, LANES - 16, 1), pltpu.roll(x, 16, 1))
    return x * cos + partner * sin_signed


def _nt_dot(a, b):
    return lax.dot_general(a, b, (((1,), (1,)), ((), ())), preferred_element_type=F32)


def _attn_ctx_kernel(q_ref, k_ref, v_ref, qg_ref, kg_ref, a_ref, kn_ref, vn_ref):
    lane = lax.broadcasted_iota(jnp.int32, (1, LANES), 1)
    scale = HEAD_DIM ** -0.5
    for p in range(q_ref.shape[1] // LANES):
        cols = slice(p * LANES, (p + 1) * LANES)
        q = _head_norm(q_ref[:, cols].astype(F32), qg_ref[...], lane)
        k = _head_norm(k_ref[:, cols].astype(F32), kg_ref[...], lane)
        v = v_ref[:, cols].astype(F32)
        kn_ref[:, cols] = k
        vn_ref[:, cols] = v
        kb = k.astype(BF16)
        acc = jnp.zeros(q.shape, F32)
        for h in range(2):
            mh = ((lane >= HEAD_DIM) == (h == 1)).astype(F32)
            s = _nt_dot((q * mh).astype(BF16), kb) * scale
            e = jnp.exp(s - jnp.max(s, axis=-1, keepdims=True))
            o = jnp.dot(e.astype(BF16), (v * mh).astype(BF16), preferred_element_type=F32)
            acc = acc + o / jnp.sum(e, axis=-1, keepdims=True)
        a_ref[:, cols] = acc.astype(BF16)


def _attn_ctx(p_arr, qg, kg, layer, n_batch, seq, col0):
    w = 4 * LANES
    cb = col0 // w
    rows = n_batch * seq
    blk = lambda c: pl.BlockSpec((seq, w), lambda b: (b, c))
    gain = pl.BlockSpec((None, 1, LANES), lambda b: (layer, 0, 0))
    return pl.pallas_call(
        _attn_ctx_kernel,
        grid=(n_batch,),
        in_specs=[blk(cb), blk(cb + 1), blk(cb + 2), gain, gain],
        out_specs=[pl.BlockSpec((seq, w), lambda b: (b, 0))] * 3,
        out_shape=[jax.ShapeDtypeStruct((rows, w), BF16),
                   jax.ShapeDtypeStruct((rows, w), F32),
                   jax.ShapeDtypeStruct((rows, w), F32)],
        compiler_params=_cparams(("parallel",)),
        name="attn_ctx",
    )(p_arr, p_arr, p_arr, qg, kg)


def _attn_dec_kernel(q_ref, k_ref, v_ref, qg_ref, kg_ref, cos_ref, sin_ref, kc_ref, vc_ref,
                     tab_ref, rm_ref, o_ref, s_ref, sc_ref, p_ref, pc_ref, li_ref):
    n = q_ref.shape[0]
    n_rows = n // GRID_W
    lane = lax.broadcasted_iota(jnp.int32, (1, LANES), 1)
    scale = HEAD_DIM ** -0.5
    q = _head_norm(q_ref[...].astype(F32), qg_ref[...], lane)
    k = _head_norm(k_ref[...].astype(F32), kg_ref[...], lane)
    qr = _rope(q, cos_ref[...], sin_ref[...], lane)
    kr = _rope(k, cos_ref[...], sin_ref[...], lane).astype(BF16)
    v = v_ref[...].astype(F32)
    kc = kc_ref[...]
    vc = vc_ref[...].astype(F32)
    acc = jnp.zeros((n, LANES), F32)
    for h in range(2):
        mh = ((lane >= HEAD_DIM) == (h == 1)).astype(F32)
        s_ref[...] = _nt_dot((qr * mh).astype(BF16), kr)
        sc_ref[...] = _nt_dot((q * mh).astype(BF16), kc)
        for r in range(n_rows):
            rows = pl.ds(r * GRID_W, GRID_W)
            which = (r + 1) % 2
            off = (n_rows - 1 - which - r) * GRID_W
            bias = tab_ref[h, which, :, off:off + n]
            sl = s_ref[rows, :] * scale + bias + rm_ref[r]
            sx = sc_ref[rows, :] * scale
            m = jnp.maximum(jnp.max(sl, axis=-1, keepdims=True), jnp.max(sx, axis=-1, keepdims=True))
            el = jnp.exp(sl - m)
            ex = jnp.exp(sx - m)
            den = jnp.sum(el, axis=-1, keepdims=True) + jnp.sum(ex, axis=-1, keepdims=True)
            p_ref[rows, :] = el.astype(BF16)
            pc_ref[rows, :] = ex.astype(BF16)
            li_ref[rows, :] = jnp.broadcast_to(1.0 / den, (GRID_W, LANES))
        o = jnp.dot(p_ref[...], (v * mh).astype(BF16), preferred_element_type=F32)
        o = o + jnp.dot(pc_ref[...], (vc * mh).astype(BF16), preferred_element_type=F32)
        acc = acc + o * li_ref[...]
    o_ref[...] = acc.astype(BF16)


def _attn_dec(p_arr, qg, kg, cos, sin, kc, vc, tab, rowmask, layer, n_batch, seq, row0, col0, past):
    rb0 = row0 // seq
    cb = col0 // LANES
    n_pairs = 4
    blk = lambda c: pl.BlockSpec((seq, LANES), lambda b, p: (rb0 + b, c + p))
    gain = pl.BlockSpec((None, 1, LANES), lambda b, p: (layer, 0, 0))
    const = pl.BlockSpec((seq, LANES), lambda b, p: (0, 0))
    cache = pl.BlockSpec((None, None, past, LANES), lambda b, p: (b, layer, 0, p))
    return pl.pallas_call(
        _attn_dec_kernel,
        grid=(n_batch, n_pairs),
        in_specs=[
            blk(cb), blk(cb + n_pairs), blk(cb + 2 * n_pairs), gain, gain, const, const, cache, cache,
            pl.BlockSpec((None, 2, 2, GRID_W, 2 * seq), lambda b, p: (layer, p, 0, 0, 0)),
            pl.BlockSpec((seq // GRID_W, 1, seq), lambda b, p: (0, 0, 0)),
        ],
        out_specs=pl.BlockSpec((seq, LANES), lambda b, p: (b, p)),
        out_shape=jax.ShapeDtypeStruct((n_batch * seq, n_pairs * LANES), BF16),
        scratch_shapes=[
            pltpu.VMEM((seq, seq), F32), pltpu.VMEM((seq, past), F32),
            pltpu.VMEM((seq, seq), BF16), pltpu.VMEM((seq, past), BF16),
            pltpu.VMEM((seq, LANES), F32),
        ],
        compiler_params=_cparams(("parallel", "parallel")),
        name="attn_dec",
    )(p_arr, p_arr, p_arr, qg, kg, cos, sin, kc, vc, tab, rowmask)


def _mix_kernel(u_ref, vg_ref, xf_ref, xp_ref, gn_ref, ws_ref, bs_ref, dft_ref, cs_ref, band_ref,
                cnt_ref, wp_ref, ps_ref, o_ref, vgn_ref, ab_ref):
    n = u_ref.shape[0]
    bw = u_ref.shape[1]
    n_groups = bw // CHUNK
    for c in range(n // CHUNK):
        rows = pl.ds(c * CHUNK, CHUNK)
        vg = jax.nn.gelu(vg_ref[rows, :].astype(F32))
        ms = jnp.mean(vg * vg, axis=-1, keepdims=True)
        vgn_ref[rows, :] = (vg * lax.rsqrt(ms + EPS) * gn_ref[...]).astype(BF16)
    for c in range(n // CHUNK):
        rows = pl.ds(c * CHUNK, CHUNK)
        parts = [jnp.dot(ws_ref[g], vgn_ref[rows, g * CHUNK:(g + 1) * CHUNK], preferred_element_type=F32)
                 for g in range(n_groups)]
        mixed = jnp.concatenate(parts, axis=1) + bs_ref[...]
        o_ref[rows, 0:bw] = (jax.nn.gelu(u_ref[rows, :].astype(F32)) * mixed).astype(BF16)
    ab_ref[...] = jnp.dot(dft_ref[...], xf_ref[...], preferred_element_type=F32).astype(BF16)
    for g in range(n_groups):
        gc = slice(g * CHUNK, (g + 1) * CHUNK)
        lhs = jnp.concatenate([ab_ref[0:n, gc], ab_ref[n:2 * n, gc]], axis=1)
        o_ref[:, bw + g * CHUNK:bw + (g + 1) * CHUNK] = jnp.dot(
            lhs, cs_ref[...], preferred_element_type=F32).astype(BF16)
    for g in range(n_groups):
        gc = slice(g * CHUNK, (g + 1) * CHUNK)
        xg = xp_ref[:, gc]
        wsum = jnp.dot(band_ref[g], xg, preferred_element_type=F32)
        pooled = wsum / cnt_ref[:, gc] - xg.astype(F32)
        y = jnp.dot(pooled.astype(BF16), wp_ref[g], preferred_element_type=F32) * ps_ref[:, gc]
        o_ref[:, 2 * bw + g * CHUNK:2 * bw + (g + 1) * CHUNK] = y.astype(BF16)


def _mix(p_arr, gn, ws, bs, dft, cs, band, cnt, wp, ps, layer, n_ctx_rows, col0):
    m = p_arr.shape[0]
    bw = gn.shape[-1]
    n = MIX_ROWS
    n_groups = bw // CHUNK
    cb = col0 // bw
    kind = lambda i: jnp.where(i < n_ctx_rows // n, 0, 1)
    blk = lambda c: pl.BlockSpec((n, bw), lambda i: (i, c))
    once = pl.Buffered(1)
    return pl.pallas_call(
        _mix_kernel,
        grid=(m // n,),
        in_specs=[
            blk(cb), blk(cb + 1), blk(cb + 2), blk(cb + 3),
            pl.BlockSpec((None, 1, bw), lambda i: (layer, 0, 0)),
            pl.BlockSpec((None, n_groups, CHUNK, CHUNK), lambda i: (layer, 0, 0, 0)),
            pl.BlockSpec((None, CHUNK, bw), lambda i: (layer, 0, 0)),
            pl.BlockSpec((None, 2 * n, n), lambda i: (kind(i), 0, 0), pipeline_mode=once),
            pl.BlockSpec((2 * CHUNK, CHUNK), lambda i: (0, 0)),
            pl.BlockSpec((None, n_groups, n, n), lambda i: (kind(i), 0, 0, 0), pipeline_mode=once),
            pl.BlockSpec((None, n, bw), lambda i: (kind(i), 0, 0), pipeline_mode=once),
            pl.BlockSpec((None, n_groups, CHUNK, CHUNK), lambda i: (layer, 0, 0, 0)),
            pl.BlockSpec((None, 1, bw), lambda i: (layer, 0, 0)),
        ],
        out_specs=pl.BlockSpec((n, 3 * bw), lambda i: (i, 0)),
        out_shape=jax.ShapeDtypeStruct((m, 3 * bw), BF16),
        scratch_shapes=[pltpu.VMEM((n, bw), BF16), pltpu.VMEM((2 * n, bw), BF16)],
        compiler_params=_cparams(("parallel",)),
        name="mix",
    )(p_arr, p_arr, p_arr, p_arr, gn, ws, bs, dft, cs, band, cnt, wp, ps)


def _merge_kernel(n_ctx_tiles, x_ref, mod_ref, gl_ref, bg_ref, actx_ref, adec_ref, bcd_ref, wb_ref,
                  wo_ref, o_ref):
    d = x_ref.shape[1]
    bw = actx_ref.shape[1]
    is_ctx = pl.program_id(0) < n_ctx_tiles
    merged = jnp.zeros(x_ref.shape, F32)
    for b in range(d // bw):
        if b == 0:
            br = jnp.where(is_ctx, actx_ref[...], adec_ref[...])
        else:
            br = bcd_ref[:, (b - 1) * bw:b * bw]
        y = jnp.dot(br, wb_ref[b * bw:(b + 1) * bw, :], preferred_element_type=F32)
        gate = jax.nn.sigmoid(gl_ref[:, b * d:(b + 1) * d].astype(F32) + bg_ref[:, b * d:(b + 1) * d])
        merged = merged + gate * y
    mix = jnp.dot(merged.astype(BF16), wo_ref[...], preferred_element_type=F32)
    o_ref[...] = x_ref[...] + mod_ref[2:3, :] * mix


def _merge(x, mod, p_arr, b_gate, a_ctx, a_dec, bcd, wb, wo, layer, n_ctx_rows, dec_seq):
    m, d = x.shape
    bw = a_ctx.shape[1]
    tm = 256
    nct = n_ctx_rows // tm
    row = functools.partial(_mod_row, tm=tm, n_ctx_rows=n_ctx_rows, dec_seq=dec_seq)
    once = pl.Buffered(1)
    return pl.pallas_call(
        functools.partial(_merge_kernel, nct),
        grid=(m // tm,),
        in_specs=[
            pl.BlockSpec((tm, d), lambda i: (i, 0)),
            pl.BlockSpec((None, None, 6, d), lambda i: (layer, row(i), 0, 0)),
            pl.BlockSpec((tm, 4 * d), lambda i: (i, 0)),
            pl.BlockSpec((None, 1, 4 * d), lambda i: (layer, 0, 0)),
            pl.BlockSpec((tm, bw), lambda i: (jnp.minimum(i, nct - 1), 0)),
            pl.BlockSpec((tm, bw), lambda i: (jnp.maximum(i - nct, 0), 0)),
            pl.BlockSpec((tm, 3 * bw), lambda i: (i, 0)),
            pl.BlockSpec((None, d, d), lambda i: (layer, 0, 0), pipeline_mode=once),
            pl.BlockSpec((None, d, d), lambda i: (layer, 0, 0), pipeline_mode=once),
        ],
        out_specs=pl.BlockSpec((tm, d), lambda i: (i, 0)),
        out_shape=jax.ShapeDtypeStruct((m, d), F32),
        compiler_params=_cparams(("parallel",)),
        name="merge",
    )(x, mod, p_arr, b_gate, a_ctx, a_dec, bcd, wb, wo)


def _mlp_kernel(x_ref, mod_ref, g_ref, w1_ref, w2_ref, o_ref, h_ref):
    f = pl.program_id(1)

    @pl.when(f == 0)
    def _():
        h = _modulated_norm(x_ref[...], g_ref[...], mod_ref[3:4, :], mod_ref[4:5, :])
        h_ref[...] = h.astype(BF16)

    a = jnp.dot(h_ref[...], w1_ref[...], preferred_element_type=F32)
    a = jnp.square(jnp.maximum(a, 0.0)).astype(BF16)
    part = jnp.dot(a, w2_ref[...], preferred_element_type=F32)

    @pl.when(f == 0)
    def _():
        o_ref[...] = part

    @pl.when(f > 0)
    def _():
        o_ref[...] += part

    @pl.when(f == pl.num_programs(1) - 1)
    def _():
        o_ref[...] = x_ref[...] + mod_ref[5:6, :] * o_ref[...]


def _mlp(x, mod, norm_g, w1, w2, layer, n_ctx_rows, dec_seq):
    m, d = x.shape
    dff = w1.shape[2]
    tm, tf = 512, 1024
    row = functools.partial(_mod_row, tm=tm, n_ctx_rows=n_ctx_rows, dec_seq=dec_seq)
    return pl.pallas_call(
        _mlp_kernel,
        grid=(m // tm, dff // tf),
        in_specs=[
            pl.BlockSpec((tm, d), lambda i, f: (i, 0)),
            pl.BlockSpec((None, None, 6, d), lambda i, f: (layer, row(i), 0, 0)),
            pl.BlockSpec((None, 1, d), lambda i, f: (layer, 0, 0)),
            pl.BlockSpec((None, d, tf), lambda i, f: (layer, 0, f)),
            pl.BlockSpec((None, tf, d), lambda i, f: (layer, f, 0)),
        ],
        out_specs=pl.BlockSpec((tm, d), lambda i, f: (i, 0)),
        out_shape=jax.ShapeDtypeStruct((m, d), F32),
        scratch_shapes=[pltpu.VMEM((tm, d), BF16)],
        compiler_params=_cparams(("parallel", "arbitrary")),
        name="mlp",
    )(x, mod, norm_g, w1, w2)


def _dft_tables(n_ctx_seq, n):
    def cs(size):
        k = np.arange(size)
        ang = 2.0 * np.pi * ((k[:, None] * k[None, :]) % size) / size
        return np.cos(ang) / np.sqrt(size), np.sin(ang) / np.sqrt(size)

    c_s, s_s = cs(n_ctx_seq)
    eye = np.eye(n // n_ctx_seq)
    c_l, s_l = cs(n)
    pos = np.stack([np.concatenate([np.kron(eye, c_s), np.kron(eye, s_s)], axis=0),
                    np.concatenate([c_l, s_l], axis=0)])
    c_c, s_c = cs(CHUNK)
    chan = np.concatenate([c_c, -s_c], axis=0)
    return jnp.asarray(pos, F32).astype(BF16), jnp.asarray(chan, F32).astype(BF16)


def _pool_tables(n_ctx_seq, n):
    def one(size):
        t = np.arange(size)
        bands, counts = [], []
        for w in POOL_WINDOWS:
            lo = np.clip(t - w // 2, 0, size - 1)
            hi = np.clip(t + w // 2 - 1, 0, size - 1)
            s = np.arange(size)[None, :]
            bands.append(((s >= lo[:, None]) & (s <= hi[:, None])).astype(np.float32))
            counts.append((hi - lo + 1).astype(np.float32))
        return np.stack(bands), np.stack(counts)

    b_s, c_s = one(n_ctx_seq)
    rep = n // n_ctx_seq
    b_l, c_l = one(n)
    band = np.stack([np.stack([np.kron(np.eye(rep), b) for b in b_s]), b_l])
    cnt = np.stack([np.tile(c_s, (1, rep)), c_l])
    cnt = np.repeat(cnt.transpose(0, 2, 1), CHUNK, axis=2)
    return jnp.asarray(band, BF16), jnp.asarray(cnt, F32)


def _rope_tables(n):
    t = jnp.arange(n)
    pos = jnp.stack([t // GRID_W, t % GRID_W], axis=-1).astype(F32)
    half = HEAD_DIM // 2
    inv = 1.0 / (ROPE_BASE ** (jnp.arange(0, half, 2, dtype=F32) / half))
    ang = pos[:, :, None] * inv
    cos, sin = jnp.cos(ang), jnp.sin(ang)
    cos_h = jnp.concatenate([cos[:, 0], cos[:, 0], cos[:, 1], cos[:, 1]], axis=-1)
    sin_h = jnp.concatenate([-sin[:, 0], sin[:, 0], -sin[:, 1], sin[:, 1]], axis=-1)
    return jnp.tile(cos_h, (1, 2)), jnp.tile(sin_h, (1, 2))


def _bias_tables(rpb, n):
    n_layers, n_heads = rpb.shape[:2]
    cq = np.arange(GRID_W)
    cs = np.clip(cq - WIN_COLS // 2, 0, GRID_W - WIN_COLS)
    kc = np.arange(GRID_W)
    col_ok = (kc[None, :] >= cs[:, None]) & (kc[None, :] < cs[:, None] + WIN_COLS)
    col_mask = np.where(col_ok, 0.0, NEG_INF).astype(np.float32)
    dc_idx = np.clip(kc[None, :] - cq[:, None], -(WIN_COLS - 1), WIN_COLS - 1) + WIN_COLS - 1
    t = rpb[:, :, :, dc_idx] + col_mask
    n_d = t.shape[2]
    flat = t.transpose(0, 1, 3, 2, 4).reshape(n_layers, n_heads, GRID_W, n_d * GRID_W)
    front = (WIN_ROWS) * GRID_W
    back = 2 * n - front - n_d * GRID_W
    tab0 = jnp.pad(flat, ((0, 0), (0, 0), (0, 0), (front, back)))
    tab1 = jnp.pad(flat, ((0, 0), (0, 0), (0, 0), (front - GRID_W, back + GRID_W)))
    rows = n // GRID_W
    r = np.arange(rows)
    rs = np.clip(r - WIN_ROWS // 2, 0, rows - WIN_ROWS)
    kr = np.arange(n) // GRID_W
    ok = (kr[None, :] >= rs[:, None]) & (kr[None, :] < rs[:, None] + WIN_ROWS)
    rowmask = np.where(ok, 0.0, NEG_INF).astype(np.float32)[:, None, :]
    return jnp.stack([tab0, tab1], axis=2), jnp.asarray(rowmask)


def kernel(x_prompt, x_sample, cache_k, cache_v, c, c_ctx, norm1_g, w_in, b_gate, q_norm_g, k_norm_g,
           rpb, gmlp_norm_g, w_spatial, b_spatial, w_pool, pool_scale, w_branch, w_out, norm2_g,
           w_mlp1, w_mlp2, w_ada, b_ada):
    n_b, seq, d = x_prompt.shape
    n_db, dec_seq, _ = x_sample.shape
    n_layers = w_in.shape[0]
    n_heads, past = cache_k.shape[2], cache_k.shape[3]
    bw = n_heads * HEAD_DIM
    n_ctx_rows = n_b * seq
    assert dec_seq == MIX_ROWS and MIX_ROWS % seq == 0 and n_ctx_rows % MIX_ROWS == 0
    assert dec_seq // GRID_W == 2 * WIN_ROWS and bw == 4 * LANES

    x = jnp.concatenate([x_prompt.reshape(n_ctx_rows, d), x_sample.reshape(n_db * dec_seq, d)], axis=0)
    pad_rows = (-(1 + n_db)) % 8
    cond = jnp.concatenate([c_ctx[None, :], c, jnp.zeros((pad_rows, d), F32)], axis=0)
    mod = _ada(cond, w_ada, b_ada).reshape(n_layers, cond.shape[0], 6, d)

    n_small = 7 * bw
    w_in_b = jnp.concatenate([w_in[:, :, n_small:], w_in[:, :, :n_small]], axis=2).astype(BF16)
    gate_cols = 4 * d
    wb_b, wo_b = w_branch.astype(BF16), w_out.astype(BF16)
    w1_b, w2_b = w_mlp1.astype(BF16), w_mlp2.astype(BF16)
    ws_b, wp_b = w_spatial.astype(BF16), w_pool.astype(BF16)
    bs_x = jnp.repeat(b_spatial.transpose(0, 2, 1), CHUNK, axis=2)
    qg = jnp.tile(q_norm_g, (1, 2))[:, None, :]
    kg = jnp.tile(k_norm_g, (1, 2))[:, None, :]
    kc = cache_k.transpose(0, 1, 3, 2, 4).reshape(n_db, n_layers, past, bw).astype(BF16)
    vc = cache_v.transpose(0, 1, 3, 2, 4).reshape(n_db, n_layers, past, bw).astype(BF16)
    dft, chan = _dft_tables(seq, MIX_ROWS)
    band, cnt = _pool_tables(seq, MIX_ROWS)
    cos, sin = _rope_tables(dec_seq)
    tab, rowmask = _bias_tables(rpb, dec_seq)
    n1 = norm1_g[:, None, :]
    n2 = norm2_g[:, None, :]
    gn = gmlp_norm_g[:, None, :]
    ps = pool_scale[:, None, :]
    bg = b_gate[:, None, :]

    new_k, new_v = [], []
    for l in range(n_layers):
        p_arr = _inproj(x, mod, n1, w_in_b, l, n_ctx_rows, dec_seq)
        a_ctx, k_n, v_n = _attn_ctx(p_arr, qg, kg, l, n_b, seq, gate_cols)
        a_dec = _attn_dec(p_arr, qg, kg, cos, sin, kc, vc, tab, rowmask, l, n_db, dec_seq,
                          n_ctx_rows, gate_cols, past)
        bcd = _mix(p_arr, gn, ws_b, bs_x, dft, chan, band, cnt, wp_b, ps, l, n_ctx_rows, gate_cols + 3 * bw)
        x = _merge(x, mod, p_arr, bg, a_ctx, a_dec, bcd, wb_b, wo_b, l, n_ctx_rows, dec_seq)
        x = _mlp(x, mod, n2, w1_b, w2_b, l, n_ctx_rows, dec_seq)
        new_k.append(k_n.reshape(n_b, seq, n_heads, HEAD_DIM).transpose(0, 2, 1, 3))
        new_v.append(v_n.reshape(n_b, seq, n_heads, HEAD_DIM).transpose(0, 2, 1, 3))

    y_prompt = x[:n_ctx_rows].reshape(n_b, seq, d)
    y_sample = x[n_ctx_rows:].reshape(n_db, dec_seq, d)
    return y_prompt, y_sample, jnp.stack(new_k, axis=1), jnp.stack(new_v, axis=1)
```

```python
import functools

import numpy as np
import jax
import jax.numpy as jnp
from jax import lax
from jax.experimental import pallas as pl
from jax.experimental.pallas import tpu as pltpu

F32 = jnp.float32
BF16 = jnp.bfloat16

GRID_W = 64
HEAD_DIM = 64
WIN_ROWS = 8
WIN_COLS = 16
ROPE_BASE = 10000.0
CHUNK = 128
POOL_WINDOWS = (2, 4, 8, 16)
EPS = 1e-6
NEG_INF = -1e30

LANES = 128
MIX_ROWS = 1024
VMEM_LIMIT = 56 * 1024 * 1024


def _cparams(sem):
    return pltpu.CompilerParams(dimension_semantics=sem, vmem_limit_bytes=VMEM_LIMIT)


def _mod_row(i, tm, n_ctx_rows, dec_seq):
    nc = n_ctx_rows // tm
    per = dec_seq // tm
    return jnp.where(i < nc, 0, 1 + (i - nc) // per)


def _ada_kernel(c_ref, w_ref, b_ref, o_ref):
    s = jax.nn.silu(c_ref[...]).astype(BF16)
    o_ref[...] = jnp.dot(s, w_ref[...].astype(BF16), preferred_element_type=F32) + b_ref[...]


def _ada(cond, w_ada, b_ada):
    n_layers, d, n = w_ada.shape
    rows = cond.shape[0]
    tn = 1024
    return pl.pallas_call(
        _ada_kernel,
        grid=(n_layers, n // tn),
        in_specs=[
            pl.BlockSpec((rows, d), lambda l, j: (0, 0)),
            pl.BlockSpec((None, d, tn), lambda l, j: (l, 0, j)),
            pl.BlockSpec((None, 1, tn), lambda l, j: (l, 0, j)),
        ],
        out_specs=pl.BlockSpec((None, rows, tn), lambda l, j: (l, 0, j)),
        out_shape=jax.ShapeDtypeStruct((n_layers, rows, n), F32),
        compiler_params=_cparams(("parallel", "parallel")),
        name="ada",
    )(cond, w_ada, b_ada.reshape(n_layers, 1, n))


def _modulated_norm(x, g, shift, scale):
    ms = jnp.mean(x * x, axis=-1, keepdims=True)
    return (x * lax.rsqrt(ms + EPS) * g) * (1.0 + scale) + shift


def _x_tile_copy(xa_hbm, xb_hbm, use_a, tile, nct, tm, xbuf, sem):
    if use_a:
        src = xa_hbm.at[pl.ds(tile * tm, tm), :]
    else:
        src = xb_hbm.at[pl.ds((tile - nct) * tm, tm), :]
    return pltpu.make_async_copy(src, xbuf, sem)


def _inproj_kernel(nct, n_small, xa_hbm, xb_hbm, mod_ref, g_ref, w_ref, os_ref, og_ref, xbuf, h_ref, sem):
    i = pl.program_id(0)
    j = pl.program_id(1)
    tm = xbuf.shape[0]
    copy = functools.partial(_x_tile_copy, xa_hbm, xb_hbm, nct=nct, tm=tm, xbuf=xbuf, sem=sem)

    def start(tile):
        @pl.when(tile < nct)
        def _():
            copy(True, tile).start()

        @pl.when(tile >= nct)
        def _():
            copy(False, tile).start()

    @pl.when((i == 0) & (j == 0))
    def _():
        start(i)

    @pl.when(j == 0)
    def _():
        copy(True, 0).wait()
        h = _modulated_norm(xbuf[...], g_ref[...], mod_ref[0:1, :], mod_ref[1:2, :])
        h_ref[...] = h.astype(BF16)

    @pl.when((j == 1) & (i + 1 < pl.num_programs(0)))
    def _():
        start(i + 1)

    res = jnp.dot(h_ref[...], w_ref[...], preferred_element_type=F32).astype(BF16)

    @pl.when(j < n_small)
    def _():
        os_ref[...] = res

    @pl.when(j >= n_small)
    def _():
        og_ref[...] = res


def _inproj(xa, xb, mod, norm_g, w, layer, n_small_cols, dec_seq):
    d = xa.shape[1]
    n_ctx_rows = xa.shape[0]
    m = n_ctx_rows + xb.shape[0]
    n = w.shape[2]
    tm, tn = 1024, 512
    ns = n_small_cols // tn
    row = functools.partial(_mod_row, tm=tm, n_ctx_rows=n_ctx_rows, dec_seq=dec_seq)
    return pl.pallas_call(
        functools.partial(_inproj_kernel, n_ctx_rows // tm, ns),
        grid=(m // tm, n // tn),
        in_specs=[
            pl.BlockSpec(memory_space=pl.ANY),
            pl.BlockSpec(memory_space=pl.ANY),
            pl.BlockSpec((None, None, 6, d), lambda i, j: (layer, row(i), 0, 0)),
            pl.BlockSpec((None, 1, d), lambda i, j: (layer, 0, 0)),
            pl.BlockSpec((None, d, tn), lambda i, j: (layer, 0, j)),
        ],
        out_specs=[
            pl.BlockSpec((tm, tn), lambda i, j: (i, jnp.minimum(j, ns - 1))),
            pl.BlockSpec((tm, tn), lambda i, j: (i, jnp.maximum(j - ns, 0))),
        ],
        out_shape=[jax.ShapeDtypeStruct((m, n_small_cols), BF16),
                   jax.ShapeDtypeStruct((m, n - n_small_cols), BF16)],
        scratch_shapes=[pltpu.VMEM((tm, d), F32), pltpu.VMEM((tm, d), BF16), pltpu.SemaphoreType.DMA(())],
        compiler_params=_cparams(("arbitrary", "arbitrary")),
        name="inproj",
    )(xa, xb, mod, norm_g, w)


def _head_norm(x, g, lane):
    low = lane < HEAD_DIM
    x2 = x * x
    s_lo = jnp.sum(jnp.where(low, x2, 0.0), axis=-1, keepdims=True)
    s_hi = jnp.sum(jnp.where(low, 0.0, x2), axis=-1, keepdims=True)
    ss = jnp.where(low, s_lo, s_hi)
    return x * lax.rsqrt(ss * (1.0 / HEAD_DIM) + EPS) * g


def _rope(x, cos, sin_signed, lane):
    first = (lane & 16) == 0
    partner = jnp.where(first, pltpu.roll(x, LANES - 16, 1), pltpu.roll(x, 16, 1))
    return x * cos + partner * sin_signed


def _nt_dot(a, b):
    return lax.dot_general(a, b, (((1,), (1,)), ((), ())), preferred_element_type=F32)


def _head_mask(lane, h, value):
    return jnp.where((lane >= HEAD_DIM) == (h == 1), value, 0.0).astype(F32)


def _attn_ctx_kernel(q_ref, k_ref, v_ref, qg_ref, kg_ref, a_ref, kn_ref, vn_ref):
    lane = lax.broadcasted_iota(jnp.int32, (1, LANES), 1)
    scale = HEAD_DIM ** -0.5
    for p in range(q_ref.shape[1] // LANES):
        cols = slice(p * LANES, (p + 1) * LANES)
        q = _head_norm(q_ref[:, cols].astype(F32), qg_ref[...], lane)
        k = _head_norm(k_ref[:, cols].astype(F32), kg_ref[...], lane)
        v = v_ref[:, cols].astype(F32)
        kn_ref[:, cols] = k
        vn_ref[:, cols] = v
        kb = k.astype(BF16)
        acc = jnp.zeros(q.shape, F32)
        for h in range(2):
            s = _nt_dot((q * _head_mask(lane, h, scale)).astype(BF16), kb)
            e = jnp.exp(s - jnp.max(s, axis=-1, keepdims=True))
            o = jnp.dot(e.astype(BF16), (v * _head_mask(lane, h, 1.0)).astype(BF16),
                        preferred_element_type=F32)
            acc = acc + o / jnp.sum(e, axis=-1, keepdims=True)
        a_ref[:, cols] = acc.astype(BF16)


def _attn_ctx(p_arr, qg, kg, layer, n_batch, seq):
    w = 4 * LANES
    rows = n_batch * seq
    blk = lambda c: pl.BlockSpec((seq, w), lambda b: (b, c))
    gain = pl.BlockSpec((None, 1, LANES), lambda b: (layer, 0, 0))
    return pl.pallas_call(
        _attn_ctx_kernel,
        grid=(n_batch,),
        in_specs=[blk(0), blk(1), blk(2), gain, gain],
        out_specs=[pl.BlockSpec((seq, w), lambda b: (b, 0))] * 3,
        out_shape=[jax.ShapeDtypeStruct((rows, w), BF16),
                   jax.ShapeDtypeStruct((rows, w), F32),
                   jax.ShapeDtypeStruct((rows, w), F32)],
        compiler_params=_cparams(("parallel",)),
        name="attn_ctx",
    )(p_arr, p_arr, p_arr, qg, kg)


def _row_window(r, n_rows):
    rs = min(max(r - WIN_ROWS // 2, 0), n_rows - WIN_ROWS)
    if rs % 2 == 0:
        return rs * GRID_W, WIN_ROWS * GRID_W, True
    return (rs - 1) * GRID_W, (WIN_ROWS + 2) * GRID_W, False


def _attn_dec_kernel(q_ref, k_ref, v_ref, qg_ref, kg_ref, cos_ref, sin_ref, kc_ref, vc_ref,
                     tab_ref, o_ref, s_ref, sc_ref, p_ref, pc_ref, li_ref, tsh_ref):
    n = q_ref.shape[0]
    n_rows = n // GRID_W
    lane = lax.broadcasted_iota(jnp.int32, (1, LANES), 1)
    scale = HEAD_DIM ** -0.5
    q = _head_norm(q_ref[...].astype(F32), qg_ref[...], lane)
    k = _head_norm(k_ref[...].astype(F32), kg_ref[...], lane)
    qr = _rope(q, cos_ref[...], sin_ref[...], lane)
    kr = _rope(k, cos_ref[...], sin_ref[...], lane).astype(BF16)
    v = v_ref[...].astype(F32)
    kc = kc_ref[...]
    vc = vc_ref[...].astype(F32)
    wide = (WIN_ROWS + 2) * GRID_W
    wl = lax.broadcasted_iota(jnp.int32, (1, wide), 1)
    edge_mask = jnp.where((wl < GRID_W) | (wl >= wide - GRID_W), NEG_INF, 0.0).astype(F32)
    acc = jnp.zeros((n, LANES), F32)
    for h in range(2):
        tsh_ref[...] = pltpu.roll(tab_ref[h], 2 * n - GRID_W, 1)
        s_ref[...] = _nt_dot((qr * _head_mask(lane, h, scale)).astype(BF16), kr)
        sc_ref[...] = _nt_dot((q * _head_mask(lane, h, scale)).astype(BF16), kc)
        for r in range(n_rows):
            rows = pl.ds(r * GRID_W, GRID_W)
            a, w, exact = _row_window(r, n_rows)
            if (n_rows - 1 - r) % 2 == 0:
                bias = tab_ref[h, :, (n_rows - 1 - r) * GRID_W + a:(n_rows - 1 - r) * GRID_W + a + w]
            else:
                bias = tsh_ref[:, (n_rows - 2 - r) * GRID_W + a:(n_rows - 2 - r) * GRID_W + a + w]
            sl = s_ref[rows, a:a + w] + bias
            if not exact:
                sl = sl + edge_mask
            sx = sc_ref[rows, :]
            m = jnp.maximum(jnp.max(sl, axis=-1, keepdims=True), jnp.max(sx, axis=-1, keepdims=True))
            el = jnp.exp(sl - m)
            ex = jnp.exp(sx - m)
            den = jnp.sum(el, axis=-1, keepdims=True) + jnp.sum(ex, axis=-1, keepdims=True)
            pieces = [el.astype(BF16)]
            if a > 0:
                pieces.insert(0, jnp.zeros((GRID_W, a), BF16))
            if a + w < n:
                pieces.append(jnp.zeros((GRID_W, n - a - w), BF16))
            p_ref[rows, :] = jnp.concatenate(pieces, axis=1) if len(pieces) > 1 else pieces[0]
            pc_ref[rows, :] = ex.astype(BF16)
            li_ref[rows, :] = jnp.broadcast_to(1.0 / den, (GRID_W, LANES))
        vm = _head_mask(lane, h, 1.0)
        o = jnp.dot(p_ref[...], (v * vm).astype(BF16), preferred_element_type=F32)
        o = o + jnp.dot(pc_ref[...], (vc * vm).astype(BF16), preferred_element_type=F32)
        acc = acc + o * li_ref[...]
    o_ref[...] = acc.astype(BF16)


def _attn_dec(p_arr, qg, kg, cos, sin, kc, vc, tab, layer, n_batch, seq, row0, past):
    rb0 = row0 // seq
    n_pairs = 4
    blk = lambda c: pl.BlockSpec((seq, LANES), lambda b, p: (rb0 + b, c + p))
    gain = pl.BlockSpec((None, 1, LANES), lambda b, p: (layer, 0, 0))
    const = pl.BlockSpec((seq, LANES), lambda b, p: (0, 0))
    cache = pl.BlockSpec((None, None, past, LANES), lambda b, p: (b, layer, 0, p))
    return pl.pallas_call(
        _attn_dec_kernel,
        grid=(n_batch, n_pairs),
        in_specs=[
            blk(0), blk(n_pairs), blk(2 * n_pairs), gain, gain, const, const, cache, cache,
            pl.BlockSpec((None, 2, GRID_W, 2 * seq), lambda b, p: (layer, p, 0, 0)),
        ],
        out_specs=pl.BlockSpec((seq, LANES), lambda b, p: (b, p)),
        out_shape=jax.ShapeDtypeStruct((n_batch * seq, n_pairs * LANES), BF16),
        scratch_shapes=[
            pltpu.VMEM((seq, seq), F32), pltpu.VMEM((seq, past), F32),
            pltpu.VMEM((seq, seq), BF16), pltpu.VMEM((seq, past), BF16),
            pltpu.VMEM((seq, LANES), F32), pltpu.VMEM((GRID_W, 2 * seq), F32),
        ],
        compiler_params=_cparams(("parallel", "parallel")),
        name="attn_dec",
    )(p_arr, p_arr, p_arr, qg, kg, cos, sin, kc, vc, tab)


def _mix_kernel(u_ref, vg_ref, xf_ref, xp_ref, gn_ref, ws_ref, bs_ref, dft_ref, cs_ref, band_ref,
                cnt_ref, wp_ref, ps_ref, o_ref, vgn_ref, ab_ref):
    n = u_ref.shape[0]
    bw = u_ref.shape[1]
    n_groups = bw // CHUNK
    for c in range(n // CHUNK):
        rows = pl.ds(c * CHUNK, CHUNK)
        vg = jax.nn.gelu(vg_ref[rows, :].astype(F32))
        ms = jnp.mean(vg * vg, axis=-1, keepdims=True)
        vgn_ref[rows, :] = (vg * lax.rsqrt(ms + EPS) * gn_ref[...]).astype(BF16)
    for c in range(n // CHUNK):
        rows = pl.ds(c * CHUNK, CHUNK)
        parts = [jnp.dot(ws_ref[g], vgn_ref[rows, g * CHUNK:(g + 1) * CHUNK], preferred_element_type=F32)
                 for g in range(n_groups)]
        mixed = jnp.concatenate(parts, axis=1) + bs_ref[...]
        o_ref[rows, 0:bw] = (jax.nn.gelu(u_ref[rows, :].astype(F32)) * mixed).astype(BF16)
    ab_ref[...] = jnp.dot(dft_ref[...], xf_ref[...], preferred_element_type=F32).astype(BF16)
    for g in range(n_groups):
        gc = slice(g * CHUNK, (g + 1) * CHUNK)
        lhs = jnp.concatenate([ab_ref[0:n, gc], ab_ref[n:2 * n, gc]], axis=1)
        o_ref[:, bw + g * CHUNK:bw + (g + 1) * CHUNK] = jnp.dot(
            lhs, cs_ref[...], preferred_element_type=F32).astype(BF16)
    for g in range(n_groups):
        gc = slice(g * CHUNK, (g + 1) * CHUNK)
        xg = xp_ref[:, gc]
        wsum = jnp.dot(band_ref[g], xg, preferred_element_type=F32)
        pooled = wsum / cnt_ref[:, gc] - xg.astype(F32)
        y = jnp.dot(pooled.astype(BF16), wp_ref[g], preferred_element_type=F32) * ps_ref[:, gc]
        o_ref[:, 2 * bw + g * CHUNK:2 * bw + (g + 1) * CHUNK] = y.astype(BF16)


def _mix(p_arr, gn, ws, bs, dft, cs, band, cnt, wp, ps, layer, n_ctx_rows):
    m = p_arr.shape[0]
    bw = gn.shape[-1]
    n = MIX_ROWS
    n_groups = bw // CHUNK
    kind = lambda i: jnp.where(i < n_ctx_rows // n, 0, 1)
    blk = lambda c: pl.BlockSpec((n, bw), lambda i: (i, c))
    once = pl.Buffered(1)
    return pl.pallas_call(
        _mix_kernel,
        grid=(m // n,),
        in_specs=[
            blk(3), blk(4), blk(5), blk(6),
            pl.BlockSpec((None, 1, bw), lambda i: (layer, 0, 0)),
            pl.BlockSpec((None, n_groups, CHUNK, CHUNK), lambda i: (layer, 0, 0, 0)),
            pl.BlockSpec((None, CHUNK, bw), lambda i: (layer, 0, 0)),
            pl.BlockSpec((None, 2 * n, n), lambda i: (kind(i), 0, 0), pipeline_mode=once),
            pl.BlockSpec((2 * CHUNK, CHUNK), lambda i: (0, 0)),
            pl.BlockSpec((None, n_groups, n, n), lambda i: (kind(i), 0, 0, 0), pipeline_mode=once),
            pl.BlockSpec((None, n, bw), lambda i: (kind(i), 0, 0), pipeline_mode=once),
            pl.BlockSpec((None, n_groups, CHUNK, CHUNK), lambda i: (layer, 0, 0, 0)),
            pl.BlockSpec((None, 1, bw), lambda i: (layer, 0, 0)),
        ],
        out_specs=pl.BlockSpec((n, 3 * bw), lambda i: (i, 0)),
        out_shape=jax.ShapeDtypeStruct((m, 3 * bw), BF16),
        scratch_shapes=[pltpu.VMEM((n, bw), BF16), pltpu.VMEM((2 * n, bw), BF16)],
        compiler_params=_cparams(("parallel",)),
        name="mix",
    )(p_arr, p_arr, p_arr, p_arr, gn, ws, bs, dft, cs, band, cnt, wp, ps)


def _merge_kernel(n_ctx_tiles, xa_ref, xb_ref, mod_ref, gl_ref, bg_ref, actx_ref, adec_ref, bcd_ref,
                  wb_ref, wo_ref, o_ref):
    d = xa_ref.shape[1]
    bw = actx_ref.shape[1]
    is_ctx = pl.program_id(0) < n_ctx_tiles
    merged = jnp.zeros(xa_ref.shape, F32)
    for b in range(d // bw):
        if b == 0:
            br = jnp.where(is_ctx, actx_ref[...], adec_ref[...])
        else:
            br = bcd_ref[:, (b - 1) * bw:b * bw]
        y = jnp.dot(br, wb_ref[b * bw:(b + 1) * bw, :], preferred_element_type=F32)
        gate = jax.nn.sigmoid(gl_ref[:, b * d:(b + 1) * d].astype(F32) + bg_ref[:, b * d:(b + 1) * d])
        merged = merged + gate * y
    mix = jnp.dot(merged.astype(BF16), wo_ref[...], preferred_element_type=F32)
    x = jnp.where(is_ctx, xa_ref[...], xb_ref[...])
    o_ref[...] = x + mod_ref[2:3, :] * mix


def _merge(xa, xb, mod, gl, b_gate, a_ctx, a_dec, bcd, wb, wo, layer, dec_seq):
    d = xa.shape[1]
    n_ctx_rows = xa.shape[0]
    m = n_ctx_rows + xb.shape[0]
    bw = a_ctx.shape[1]
    tm = 256
    nct = n_ctx_rows // tm
    row = functools.partial(_mod_row, tm=tm, n_ctx_rows=n_ctx_rows, dec_seq=dec_seq)
    once = pl.Buffered(1)
    ctx_tile = lambda i: (jnp.minimum(i, nct - 1), 0)
    dec_tile = lambda i: (jnp.maximum(i - nct, 0), 0)
    return pl.pallas_call(
        functools.partial(_merge_kernel, nct),
        grid=(m // tm,),
        in_specs=[
            pl.BlockSpec((tm, d), ctx_tile),
            pl.BlockSpec((tm, d), dec_tile),
            pl.BlockSpec((None, None, 6, d), lambda i: (layer, row(i), 0, 0)),
            pl.BlockSpec((tm, 4 * d), lambda i: (i, 0)),
            pl.BlockSpec((None, 1, 4 * d), lambda i: (layer, 0, 0)),
            pl.BlockSpec((tm, bw), ctx_tile),
            pl.BlockSpec((tm, bw), dec_tile),
            pl.BlockSpec((tm, 3 * bw), lambda i: (i, 0)),
            pl.BlockSpec((None, d, d), lambda i: (layer, 0, 0), pipeline_mode=once),
            pl.BlockSpec((None, d, d), lambda i: (layer, 0, 0), pipeline_mode=once),
        ],
        out_specs=pl.BlockSpec((tm, d), lambda i: (i, 0)),
        out_shape=jax.ShapeDtypeStruct((m, d), F32),
        compiler_params=_cparams(("parallel",)),
        name="merge",
    )(xa, xb, mod, gl, b_gate, a_ctx, a_dec, bcd, wb, wo)


def _mlp_kernel(n_ctx_tiles, x_ref, mod_ref, g_ref, w1_ref, w2_ref, oa_ref, ob_ref, h_ref, acc_ref):
    f = pl.program_id(1)
    is_ctx = pl.program_id(0) < n_ctx_tiles
    last = f == pl.num_programs(1) - 1

    @pl.when(f == 0)
    def _():
        h = _modulated_norm(x_ref[...], g_ref[...], mod_ref[3:4, :], mod_ref[4:5, :])
        h_ref[...] = h.astype(BF16)

    a = jnp.dot(h_ref[...], w1_ref[...], preferred_element_type=F32)
    a = jnp.square(jnp.maximum(a, 0.0)).astype(BF16)
    part = jnp.dot(a, w2_ref[...], preferred_element_type=F32)

    @pl.when(f == 0)
    def _():
        acc_ref[...] = part

    @pl.when(f > 0)
    def _():
        acc_ref[...] += part

    @pl.when(last & is_ctx)
    def _():
        oa_ref[...] = x_ref[...] + mod_ref[5:6, :] * acc_ref[...]

    @pl.when(last & jnp.logical_not(is_ctx))
    def _():
        ob_ref[...] = x_ref[...] + mod_ref[5:6, :] * acc_ref[...]


def _mlp(x, mod, norm_g, w1, w2, layer, n_ctx_rows, dec_seq):
    m, d = x.shape
    dff = w1.shape[2]
    tm, tf = 512, 1024
    nct = n_ctx_rows // tm
    row = functools.partial(_mod_row, tm=tm, n_ctx_rows=n_ctx_rows, dec_seq=dec_seq)
    return pl.pallas_call(
        functools.partial(_mlp_kernel, nct),
        grid=(m // tm, dff // tf),
        in_specs=[
            pl.BlockSpec((tm, d), lambda i, f: (i, 0)),
            pl.BlockSpec((None, None, 6, d), lambda i, f: (layer, row(i), 0, 0)),
            pl.BlockSpec((None, 1, d), lambda i, f: (layer, 0, 0)),
            pl.BlockSpec((None, d, tf), lambda i, f: (layer, 0, f)),
            pl.BlockSpec((None, tf, d), lambda i, f: (layer, f, 0)),
        ],
        out_specs=[
            pl.BlockSpec((tm, d), lambda i, f: (jnp.minimum(i, nct - 1), 0)),
            pl.BlockSpec((tm, d), lambda i, f: (jnp.maximum(i - nct, 0), 0)),
        ],
        out_shape=[jax.ShapeDtypeStruct((n_ctx_rows, d), F32),
                   jax.ShapeDtypeStruct((m - n_ctx_rows, d), F32)],
        scratch_shapes=[pltpu.VMEM((tm, d), BF16), pltpu.VMEM((tm, d), F32)],
        compiler_params=_cparams(("arbitrary", "arbitrary")),
        name="mlp",
    )(x, mod, norm_g, w1, w2)


def _dft_tables(n_ctx_seq, n):
    def cs(size):
        k = np.arange(size)
        ang = 2.0 * np.pi * ((k[:, None] * k[None, :]) % size) / size
        return np.cos(ang) / np.sqrt(size), np.sin(ang) / np.sqrt(size)

    c_s, s_s = cs(n_ctx_seq)
    eye = np.eye(n // n_ctx_seq)
    c_l, s_l = cs(n)
    pos = np.stack([np.concatenate([np.kron(eye, c_s), np.kron(eye, s_s)], axis=0),
                    np.concatenate([c_l, s_l], axis=0)])
    c_c, s_c = cs(CHUNK)
    chan = np.concatenate([c_c, -s_c], axis=0)
    return jnp.asarray(pos, F32).astype(BF16), jnp.asarray(chan, F32).astype(BF16)


def _pool_tables(n_ctx_seq, n):
    def one(size):
        t = np.arange(size)
        bands, counts = [], []
        for w in POOL_WINDOWS:
            lo = np.clip(t - w // 2, 0, size - 1)
            hi = np.clip(t + w // 2 - 1, 0, size - 1)
            s = np.arange(size)[None, :]
            bands.append(((s >= lo[:, None]) & (s <= hi[:, None])).astype(np.float32))
            counts.append((hi - lo + 1).astype(np.float32))
        return np.stack(bands), np.stack(counts)

    b_s, c_s = one(n_ctx_seq)
    rep = n // n_ctx_seq
    b_l, c_l = one(n)
    band = np.stack([np.stack([np.kron(np.eye(rep), b) for b in b_s]), b_l])
    cnt = np.stack([np.tile(c_s, (1, rep)), c_l])
    cnt = np.repeat(cnt.transpose(0, 2, 1), CHUNK, axis=2)
    return jnp.asarray(band, BF16), jnp.asarray(cnt, F32)


def _rope_tables(n):
    t = jnp.arange(n)
    pos = jnp.stack([t // GRID_W, t % GRID_W], axis=-1).astype(F32)
    half = HEAD_DIM // 2
    inv = 1.0 / (ROPE_BASE ** (jnp.arange(0, half, 2, dtype=F32) / half))
    ang = pos[:, :, None] * inv
    cos, sin = jnp.cos(ang), jnp.sin(ang)
    cos_h = jnp.concatenate([cos[:, 0], cos[:, 0], cos[:, 1], cos[:, 1]], axis=-1)
    sin_h = jnp.concatenate([-sin[:, 0], sin[:, 0], -sin[:, 1], sin[:, 1]], axis=-1)
    return jnp.tile(cos_h, (1, 2)), jnp.tile(sin_h, (1, 2))


def _bias_table(rpb, n):
    n_layers, n_heads, n_d, n_e = rpb.shape
    cq = np.arange(GRID_W)
    cs = np.clip(cq - WIN_COLS // 2, 0, GRID_W - WIN_COLS)
    kc = np.arange(GRID_W)
    col_ok = (kc[None, :] >= cs[:, None]) & (kc[None, :] < cs[:, None] + WIN_COLS)
    col_mask = np.where(col_ok, 0.0, NEG_INF).astype(np.float32)
    dc_idx = np.clip(kc[None, :] - cq[:, None], -(WIN_COLS - 1), WIN_COLS - 1) + WIN_COLS - 1
    pick = (dc_idx[None, :, :] == np.arange(n_e)[:, None, None]).astype(np.float32)
    t = jnp.einsum("lhde,eck->lhcdk", rpb, jnp.asarray(pick), precision=lax.Precision.HIGHEST)
    t = t + jnp.asarray(col_mask)[:, None, :]
    flat = t.reshape(n_layers, n_heads, GRID_W, n_d * GRID_W)
    front = WIN_ROWS * GRID_W
    back = 2 * n - front - n_d * GRID_W
    return jnp.pad(flat, ((0, 0), (0, 0), (0, 0), (front, back)))


def kernel(x_prompt, x_sample, cache_k, cache_v, c, c_ctx, norm1_g, w_in, b_gate, q_norm_g, k_norm_g,
           rpb, gmlp_norm_g, w_spatial, b_spatial, w_pool, pool_scale, w_branch, w_out, norm2_g,
           w_mlp1, w_mlp2, w_ada, b_ada):
    n_b, seq, d = x_prompt.shape
    n_db, dec_seq, _ = x_sample.shape
    n_layers = w_in.shape[0]
    n_heads, past = cache_k.shape[2], cache_k.shape[3]
    bw = n_heads * HEAD_DIM
    n_ctx_rows = n_b * seq
    assert dec_seq == MIX_ROWS and MIX_ROWS % seq == 0 and n_ctx_rows % MIX_ROWS == 0
    assert dec_seq // GRID_W == 2 * WIN_ROWS and bw == 4 * LANES

    xa = x_prompt.reshape(n_ctx_rows, d)
    xb = x_sample.reshape(n_db * dec_seq, d)
    pad_rows = (-(1 + n_db)) % 8
    cond = jnp.concatenate([c_ctx[None, :], c, jnp.zeros((pad_rows, d), F32)], axis=0)
    mod = _ada(cond, w_ada, b_ada).reshape(n_layers, cond.shape[0], 6, d)

    w_in_b = w_in.astype(BF16)
    wb_b, wo_b = w_branch.astype(BF16), w_out.astype(BF16)
    w1_b, w2_b = w_mlp1.astype(BF16), w_mlp2.astype(BF16)
    ws_b, wp_b = w_spatial.astype(BF16), w_pool.astype(BF16)
    bs_x = jnp.repeat(b_spatial.transpose(0, 2, 1), CHUNK, axis=2)
    qg = jnp.tile(q_norm_g, (1, 2))[:, None, :]
    kg = jnp.tile(k_norm_g, (1, 2))[:, None, :]
    kc = cache_k.transpose(0, 1, 3, 2, 4).reshape(n_db, n_layers, past, bw).astype(BF16)
    vc = cache_v.transpose(0, 1, 3, 2, 4).reshape(n_db, n_layers, past, bw).astype(BF16)
    dft, chan = _dft_tables(seq, MIX_ROWS)
    band, cnt = _pool_tables(seq, MIX_ROWS)
    cos, sin = _rope_tables(dec_seq)
    tab = _bias_table(rpb, dec_seq)
    n1 = norm1_g[:, None, :]
    n2 = norm2_g[:, None, :]
    gn = gmlp_norm_g[:, None, :]
    ps = pool_scale[:, None, :]
    bg = b_gate[:, None, :]

    new_k, new_v = [], []
    for l in range(n_layers):
        p_arr, gl = _inproj(xa, xb, mod, n1, w_in_b, l, 7 * bw, dec_seq)
        a_ctx, k_n, v_n = _attn_ctx(p_arr, qg, kg, l, n_b, seq)
        a_dec = _attn_dec(p_arr, qg, kg, cos, sin, kc, vc, tab, l, n_db, dec_seq, n_ctx_rows, past)
        bcd = _mix(p_arr, gn, ws_b, bs_x, dft, chan, band, cnt, wp_b, ps, l, n_ctx_rows)
        x1 = _merge(xa, xb, mod, gl, bg, a_ctx, a_dec, bcd, wb_b, wo_b, l, dec_seq)
        xa, xb = _mlp(x1, mod, n2, w1_b, w2_b, l, n_ctx_rows, dec_seq)
        new_k.append(k_n.reshape(n_b, seq, n_heads, HEAD_DIM).transpose(0, 2, 1, 3))
        new_v.append(v_n.reshape(n_b, seq, n_heads, HEAD_DIM).transpose(0, 2, 1, 3))

    return (xa.reshape(n_b, seq, d), xb.reshape(n_db, dec_seq, d),
            jnp.stack(new_k, axis=1), jnp.stack(new_v, axis=1))
```

```python
import functools

import numpy as np
import jax
import jax.numpy as jnp
from jax import lax
from jax.experimental import pallas as pl
from jax.experimental.pallas import tpu as pltpu

F32 = jnp.float32
BF16 = jnp.bfloat16

GRID_W = 64
HEAD_DIM = 64
WIN_ROWS = 8
WIN_COLS = 16
ROPE_BASE = 10000.0
CHUNK = 128
POOL_WINDOWS = (2, 4, 8, 16)
EPS = 1e-6
NEG_INF = -1e30

LANES = 128
MIX_ROWS = 1024
VMEM_LIMIT = 56 * 1024 * 1024


def _cparams(sem):
    return pltpu.CompilerParams(dimension_semantics=sem, vmem_limit_bytes=VMEM_LIMIT)


def _mod_row(i, tm, n_ctx_rows, dec_seq):
    nc = n_ctx_rows // tm
    per = dec_seq // tm
    return jnp.where(i < nc, 0, 1 + (i - nc) // per)


def _ada_kernel(c_ref, w_ref, b_ref, o_ref):
    s = jax.nn.silu(c_ref[...]).astype(BF16)
    o_ref[...] = jnp.dot(s, w_ref[...].astype(BF16), preferred_element_type=F32) + b_ref[...]


def _ada(cond, w_ada, b_ada):
    n_layers, d, n = w_ada.shape
    rows = cond.shape[0]
    tn = 1024
    return pl.pallas_call(
        _ada_kernel,
        grid=(n_layers, n // tn),
        in_specs=[
            pl.BlockSpec((rows, d), lambda l, j: (0, 0)),
            pl.BlockSpec((None, d, tn), lambda l, j: (l, 0, j)),
            pl.BlockSpec((None, 1, tn), lambda l, j: (l, 0, j)),
        ],
        out_specs=pl.BlockSpec((None, rows, tn), lambda l, j: (l, 0, j)),
        out_shape=jax.ShapeDtypeStruct((n_layers, rows, n), F32),
        compiler_params=_cparams(("parallel", "parallel")),
        name="ada",
    )(cond, w_ada, b_ada.reshape(n_layers, 1, n))


def _modulated_norm(x, g, shift, scale):
    ms = jnp.mean(x * x, axis=-1, keepdims=True)
    return (x * lax.rsqrt(ms + EPS) * g) * (1.0 + scale) + shift


def _x_tile_copy(xa_hbm, xb_hbm, use_a, tile, nct, tm, xbuf, sem):
    if use_a:
        src = xa_hbm.at[pl.ds(tile * tm, tm), :]
    else:
        src = xb_hbm.at[pl.ds((tile - nct) * tm, tm), :]
    return pltpu.make_async_copy(src, xbuf, sem)


def _inproj_kernel(nct, n_small, xa_hbm, xb_hbm, mod_ref, g_ref, w_ref, bg_ref, os_ref, og_ref, xbuf, h_ref,
                   sem):
    i = pl.program_id(0)
    j = pl.program_id(1)
    tm = xbuf.shape[0]
    copy = functools.partial(_x_tile_copy, xa_hbm, xb_hbm, nct=nct, tm=tm, xbuf=xbuf, sem=sem)

    def start(tile):
        @pl.when(tile < nct)
        def _():
            copy(True, tile).start()

        @pl.when(tile >= nct)
        def _():
            copy(False, tile).start()

    @pl.when((i == 0) & (j == 0))
    def _():
        start(i)

    @pl.when(j == 0)
    def _():
        copy(True, 0).wait()
        h = _modulated_norm(xbuf[...], g_ref[...], mod_ref[0:1, :], mod_ref[1:2, :])
        h_ref[...] = h.astype(BF16)

    @pl.when((j == 1) & (i + 1 < pl.num_programs(0)))
    def _():
        start(i + 1)

    @pl.when(j < n_small)
    def _():
        os_ref[...] = jnp.dot(h_ref[...], w_ref[...], preferred_element_type=F32).astype(BF16)

    @pl.when(j >= n_small)
    def _():
        logits = jnp.dot(h_ref[...], w_ref[...], preferred_element_type=F32) + bg_ref[...]
        og_ref[...] = jax.nn.sigmoid(logits).astype(BF16)


def _inproj(xa, xb, mod, norm_g, w, b_gate, layer, n_small_cols, dec_seq):
    d = xa.shape[1]
    n_ctx_rows = xa.shape[0]
    m = n_ctx_rows + xb.shape[0]
    n = w.shape[2]
    tm, tn = 1024, 512
    ns = n_small_cols // tn
    row = functools.partial(_mod_row, tm=tm, n_ctx_rows=n_ctx_rows, dec_seq=dec_seq)
    return pl.pallas_call(
        functools.partial(_inproj_kernel, n_ctx_rows // tm, ns),
        grid=(m // tm, n // tn),
        in_specs=[
            pl.BlockSpec(memory_space=pl.ANY),
            pl.BlockSpec(memory_space=pl.ANY),
            pl.BlockSpec((None, None, 6, d), lambda i, j: (layer, row(i), 0, 0)),
            pl.BlockSpec((None, 1, d), lambda i, j: (layer, 0, 0)),
            pl.BlockSpec((None, d, tn), lambda i, j: (layer, 0, j)),
            pl.BlockSpec((None, 1, tn), lambda i, j: (layer, 0, jnp.maximum(j - ns, 0))),
        ],
        out_specs=[
            pl.BlockSpec((tm, tn), lambda i, j: (i, jnp.minimum(j, ns - 1))),
            pl.BlockSpec((tm, tn), lambda i, j: (i, jnp.maximum(j - ns, 0))),
        ],
        out_shape=[jax.ShapeDtypeStruct((m, n_small_cols), BF16),
                   jax.ShapeDtypeStruct((m, n - n_small_cols), BF16)],
        scratch_shapes=[pltpu.VMEM((tm, d), F32), pltpu.VMEM((tm, d), BF16), pltpu.SemaphoreType.DMA(())],
        compiler_params=_cparams(("arbitrary", "arbitrary")),
        name="inproj",
    )(xa, xb, mod, norm_g, w, b_gate)


def _head_norm(x, g, lane):
    row = lax.broadcasted_iota(jnp.int32, (LANES, LANES), 0)
    same_head = (row >= HEAD_DIM) == (lane >= HEAD_DIM)
    ones = jnp.where(same_head, 1.0, 0.0).astype(BF16)
    x2 = x * x
    hi = x2.astype(BF16)
    lo = (x2 - hi.astype(F32)).astype(BF16)
    ss = (jnp.dot(hi, ones, preferred_element_type=F32) + jnp.dot(lo, ones, preferred_element_type=F32))
    return x * lax.rsqrt(ss * (1.0 / HEAD_DIM) + EPS) * g


def _rope(x, cos, sin_signed, lane):
    first = (lane & 16) == 0
    partner = jnp.where(first, pltpu.roll(x, LANES - 16, 1), pltpu.roll(x, 16, 1))
    return x * cos + partner * sin_signed


def _nt_dot(a, b):
    return lax.dot_general(a, b, (((1,), (1,)), ((), ())), preferred_element_type=F32)


def _head_mask(lane, h, value):
    return jnp.where((lane >= HEAD_DIM) == (h == 1), value, 0.0).astype(F32)


def _attn_ctx_kernel(q_ref, k_ref, v_ref, qg_ref, kg_ref, a_ref, kn_ref, vn_ref):
    lane = lax.broadcasted_iota(jnp.int32, (1, LANES), 1)
    scale = HEAD_DIM ** -0.5
    for p in range(q_ref.shape[1] // LANES):
        cols = slice(p * LANES, (p + 1) * LANES)
        q = _head_norm(q_ref[:, cols].astype(F32), qg_ref[...], lane)
        k = _head_norm(k_ref[:, cols].astype(F32), kg_ref[...], lane)
        v = v_ref[:, cols].astype(F32)
        kn_ref[:, cols] = k
        vn_ref[:, cols] = v
        kb = k.astype(BF16)
        acc = jnp.zeros(q.shape, F32)
        for h in range(2):
            s = _nt_dot((q * _head_mask(lane, h, scale)).astype(BF16), kb)
            e = jnp.exp(s - jnp.max(s, axis=-1, keepdims=True))
            o = jnp.dot(e.astype(BF16), (v * _head_mask(lane, h, 1.0)).astype(BF16),
                        preferred_element_type=F32)
            acc = acc + o / jnp.sum(e, axis=-1, keepdims=True)
        a_ref[:, cols] = acc.astype(BF16)


def _attn_ctx(p_arr, qg, kg, layer, n_batch, seq):
    w = 4 * LANES
    rows = n_batch * seq
    blk = lambda c: pl.BlockSpec((seq, w), lambda b: (b, c))
    gain = pl.BlockSpec((None, 1, LANES), lambda b: (layer, 0, 0))
    return pl.pallas_call(
        _attn_ctx_kernel,
        grid=(n_batch,),
        in_specs=[blk(0), blk(1), blk(2), gain, gain],
        out_specs=[pl.BlockSpec((seq, w), lambda b: (b, 0))] * 3,
        out_shape=[jax.ShapeDtypeStruct((rows, w), BF16),
                   jax.ShapeDtypeStruct((rows, w), F32),
                   jax.ShapeDtypeStruct((rows, w), F32)],
        compiler_params=_cparams(("parallel",)),
        name="attn_ctx",
    )(p_arr, p_arr, p_arr, qg, kg)


def _row_window(r, n_rows):
    rs = min(max(r - WIN_ROWS // 2, 0), n_rows - WIN_ROWS)
    if rs % 2 == 0:
        return rs * GRID_W, WIN_ROWS * GRID_W, True
    return (rs - 1) * GRID_W, (WIN_ROWS + 2) * GRID_W, False


def _attn_dec_kernel(q_ref, k_ref, v_ref, qg_ref, kg_ref, cos_ref, sin_ref, kc_ref, vc_ref,
                     tab_ref, o_ref, s_ref, sc_ref, p_ref, pc_ref, li_ref, tsh_ref):
    n = q_ref.shape[0]
    n_rows = n // GRID_W
    lane = lax.broadcasted_iota(jnp.int32, (1, LANES), 1)
    scale = HEAD_DIM ** -0.5
    q = _head_norm(q_ref[...].astype(F32), qg_ref[...], lane)
    k = _head_norm(k_ref[...].astype(F32), kg_ref[...], lane)
    qr = _rope(q, cos_ref[...], sin_ref[...], lane)
    kr = _rope(k, cos_ref[...], sin_ref[...], lane).astype(BF16)
    v = v_ref[...].astype(F32)
    kc = kc_ref[...]
    vc = vc_ref[...].astype(F32)
    wide = (WIN_ROWS + 2) * GRID_W
    wl = lax.broadcasted_iota(jnp.int32, (1, wide), 1)
    edge_mask = jnp.where((wl < GRID_W) | (wl >= wide - GRID_W), NEG_INF, 0.0).astype(F32)
    acc = jnp.zeros((n, LANES), F32)
    for h in range(2):
        tsh_ref[...] = pltpu.roll(tab_ref[h], 2 * n - GRID_W, 1)
        s_ref[...] = _nt_dot((qr * _head_mask(lane, h, scale)).astype(BF16), kr)
        sc_ref[...] = _nt_dot((q * _head_mask(lane, h, scale)).astype(BF16), kc)
        for r in range(n_rows):
            rows = pl.ds(r * GRID_W, GRID_W)
            a, w, exact = _row_window(r, n_rows)
            if (n_rows - 1 - r) % 2 == 0:
                bias = tab_ref[h, :, (n_rows - 1 - r) * GRID_W + a:(n_rows - 1 - r) * GRID_W + a + w]
            else:
                bias = tsh_ref[:, (n_rows - 2 - r) * GRID_W + a:(n_rows - 2 - r) * GRID_W + a + w]
            sl = s_ref[rows, a:a + w] + bias
            if not exact:
                sl = sl + edge_mask
            sx = sc_ref[rows, :]
            m = jnp.maximum(jnp.max(sl, axis=-1, keepdims=True), jnp.max(sx, axis=-1, keepdims=True))
            el = jnp.exp(sl - m)
            ex = jnp.exp(sx - m)
            den = jnp.sum(el, axis=-1, keepdims=True) + jnp.sum(ex, axis=-1, keepdims=True)
            pieces = [el.astype(BF16)]
            if a > 0:
                pieces.insert(0, jnp.zeros((GRID_W, a), BF16))
            if a + w < n:
                pieces.append(jnp.zeros((GRID_W, n - a - w), BF16))
            p_ref[rows, :] = jnp.concatenate(pieces, axis=1) if len(pieces) > 1 else pieces[0]
            pc_ref[rows, :] = ex.astype(BF16)
            li_ref[rows, :] = jnp.broadcast_to(1.0 / den, (GRID_W, LANES))
        vm = _head_mask(lane, h, 1.0)
        o = jnp.dot(p_ref[...], (v * vm).astype(BF16), preferred_element_type=F32)
        o = o + jnp.dot(pc_ref[...], (vc * vm).astype(BF16), preferred_element_type=F32)
        acc = acc + o * li_ref[...]
    o_ref[...] = acc.astype(BF16)


def _attn_dec(p_arr, qg, kg, cos, sin, kc, vc, tab, layer, n_batch, seq, row0, past):
    rb0 = row0 // seq
    n_pairs = 4
    blk = lambda c: pl.BlockSpec((seq, LANES), lambda b, p: (rb0 + b, c + p))
    gain = pl.BlockSpec((None, 1, LANES), lambda b, p: (layer, 0, 0))
    const = pl.BlockSpec((seq, LANES), lambda b, p: (0, 0))
    cache = pl.BlockSpec((None, None, past, LANES), lambda b, p: (b, layer, 0, p))
    return pl.pallas_call(
        _attn_dec_kernel,
        grid=(n_batch, n_pairs),
        in_specs=[
            blk(0), blk(n_pairs), blk(2 * n_pairs), gain, gain, const, const, cache, cache,
            pl.BlockSpec((None, 2, GRID_W, 2 * seq), lambda b, p: (layer, p, 0, 0)),
        ],
        out_specs=pl.BlockSpec((seq, LANES), lambda b, p: (b, p)),
        out_shape=jax.ShapeDtypeStruct((n_batch * seq, n_pairs * LANES), BF16),
        scratch_shapes=[
            pltpu.VMEM((seq, seq), F32), pltpu.VMEM((seq, past), F32),
            pltpu.VMEM((seq, seq), BF16), pltpu.VMEM((seq, past), BF16),
            pltpu.VMEM((seq, LANES), F32), pltpu.VMEM((GRID_W, 2 * seq), F32),
        ],
        compiler_params=_cparams(("parallel", "parallel")),
        name="attn_dec",
    )(p_arr, p_arr, p_arr, qg, kg, cos, sin, kc, vc, tab)


def _mix_kernel(u_ref, vg_ref, xf_ref, xp_ref, gn_ref, ws_ref, bs_ref, dft_ref, cs_ref, band_ref,
                cnt_ref, wp_ref, ps_ref, o_ref, vgn_ref, ab_ref):
    n = u_ref.shape[0]
    bw = u_ref.shape[1]
    n_groups = bw // CHUNK
    for c in range(n // CHUNK):
        rows = pl.ds(c * CHUNK, CHUNK)
        vg = jax.nn.gelu(vg_ref[rows, :].astype(F32))
        ms = jnp.mean(vg * vg, axis=-1, keepdims=True)
        vgn_ref[rows, :] = (vg * lax.rsqrt(ms + EPS) * gn_ref[...]).astype(BF16)
    for c in range(n // CHUNK):
        rows = pl.ds(c * CHUNK, CHUNK)
        parts = [jnp.dot(ws_ref[g], vgn_ref[rows, g * CHUNK:(g + 1) * CHUNK], preferred_element_type=F32)
                 for g in range(n_groups)]
        mixed = jnp.concatenate(parts, axis=1) + bs_ref[...]
        o_ref[rows, 0:bw] = (jax.nn.gelu(u_ref[rows, :].astype(F32)) * mixed).astype(BF16)
    ab_ref[...] = jnp.dot(dft_ref[...], xf_ref[...], preferred_element_type=F32).astype(BF16)
    for g in range(n_groups):
        gc = slice(g * CHUNK, (g + 1) * CHUNK)
        lhs = jnp.concatenate([ab_ref[0:n, gc], ab_ref[n:2 * n, gc]], axis=1)
        o_ref[:, bw + g * CHUNK:bw + (g + 1) * CHUNK] = jnp.dot(
            lhs, cs_ref[...], preferred_element_type=F32).astype(BF16)
    for g in range(n_groups):
        gc = slice(g * CHUNK, (g + 1) * CHUNK)
        xg = xp_ref[:, gc]
        wsum = jnp.dot(band_ref[g], xg, preferred_element_type=F32)
        pooled = wsum / cnt_ref[:, gc] - xg.astype(F32)
        y = jnp.dot(pooled.astype(BF16), wp_ref[g], preferred_element_type=F32) * ps_ref[:, gc]
        o_ref[:, 2 * bw + g * CHUNK:2 * bw + (g + 1) * CHUNK] = y.astype(BF16)


def _mix(p_arr, gn, ws, bs, dft, cs, band, cnt, wp, ps, layer, n_ctx_rows):
    m = p_arr.shape[0]
    bw = gn.shape[-1]
    n = MIX_ROWS
    n_groups = bw // CHUNK
    kind = lambda i: jnp.where(i < n_ctx_rows // n, 0, 1)
    blk = lambda c: pl.BlockSpec((n, bw), lambda i: (i, c))
    once = pl.Buffered(1)
    return pl.pallas_call(
        _mix_kernel,
        grid=(m // n,),
        in_specs=[
            blk(3), blk(4), blk(5), blk(6),
            pl.BlockSpec((None, 1, bw), lambda i: (layer, 0, 0)),
            pl.BlockSpec((None, n_groups, CHUNK, CHUNK), lambda i: (layer, 0, 0, 0)),
            pl.BlockSpec((None, CHUNK, bw), lambda i: (layer, 0, 0)),
            pl.BlockSpec((None, 2 * n, n), lambda i: (kind(i), 0, 0), pipeline_mode=once),
            pl.BlockSpec((2 * CHUNK, CHUNK), lambda i: (0, 0)),
            pl.BlockSpec((None, n_groups, n, n), lambda i: (kind(i), 0, 0, 0), pipeline_mode=once),
            pl.BlockSpec((None, n, bw), lambda i: (kind(i), 0, 0), pipeline_mode=once),
            pl.BlockSpec((None, n_groups, CHUNK, CHUNK), lambda i: (layer, 0, 0, 0)),
            pl.BlockSpec((None, 1, bw), lambda i: (layer, 0, 0)),
        ],
        out_specs=pl.BlockSpec((n, 3 * bw), lambda i: (i, 0)),
        out_shape=jax.ShapeDtypeStruct((m, 3 * bw), BF16),
        scratch_shapes=[pltpu.VMEM((n, bw), BF16), pltpu.VMEM((2 * n, bw), BF16)],
        compiler_params=_cparams(("parallel",)),
        name="mix",
    )(p_arr, p_arr, p_arr, p_arr, gn, ws, bs, dft, cs, band, cnt, wp, ps)


def _merge_kernel(n_ctx_tiles, xa_ref, xb_ref, mod_ref, g2_ref, gate_ref, actx_ref, adec_ref, bcd_ref,
                  wb_ref, wo_ref, o_ref, h_ref):
    d = xa_ref.shape[1]
    bw = actx_ref.shape[1]
    is_ctx = pl.program_id(0) < n_ctx_tiles
    merged = jnp.zeros(xa_ref.shape, F32)
    for b in range(d // bw):
        if b == 0:
            br = jnp.where(is_ctx, actx_ref[...], adec_ref[...])
        else:
            br = bcd_ref[:, (b - 1) * bw:b * bw]
        y = jnp.dot(br, wb_ref[b * bw:(b + 1) * bw, :], preferred_element_type=F32)
        merged = merged + gate_ref[:, b * d:(b + 1) * d].astype(F32) * y
    mix = jnp.dot(merged.astype(BF16), wo_ref[...], preferred_element_type=F32)
    x = jnp.where(is_ctx, xa_ref[...], xb_ref[...])
    x1 = x + mod_ref[2:3, :] * mix
    o_ref[...] = x1
    h_ref[...] = _modulated_norm(x1, g2_ref[...], mod_ref[3:4, :], mod_ref[4:5, :]).astype(BF16)


def _merge(xa, xb, mod, norm2_g, gates, a_ctx, a_dec, bcd, wb, wo, layer, dec_seq):
    d = xa.shape[1]
    n_ctx_rows = xa.shape[0]
    m = n_ctx_rows + xb.shape[0]
    bw = a_ctx.shape[1]
    tm = 256
    nct = n_ctx_rows // tm
    row = functools.partial(_mod_row, tm=tm, n_ctx_rows=n_ctx_rows, dec_seq=dec_seq)
    once = pl.Buffered(1)
    ctx_tile = lambda i: (jnp.minimum(i, nct - 1), 0)
    dec_tile = lambda i: (jnp.maximum(i - nct, 0), 0)
    return pl.pallas_call(
        functools.partial(_merge_kernel, nct),
        grid=(m // tm,),
        in_specs=[
            pl.BlockSpec((tm, d), ctx_tile),
            pl.BlockSpec((tm, d), dec_tile),
            pl.BlockSpec((None, None, 6, d), lambda i: (layer, row(i), 0, 0)),
            pl.BlockSpec((None, 1, d), lambda i: (layer, 0, 0)),
            pl.BlockSpec((tm, 4 * d), lambda i: (i, 0)),
            pl.BlockSpec((tm, bw), ctx_tile),
            pl.BlockSpec((tm, bw), dec_tile),
            pl.BlockSpec((tm, 3 * bw), lambda i: (i, 0)),
            pl.BlockSpec((None, d, d), lambda i: (layer, 0, 0), pipeline_mode=once),
            pl.BlockSpec((None, d, d), lambda i: (layer, 0, 0), pipeline_mode=once),
        ],
        out_specs=[pl.BlockSpec((tm, d), lambda i: (i, 0))] * 2,
        out_shape=[jax.ShapeDtypeStruct((m, d), F32), jax.ShapeDtypeStruct((m, d), BF16)],
        compiler_params=_cparams(("parallel",)),
        name="merge",
    )(xa, xb, mod, norm2_g, gates, a_ctx, a_dec, bcd, wb, wo)


def _mlp_kernel(n_ctx_tiles, n_up, h_ref, x_ref, mod_ref, w1_ref, w2_ref, oa_ref, ob_ref, a_ref):
    s = pl.program_id(1)
    is_ctx = pl.program_id(0) < n_ctx_tiles
    tf = w1_ref.shape[1]

    @pl.when(s < n_up)
    def _():
        a = jnp.dot(h_ref[...], w1_ref[...], preferred_element_type=F32)
        a_ref[s] = jnp.square(jnp.maximum(a, 0.0)).astype(BF16)

    def down():
        acc = jnp.dot(a_ref[0], w2_ref[0:tf, :], preferred_element_type=F32)
        for c in range(1, n_up):
            acc = acc + jnp.dot(a_ref[c], w2_ref[c * tf:(c + 1) * tf, :], preferred_element_type=F32)
        return x_ref[...] + mod_ref[5:6, :] * acc

    @pl.when((s >= n_up) & is_ctx)
    def _():
        oa_ref[...] = down()

    @pl.when((s >= n_up) & jnp.logical_not(is_ctx))
    def _():
        ob_ref[...] = down()


def _mlp(h, x, mod, w1, w2, layer, n_ctx_rows, dec_seq):
    m, d = x.shape
    dff = w1.shape[2]
    tm, tf, tn = 1024, 1024, 256
    n_up, n_down = dff // tf, d // tn
    nct = n_ctx_rows // tm
    row = functools.partial(_mod_row, tm=tm, n_ctx_rows=n_ctx_rows, dec_seq=dec_seq)
    col = lambda s: jnp.maximum(s - n_up, 0)
    ctx_out = lambda i, s: (jnp.minimum(i, nct - 1), jnp.where(i < nct, col(s), n_down - 1))
    dec_out = lambda i, s: (jnp.maximum(i - nct, 0), jnp.where(i < nct, 0, col(s)))
    return pl.pallas_call(
        functools.partial(_mlp_kernel, nct, n_up),
        grid=(m // tm, n_up + n_down),
        in_specs=[
            pl.BlockSpec((tm, d), lambda i, s: (i, 0)),
            pl.BlockSpec((tm, tn), lambda i, s: (i, col(s))),
            pl.BlockSpec((None, None, 6, tn), lambda i, s: (layer, row(i), 0, col(s))),
            pl.BlockSpec((None, d, tf), lambda i, s: (layer, 0, jnp.minimum(s, n_up - 1))),
            pl.BlockSpec((None, dff, tn), lambda i, s: (layer, 0, col(s))),
        ],
        out_specs=[pl.BlockSpec((tm, tn), ctx_out), pl.BlockSpec((tm, tn), dec_out)],
        out_shape=[jax.ShapeDtypeStruct((n_ctx_rows, d), F32),
                   jax.ShapeDtypeStruct((m - n_ctx_rows, d), F32)],
        scratch_shapes=[pltpu.VMEM((n_up, tm, tf), BF16)],
        compiler_params=_cparams(("arbitrary", "arbitrary")),
        name="mlp",
    )(h, x, mod, w1, w2)


def _dft_tables(n_ctx_seq, n):
    def cs(size):
        k = np.arange(size)
        ang = 2.0 * np.pi * ((k[:, None] * k[None, :]) % size) / size
        return np.cos(ang) / np.sqrt(size), np.sin(ang) / np.sqrt(size)

    c_s, s_s = cs(n_ctx_seq)
    eye = np.eye(n // n_ctx_seq)
    c_l, s_l = cs(n)
    pos = np.stack([np.concatenate([np.kron(eye, c_s), np.kron(eye, s_s)], axis=0),
                    np.concatenate([c_l, s_l], axis=0)])
    c_c, s_c = cs(CHUNK)
    chan = np.concatenate([c_c, -s_c], axis=0)
    return jnp.asarray(pos, F32).astype(BF16), jnp.asarray(chan, F32).astype(BF16)


def _pool_tables(n_ctx_seq, n):
    def one(size):
        t = np.arange(size)
        bands, counts = [], []
        for w in POOL_WINDOWS:
            lo = np.clip(t - w // 2, 0, size - 1)
            hi = np.clip(t + w // 2 - 1, 0, size - 1)
            s = np.arange(size)[None, :]
            bands.append(((s >= lo[:, None]) & (s <= hi[:, None])).astype(np.float32))
            counts.append((hi - lo + 1).astype(np.float32))
        return np.stack(bands), np.stack(counts)

    b_s, c_s = one(n_ctx_seq)
    rep = n // n_ctx_seq
    b_l, c_l = one(n)
    band = np.stack([np.stack([np.kron(np.eye(rep), b) for b in b_s]), b_l])
    cnt = np.stack([np.tile(c_s, (1, rep)), c_l])
    cnt = np.repeat(cnt.transpose(0, 2, 1), CHUNK, axis=2)
    return jnp.asarray(band, BF16), jnp.asarray(cnt, F32)


def _rope_tables(n):
    t = jnp.arange(n)
    pos = jnp.stack([t // GRID_W, t % GRID_W], axis=-1).astype(F32)
    half = HEAD_DIM // 2
    inv = 1.0 / (ROPE_BASE ** (jnp.arange(0, half, 2, dtype=F32) / half))
    ang = pos[:, :, None] * inv
    cos, sin = jnp.cos(ang), jnp.sin(ang)
    cos_h = jnp.concatenate([cos[:, 0], cos[:, 0], cos[:, 1], cos[:, 1]], axis=-1)
    sin_h = jnp.concatenate([-sin[:, 0], sin[:, 0], -sin[:, 1], sin[:, 1]], axis=-1)
    return jnp.tile(cos_h, (1, 2)), jnp.tile(sin_h, (1, 2))


def _bias_table(rpb, n):
    n_layers, n_heads, n_d, n_e = rpb.shape
    cq = np.arange(GRID_W)
    cs = np.clip(cq - WIN_COLS // 2, 0, GRID_W - WIN_COLS)
    kc = np.arange(GRID_W)
    col_ok = (kc[None, :] >= cs[:, None]) & (kc[None, :] < cs[:, None] + WIN_COLS)
    col_mask = np.where(col_ok, 0.0, NEG_INF).astype(np.float32)
    dc_idx = np.clip(kc[None, :] - cq[:, None], -(WIN_COLS - 1), WIN_COLS - 1) + WIN_COLS - 1
    pick = (dc_idx[None, :, :] == np.arange(n_e)[:, None, None]).astype(np.float32)
    t = jnp.einsum("lhde,eck->lhcdk", rpb, jnp.asarray(pick), precision=lax.Precision.HIGHEST)
    t = t + jnp.asarray(col_mask)[:, None, :]
    flat = t.reshape(n_layers, n_heads, GRID_W, n_d * GRID_W)
    front = WIN_ROWS * GRID_W
    back = 2 * n - front - n_d * GRID_W
    return jnp.pad(flat, ((0, 0), (0, 0), (0, 0), (front, back)))


def kernel(x_prompt, x_sample, cache_k, cache_v, c, c_ctx, norm1_g, w_in, b_gate, q_norm_g, k_norm_g,
           rpb, gmlp_norm_g, w_spatial, b_spatial, w_pool, pool_scale, w_branch, w_out, norm2_g,
           w_mlp1, w_mlp2, w_ada, b_ada):
    n_b, seq, d = x_prompt.shape
    n_db, dec_seq, _ = x_sample.shape
    n_layers = w_in.shape[0]
    n_heads, past = cache_k.shape[2], cache_k.shape[3]
    bw = n_heads * HEAD_DIM
    n_ctx_rows = n_b * seq
    assert dec_seq == MIX_ROWS and MIX_ROWS % seq == 0 and n_ctx_rows % MIX_ROWS == 0
    assert dec_seq // GRID_W == 2 * WIN_ROWS and bw == 4 * LANES

    xa = x_prompt.reshape(n_ctx_rows, d)
    xb = x_sample.reshape(n_db * dec_seq, d)
    pad_rows = (-(1 + n_db)) % 8
    cond = jnp.concatenate([c_ctx[None, :], c, jnp.zeros((pad_rows, d), F32)], axis=0)
    mod = _ada(cond, w_ada, b_ada).reshape(n_layers, cond.shape[0], 6, d)

    w_in_b = w_in.astype(BF16)
    wb_b, wo_b = w_branch.astype(BF16), w_out.astype(BF16)
    w1_b, w2_b = w_mlp1.astype(BF16), w_mlp2.astype(BF16)
    ws_b, wp_b = w_spatial.astype(BF16), w_pool.astype(BF16)
    bs_x = jnp.repeat(b_spatial.transpose(0, 2, 1), CHUNK, axis=2)
    qg = jnp.tile(q_norm_g, (1, 2))[:, None, :]
    kg = jnp.tile(k_norm_g, (1, 2))[:, None, :]
    kc = cache_k.transpose(0, 1, 3, 2, 4).reshape(n_db, n_layers, past, bw).astype(BF16)
    vc = cache_v.transpose(0, 1, 3, 2, 4).reshape(n_db, n_layers, past, bw).astype(BF16)
    dft, chan = _dft_tables(seq, MIX_ROWS)
    band, cnt = _pool_tables(seq, MIX_ROWS)
    cos, sin = _rope_tables(dec_seq)
    tab = _bias_table(rpb, dec_seq)
    n1 = norm1_g[:, None, :]
    n2 = norm2_g[:, None, :]
    gn = gmlp_norm_g[:, None, :]
    ps = pool_scale[:, None, :]
    bg = b_gate[:, None, :]

    new_k, new_v = [], []
    for l in range(n_layers):
        p_arr, gates = _inproj(xa, xb, mod, n1, w_in_b, bg, l, 7 * bw, dec_seq)
        a_ctx, k_n, v_n = _attn_ctx(p_arr, qg, kg, l, n_b, seq)
        a_dec = _attn_dec(p_arr, qg, kg, cos, sin, kc, vc, tab, l, n_db, dec_seq, n_ctx_rows, past)
        bcd = _mix(p_arr, gn, ws_b, bs_x, dft, chan, band, cnt, wp_b, ps, l, n_ctx_rows)
        x1, h2 = _merge(xa, xb, mod, n2, gates, a_ctx, a_dec, bcd, wb_b, wo_b, l, dec_seq)
        xa, xb = _mlp(h2, x1, mod, w1_b, w2_b, l, n_ctx_rows, dec_seq)
        new_k.append(k_n.reshape(n_b, seq, n_heads, HEAD_DIM).transpose(0, 2, 1, 3))
        new_v.append(v_n.reshape(n_b, seq, n_heads, HEAD_DIM).transpose(0, 2, 1, 3))

    return (xa.reshape(n_b, seq, d), xb.reshape(n_db, dec_seq, d),
            jnp.stack(new_k, axis=1), jnp.stack(new_v, axis=1))
```

```python
import functools

import numpy as np
import jax
import jax.numpy as jnp
from jax import lax
from jax.experimental import pallas as pl
from jax.experimental.pallas import tpu as pltpu

F32 = jnp.float32
BF16 = jnp.bfloat16

GRID_W = 64
HEAD_DIM = 64
WIN_ROWS = 8
WIN_COLS = 16
ROPE_BASE = 10000.0
CHUNK = 128
POOL_WINDOWS = (2, 4, 8, 16)
EPS = 1e-6
NEG_INF = -1e30

LANES = 128
MIX_ROWS = 1024
VMEM_LIMIT = 56 * 1024 * 1024


def _cparams(sem):
    return pltpu.CompilerParams(dimension_semantics=sem, vmem_limit_bytes=VMEM_LIMIT)


def _mod_row(i, tm, n_ctx_rows, dec_seq):
    nc = n_ctx_rows // tm
    per = dec_seq // tm
    return jnp.where(i < nc, 0, 1 + (i - nc) // per)


def _ada_kernel(c_ref, w_ref, b_ref, o_ref):
    s = jax.nn.silu(c_ref[...]).astype(BF16)
    o_ref[...] = jnp.dot(s, w_ref[...].astype(BF16), preferred_element_type=F32) + b_ref[...]


def _ada(cond, w_ada, b_ada):
    n_layers, d, n = w_ada.shape
    rows = cond.shape[0]
    tn = 1024
    return pl.pallas_call(
        _ada_kernel,
        grid=(n_layers, n // tn),
        in_specs=[
            pl.BlockSpec((rows, d), lambda l, j: (0, 0)),
            pl.BlockSpec((None, d, tn), lambda l, j: (l, 0, j)),
            pl.BlockSpec((None, 1, tn), lambda l, j: (l, 0, j)),
        ],
        out_specs=pl.BlockSpec((None, rows, tn), lambda l, j: (l, 0, j)),
        out_shape=jax.ShapeDtypeStruct((n_layers, rows, n), F32),
        compiler_params=_cparams(("parallel", "parallel")),
        name="ada",
    )(cond, w_ada, b_ada.reshape(n_layers, 1, n))


def _modulated_norm(x, g, shift, scale):
    ms = jnp.mean(x * x, axis=-1, keepdims=True)
    return (x * lax.rsqrt(ms + EPS) * g) * (1.0 + scale) + shift


def _x_tile_copy(xa_hbm, xb_hbm, use_a, tile, nct, tm, xbuf, sem):
    if use_a:
        src = xa_hbm.at[pl.ds(tile * tm, tm), :]
    else:
        src = xb_hbm.at[pl.ds((tile - nct) * tm, tm), :]
    return pltpu.make_async_copy(src, xbuf, sem)


def _inproj_kernel(nct, n_a, n_g, xa_hbm, xb_hbm, mod_ref, modn_ref, g_ref, wa_ref, wg_ref, bg_ref, os_ref,
                   og_ref, xbuf, h0_ref, h1_ref, sem):
    i = pl.program_id(0)
    j = pl.program_id(1)
    n_tiles = pl.num_programs(0)
    tm = xbuf.shape[0]
    rc = tm // n_g
    slot = i % 2
    copy = functools.partial(_x_tile_copy, xa_hbm, xb_hbm, nct=nct, tm=tm, xbuf=xbuf, sem=sem)

    def start(tile):
        @pl.when(tile < nct)
        def _():
            copy(True, tile).start()

        @pl.when(tile >= nct)
        def _():
            copy(False, tile).start()

    @pl.when((i == 0) & (j == 0))
    def _():
        start(i)
        copy(True, 0).wait()
        h = _modulated_norm(xbuf[...], g_ref[...], mod_ref[0:1, :], mod_ref[1:2, :])
        h0_ref[...] = h.astype(BF16)

    @pl.when((j == 0) & (i + 1 < n_tiles))
    def _():
        start(i + 1)

    @pl.when((j == n_a) & (i + 1 < n_tiles))
    def _():
        copy(True, 0).wait()

    for cur, nxt in ((h0_ref, h1_ref), (h1_ref, h0_ref)):
        mine = slot == (0 if cur is h0_ref else 1)

        @pl.when((j < n_a) & mine)
        def _(cur=cur):
            os_ref[...] = jnp.dot(cur[...], wa_ref[...], preferred_element_type=F32).astype(BF16)

        @pl.when((j >= n_a) & mine)
        def _(cur=cur, nxt=nxt):
            rows = pl.ds(pl.multiple_of((j - n_a) * rc, rc), rc)
            hn = _modulated_norm(xbuf[rows, :], g_ref[...], modn_ref[0:1, :], modn_ref[1:2, :])
            nxt[rows, :] = hn.astype(BF16)
            logits = jnp.dot(cur[...], wg_ref[...], preferred_element_type=F32) + bg_ref[...]
            og_ref[...] = (0.5 * jnp.tanh(0.5 * logits) + 0.5).astype(BF16)


def _inproj(xa, xb, mod, norm_g, w_mix, w_gate, b_gate, layer, dec_seq):
    d = xa.shape[1]
    n_ctx_rows = xa.shape[0]
    m = n_ctx_rows + xb.shape[0]
    n_mix, n_gate = w_mix.shape[2], w_gate.shape[2]
    tm, ta, tg = 1024, 896, 1024
    n_a, n_g = n_mix // ta, n_gate // tg
    n_tiles = m // tm
    row = functools.partial(_mod_row, tm=tm, n_ctx_rows=n_ctx_rows, dec_seq=dec_seq)
    a_col = lambda j: jnp.minimum(j, n_a - 1)
    g_col = lambda j: jnp.maximum(j - n_a, 0)
    return pl.pallas_call(
        functools.partial(_inproj_kernel, n_ctx_rows // tm, n_a, n_g),
        grid=(n_tiles, n_a + n_g),
        in_specs=[
            pl.BlockSpec(memory_space=pl.ANY),
            pl.BlockSpec(memory_space=pl.ANY),
            pl.BlockSpec((None, None, 6, d), lambda i, j: (layer, row(i), 0, 0)),
            pl.BlockSpec((None, None, 6, d), lambda i, j: (layer, row(jnp.minimum(i + 1, n_tiles - 1)), 0, 0)),
            pl.BlockSpec((None, 1, d), lambda i, j: (layer, 0, 0)),
            pl.BlockSpec((None, d, ta), lambda i, j: (layer, 0, a_col(j))),
            pl.BlockSpec((None, d, tg), lambda i, j: (layer, 0, g_col(j))),
            pl.BlockSpec((None, 1, tg), lambda i, j: (layer, 0, g_col(j))),
        ],
        out_specs=[
            pl.BlockSpec((tm, ta), lambda i, j: (i, a_col(j))),
            pl.BlockSpec((tm, tg), lambda i, j: (i, g_col(j))),
        ],
        out_shape=[jax.ShapeDtypeStruct((m, n_mix), BF16), jax.ShapeDtypeStruct((m, n_gate), BF16)],
        scratch_shapes=[pltpu.VMEM((tm, d), F32), pltpu.VMEM((tm, d), BF16), pltpu.VMEM((tm, d), BF16),
                        pltpu.SemaphoreType.DMA(())],
        compiler_params=_cparams(("arbitrary", "arbitrary")),
        name="inproj",
    )(xa, xb, mod, mod, norm_g, w_mix, w_gate, b_gate)


def _head_norm(x, g, lane):
    row = lax.broadcasted_iota(jnp.int32, (LANES, LANES), 0)
    same_head = (row >= HEAD_DIM) == (lane >= HEAD_DIM)
    ones = jnp.where(same_head, 1.0, 0.0).astype(BF16)
    x2 = x * x
    hi = x2.astype(BF16)
    lo = (x2 - hi.astype(F32)).astype(BF16)
    ss = (jnp.dot(hi, ones, preferred_element_type=F32) + jnp.dot(lo, ones, preferred_element_type=F32))
    return x * lax.rsqrt(ss * (1.0 / HEAD_DIM) + EPS) * g


def _head_norm_xlu(x, g, lane):
    low = lane < HEAD_DIM
    x2 = x * x
    s_lo = jnp.sum(jnp.where(low, x2, 0.0), axis=-1, keepdims=True)
    s_hi = jnp.sum(jnp.where(low, 0.0, x2), axis=-1, keepdims=True)
    ss = jnp.where(low, s_lo, s_hi)
    return x * lax.rsqrt(ss * (1.0 / HEAD_DIM) + EPS) * g


def _rope(x, cos, sin_signed, lane):
    first = (lane & 16) == 0
    partner = jnp.where(first, pltpu.roll(x, LANES - 16, 1), pltpu.roll(x, 16, 1))
    return x * cos + partner * sin_signed


def _nt_dot(a, b):
    return lax.dot_general(a, b, (((1,), (1,)), ((), ())), preferred_element_type=F32)


def _head_mask(lane, h, value):
    return jnp.where((lane >= HEAD_DIM) == (h == 1), value, 0.0).astype(F32)


def _attn_ctx_kernel(q_ref, k_ref, v_ref, qg_ref, kg_ref, a_ref, kn_ref, vn_ref):
    lane = lax.broadcasted_iota(jnp.int32, (1, LANES), 1)
    scale = HEAD_DIM ** -0.5
    for p in range(q_ref.shape[1] // LANES):
        cols = slice(p * LANES, (p + 1) * LANES)
        q = _head_norm_xlu(q_ref[:, cols].astype(F32), qg_ref[...], lane)
        k = _head_norm_xlu(k_ref[:, cols].astype(F32), kg_ref[...], lane)
        v = v_ref[:, cols].astype(F32)
        kn_ref[:, cols] = k
        vn_ref[:, cols] = v
        kb = k.astype(BF16)
        acc = jnp.zeros(q.shape, F32)
        for h in range(2):
            s = _nt_dot((q * _head_mask(lane, h, scale)).astype(BF16), kb)
            e = jnp.exp(s - jnp.max(s, axis=-1, keepdims=True))
            o = jnp.dot(e.astype(BF16), (v * _head_mask(lane, h, 1.0)).astype(BF16),
                        preferred_element_type=F32)
            acc = acc + o / jnp.sum(e, axis=-1, keepdims=True)
        a_ref[:, cols] = acc.astype(BF16)


def _attn_ctx(p_arr, qg, kg, layer, n_batch, seq):
    w = 4 * LANES
    rows = n_batch * seq
    blk = lambda c: pl.BlockSpec((seq, w), lambda b: (b, c))
    gain = pl.BlockSpec((None, 1, LANES), lambda b: (layer, 0, 0))
    return pl.pallas_call(
        _attn_ctx_kernel,
        grid=(n_batch,),
        in_specs=[blk(0), blk(1), blk(2), gain, gain],
        out_specs=[pl.BlockSpec((seq, w), lambda b: (b, 0))] * 3,
        out_shape=[jax.ShapeDtypeStruct((rows, w), BF16),
                   jax.ShapeDtypeStruct((rows, w), F32),
                   jax.ShapeDtypeStruct((rows, w), F32)],
        compiler_params=_cparams(("parallel",)),
        name="attn_ctx",
    )(p_arr, p_arr, p_arr, qg, kg)


def _row_window(r, n_rows):
    rs = min(max(r - WIN_ROWS // 2, 0), n_rows - WIN_ROWS)
    if rs % 2 == 0:
        return rs * GRID_W, WIN_ROWS * GRID_W, True
    return (rs - 1) * GRID_W, (WIN_ROWS + 2) * GRID_W, False


def _attn_dec_kernel(q_ref, k_ref, v_ref, qg_ref, kg_ref, cos_ref, sin_ref, kc_ref, vc_ref,
                     tab_ref, o_ref, s_ref, sc_ref, p_ref, pc_ref, li_ref, tsh_ref):
    n = q_ref.shape[0]
    n_rows = n // GRID_W
    lane = lax.broadcasted_iota(jnp.int32, (1, LANES), 1)
    scale = HEAD_DIM ** -0.5
    q = _head_norm(q_ref[...].astype(F32), qg_ref[...], lane)
    k = _head_norm(k_ref[...].astype(F32), kg_ref[...], lane)
    qr = _rope(q, cos_ref[...], sin_ref[...], lane)
    kr = _rope(k, cos_ref[...], sin_ref[...], lane).astype(BF16)
    v = v_ref[...].astype(F32)
    kc = kc_ref[...]
    vc = vc_ref[...].astype(F32)
    wide = (WIN_ROWS + 2) * GRID_W
    wl = lax.broadcasted_iota(jnp.int32, (1, wide), 1)
    edge_mask = jnp.where((wl < GRID_W) | (wl >= wide - GRID_W), NEG_INF, 0.0).astype(F32)
    acc = jnp.zeros((n, LANES), F32)
    for h in range(2):
        tsh_ref[...] = pltpu.roll(tab_ref[h], 2 * n - GRID_W, 1)
        s_ref[...] = _nt_dot((qr * _head_mask(lane, h, scale)).astype(BF16), kr)
        sc_ref[...] = _nt_dot((q * _head_mask(lane, h, scale)).astype(BF16), kc)
        for r in range(n_rows):
            rows = pl.ds(r * GRID_W, GRID_W)
            a, w, exact = _row_window(r, n_rows)
            if (n_rows - 1 - r) % 2 == 0:
                bias = tab_ref[h, :, (n_rows - 1 - r) * GRID_W + a:(n_rows - 1 - r) * GRID_W + a + w]
            else:
                bias = tsh_ref[:, (n_rows - 2 - r) * GRID_W + a:(n_rows - 2 - r) * GRID_W + a + w]
            sl = s_ref[rows, a:a + w] + bias
            if not exact:
                sl = sl + edge_mask
            sx = sc_ref[rows, :]
            m = jnp.maximum(jnp.max(sl, axis=-1, keepdims=True), jnp.max(sx, axis=-1, keepdims=True))
            el = jnp.exp(sl - m)
            ex = jnp.exp(sx - m)
            den = jnp.sum(el, axis=-1, keepdims=True) + jnp.sum(ex, axis=-1, keepdims=True)
            pieces = [el.astype(BF16)]
            if a > 0:
                pieces.insert(0, jnp.zeros((GRID_W, a), BF16))
            if a + w < n:
                pieces.append(jnp.zeros((GRID_W, n - a - w), BF16))
            p_ref[rows, :] = jnp.concatenate(pieces, axis=1) if len(pieces) > 1 else pieces[0]
            pc_ref[rows, :] = ex.astype(BF16)
            li_ref[rows, :] = jnp.broadcast_to(1.0 / den, (GRID_W, LANES))
        vm = _head_mask(lane, h, 1.0)
        o = jnp.dot(p_ref[...], (v * vm).astype(BF16), preferred_element_type=F32)
        o = o + jnp.dot(pc_ref[...], (vc * vm).astype(BF16), preferred_element_type=F32)
        acc = acc + o * li_ref[...]
    o_ref[...] = acc.astype(BF16)


def _attn_dec(p_arr, qg, kg, cos, sin, kc, vc, tab, layer, n_batch, seq, row0, past):
    rb0 = row0 // seq
    n_pairs = 4
    blk = lambda c: pl.BlockSpec((seq, LANES), lambda b, p: (rb0 + b, c + p))
    gain = pl.BlockSpec((None, 1, LANES), lambda b, p: (layer, 0, 0))
    const = pl.BlockSpec((seq, LANES), lambda b, p: (0, 0))
    cache = pl.BlockSpec((None, None, past, LANES), lambda b, p: (b, layer, 0, p))
    return pl.pallas_call(
        _attn_dec_kernel,
        grid=(n_batch, n_pairs),
        in_specs=[
            blk(0), blk(n_pairs), blk(2 * n_pairs), gain, gain, const, const, cache, cache,
            pl.BlockSpec((None, 2, GRID_W, 2 * seq), lambda b, p: (layer, p, 0, 0)),
        ],
        out_specs=pl.BlockSpec((seq, LANES), lambda b, p: (b, p)),
        out_shape=jax.ShapeDtypeStruct((n_batch * seq, n_pairs * LANES), BF16),
        scratch_shapes=[
            pltpu.VMEM((seq, seq), F32), pltpu.VMEM((seq, past), F32),
            pltpu.VMEM((seq, seq), BF16), pltpu.VMEM((seq, past), BF16),
            pltpu.VMEM((seq, LANES), F32), pltpu.VMEM((GRID_W, 2 * seq), F32),
        ],
        compiler_params=_cparams(("parallel", "parallel")),
        name="attn_dec",
    )(p_arr, p_arr, p_arr, qg, kg, cos, sin, kc, vc, tab)


def _mix_kernel(u_ref, vg_ref, xf_ref, xp_ref, gn_ref, ws_ref, bs_ref, dft_ref, cs_ref, band_ref,
                cnt_ref, wp_ref, ps_ref, o_ref, vgn_ref, ab_ref):
    n = u_ref.shape[0]
    bw = u_ref.shape[1]
    n_groups = bw // CHUNK
    for c in range(n // CHUNK):
        rows = pl.ds(c * CHUNK, CHUNK)
        vg = jax.nn.gelu(vg_ref[rows, :].astype(F32))
        ms = jnp.mean(vg * vg, axis=-1, keepdims=True)
        vgn_ref[rows, :] = (vg * lax.rsqrt(ms + EPS) * gn_ref[...]).astype(BF16)
    for c in range(n // CHUNK):
        rows = pl.ds(c * CHUNK, CHUNK)
        parts = [jnp.dot(ws_ref[g], vgn_ref[rows, g * CHUNK:(g + 1) * CHUNK], preferred_element_type=F32)
                 for g in range(n_groups)]
        mixed = jnp.concatenate(parts, axis=1) + bs_ref[...]
        o_ref[rows, 0:bw] = (jax.nn.gelu(u_ref[rows, :].astype(F32)) * mixed).astype(BF16)
    ab_ref[...] = jnp.dot(dft_ref[...], xf_ref[...], preferred_element_type=F32).astype(BF16)
    for g in range(n_groups):
        gc = slice(g * CHUNK, (g + 1) * CHUNK)
        lhs = jnp.concatenate([ab_ref[0:n, gc], ab_ref[n:2 * n, gc]], axis=1)
        o_ref[:, bw + g * CHUNK:bw + (g + 1) * CHUNK] = jnp.dot(
            lhs, cs_ref[...], preferred_element_type=F32).astype(BF16)
    for g in range(n_groups):
        gc = slice(g * CHUNK, (g + 1) * CHUNK)
        xg = xp_ref[:, gc]
        wsum = jnp.dot(band_ref[g], xg, preferred_element_type=F32)
        pooled = wsum / cnt_ref[:, gc] - xg.astype(F32)
        y = jnp.dot(pooled.astype(BF16), wp_ref[g], preferred_element_type=F32) * ps_ref[:, gc]
        o_ref[:, 2 * bw + g * CHUNK:2 * bw + (g + 1) * CHUNK] = y.astype(BF16)


def _mix(p_arr, gn, ws, bs, dft, cs, band, cnt, wp, ps, layer, n_ctx_rows):
    m = p_arr.shape[0]
    bw = gn.shape[-1]
    n = MIX_ROWS
    n_groups = bw // CHUNK
    kind = lambda i: jnp.where(i < n_ctx_rows // n, 0, 1)
    blk = lambda c: pl.BlockSpec((n, bw), lambda i: (i, c))
    once = pl.Buffered(1)
    return pl.pallas_call(
        _mix_kernel,
        grid=(m // n,),
        in_specs=[
            blk(3), blk(4), blk(5), blk(6),
            pl.BlockSpec((None, 1, bw), lambda i: (layer, 0, 0)),
            pl.BlockSpec((None, n_groups, CHUNK, CHUNK), lambda i: (layer, 0, 0, 0)),
            pl.BlockSpec((None, CHUNK, bw), lambda i: (layer, 0, 0)),
            pl.BlockSpec((None, 2 * n, n), lambda i: (kind(i), 0, 0), pipeline_mode=once),
            pl.BlockSpec((2 * CHUNK, CHUNK), lambda i: (0, 0)),
            pl.BlockSpec((None, n_groups, n, n), lambda i: (kind(i), 0, 0, 0), pipeline_mode=once),
            pl.BlockSpec((None, n, bw), lambda i: (kind(i), 0, 0), pipeline_mode=once),
            pl.BlockSpec((None, n_groups, CHUNK, CHUNK), lambda i: (layer, 0, 0, 0)),
            pl.BlockSpec((None, 1, bw), lambda i: (layer, 0, 0)),
        ],
        out_specs=pl.BlockSpec((n, 3 * bw), lambda i: (i, 0)),
        out_shape=jax.ShapeDtypeStruct((m, 3 * bw), BF16),
        scratch_shapes=[pltpu.VMEM((n, bw), BF16), pltpu.VMEM((2 * n, bw), BF16)],
        compiler_params=_cparams(("parallel",)),
        name="mix",
    )(p_arr, p_arr, p_arr, p_arr, gn, ws, bs, dft, cs, band, cnt, wp, ps)


def _merge_kernel(n_ctx_tiles, xa_ref, xb_ref, mod_ref, g2_ref, gate_ref, actx_ref, adec_ref, bcd_ref,
                  wb_ref, wo_ref, o_ref, h_ref):
    d = xa_ref.shape[1]
    bw = actx_ref.shape[1]
    is_ctx = pl.program_id(0) < n_ctx_tiles
    merged = jnp.zeros(xa_ref.shape, F32)
    for b in range(d // bw):
        if b == 0:
            br = jnp.where(is_ctx, actx_ref[...], adec_ref[...])
        else:
            br = bcd_ref[:, (b - 1) * bw:b * bw]
        y = jnp.dot(br, wb_ref[b * bw:(b + 1) * bw, :], preferred_element_type=F32)
        merged = merged + gate_ref[:, b * d:(b + 1) * d].astype(F32) * y
    mix = jnp.dot(merged.astype(BF16), wo_ref[...], preferred_element_type=F32)
    x = jnp.where(is_ctx, xa_ref[...], xb_ref[...])
    x1 = x + mod_ref[2:3, :] * mix
    o_ref[...] = x1
    h_ref[...] = _modulated_norm(x1, g2_ref[...], mod_ref[3:4, :], mod_ref[4:5, :]).astype(BF16)


def _merge(xa, xb, mod, norm2_g, gates, a_ctx, a_dec, bcd, wb, wo, layer, dec_seq):
    d = xa.shape[1]
    n_ctx_rows = xa.shape[0]
    m = n_ctx_rows + xb.shape[0]
    bw = a_ctx.shape[1]
    tm = 256
    nct = n_ctx_rows // tm
    row = functools.partial(_mod_row, tm=tm, n_ctx_rows=n_ctx_rows, dec_seq=dec_seq)
    once = pl.Buffered(1)
    ctx_tile = lambda i: (jnp.minimum(i, nct - 1), 0)
    dec_tile = lambda i: (jnp.maximum(i - nct, 0), 0)
    return pl.pallas_call(
        functools.partial(_merge_kernel, nct),
        grid=(m // tm,),
        in_specs=[
            pl.BlockSpec((tm, d), ctx_tile),
            pl.BlockSpec((tm, d), dec_tile),
            pl.BlockSpec((None, None, 6, d), lambda i: (layer, row(i), 0, 0)),
            pl.BlockSpec((None, 1, d), lambda i: (layer, 0, 0)),
            pl.BlockSpec((tm, 4 * d), lambda i: (i, 0)),
            pl.BlockSpec((tm, bw), ctx_tile),
            pl.BlockSpec((tm, bw), dec_tile),
            pl.BlockSpec((tm, 3 * bw), lambda i: (i, 0)),
            pl.BlockSpec((None, d, d), lambda i: (layer, 0, 0), pipeline_mode=once),
            pl.BlockSpec((None, d, d), lambda i: (layer, 0, 0), pipeline_mode=once),
        ],
        out_specs=[pl.BlockSpec((tm, d), lambda i: (i, 0))] * 2,
        out_shape=[jax.ShapeDtypeStruct((m, d), F32), jax.ShapeDtypeStruct((m, d), BF16)],
        compiler_params=_cparams(("parallel",)),
        name="merge",
    )(xa, xb, mod, norm2_g, gates, a_ctx, a_dec, bcd, wb, wo)


def _mlp_kernel(n_ctx_tiles, n_up, h_ref, x_ref, mod_ref, w1_ref, w2_ref, oa_ref, ob_ref, a_ref):
    s = pl.program_id(1)
    is_ctx = pl.program_id(0) < n_ctx_tiles
    tf = w1_ref.shape[1]

    @pl.when(s < n_up)
    def _():
        a = jnp.dot(h_ref[...], w1_ref[...], preferred_element_type=F32)
        a_ref[s] = jnp.square(jnp.maximum(a, 0.0)).astype(BF16)

    def down():
        acc = jnp.dot(a_ref[0], w2_ref[0:tf, :], preferred_element_type=F32)
        for c in range(1, n_up):
            acc = acc + jnp.dot(a_ref[c], w2_ref[c * tf:(c + 1) * tf, :], preferred_element_type=F32)
        return x_ref[...] + mod_ref[5:6, :] * acc

    @pl.when((s >= n_up) & is_ctx)
    def _():
        oa_ref[...] = down()

    @pl.when((s >= n_up) & jnp.logical_not(is_ctx))
    def _():
        ob_ref[...] = down()


def _mlp(h, x, mod, w1, w2, layer, n_ctx_rows, dec_seq):
    m, d = x.shape
    dff = w1.shape[2]
    tm, tf, tn = 1024, 1024, 256
    n_up, n_down = dff // tf, d // tn
    nct = n_ctx_rows // tm
    row = functools.partial(_mod_row, tm=tm, n_ctx_rows=n_ctx_rows, dec_seq=dec_seq)
    col = lambda s: jnp.maximum(s - n_up, 0)
    ctx_out = lambda i, s: (jnp.minimum(i, nct - 1), jnp.where(i < nct, col(s), n_down - 1))
    dec_out = lambda i, s: (jnp.maximum(i - nct, 0), jnp.where(i < nct, 0, col(s)))
    return pl.pallas_call(
        functools.partial(_mlp_kernel, nct, n_up),
        grid=(m // tm, n_up + n_down),
        in_specs=[
            pl.BlockSpec((tm, d), lambda i, s: (i, 0)),
            pl.BlockSpec((tm, tn), lambda i, s: (i, col(s))),
            pl.BlockSpec((None, None, 6, tn), lambda i, s: (layer, row(i), 0, col(s))),
            pl.BlockSpec((None, d, tf), lambda i, s: (layer, 0, jnp.minimum(s, n_up - 1))),
            pl.BlockSpec((None, dff, tn), lambda i, s: (layer, 0, col(s))),
        ],
        out_specs=[pl.BlockSpec((tm, tn), ctx_out), pl.BlockSpec((tm, tn), dec_out)],
        out_shape=[jax.ShapeDtypeStruct((n_ctx_rows, d), F32),
                   jax.ShapeDtypeStruct((m - n_ctx_rows, d), F32)],
        scratch_shapes=[pltpu.VMEM((n_up, tm, tf), BF16)],
        compiler_params=_cparams(("arbitrary", "arbitrary")),
        name="mlp",
    )(h, x, mod, w1, w2)


def _dft_tables(n_ctx_seq, n):
    def cs(size):
        k = np.arange(size)
        ang = 2.0 * np.pi * ((k[:, None] * k[None, :]) % size) / size
        return np.cos(ang) / np.sqrt(size), np.sin(ang) / np.sqrt(size)

    c_s, s_s = cs(n_ctx_seq)
    eye = np.eye(n // n_ctx_seq)
    c_l, s_l = cs(n)
    pos = np.stack([np.concatenate([np.kron(eye, c_s), np.kron(eye, s_s)], axis=0),
                    np.concatenate([c_l, s_l], axis=0)])
    c_c, s_c = cs(CHUNK)
    chan = np.concatenate([c_c, -s_c], axis=0)
    return jnp.asarray(pos, F32).astype(BF16), jnp.asarray(chan, F32).astype(BF16)


def _pool_tables(n_ctx_seq, n):
    def one(size):
        t = np.arange(size)
        bands, counts = [], []
        for w in POOL_WINDOWS:
            lo = np.clip(t - w // 2, 0, size - 1)
            hi = np.clip(t + w // 2 - 1, 0, size - 1)
            s = np.arange(size)[None, :]
            bands.append(((s >= lo[:, None]) & (s <= hi[:, None])).astype(np.float32))
            counts.append((hi - lo + 1).astype(np.float32))
        return np.stack(bands), np.stack(counts)

    b_s, c_s = one(n_ctx_seq)
    rep = n // n_ctx_seq
    b_l, c_l = one(n)
    band = np.stack([np.stack([np.kron(np.eye(rep), b) for b in b_s]), b_l])
    cnt = np.stack([np.tile(c_s, (1, rep)), c_l])
    cnt = np.repeat(cnt.transpose(0, 2, 1), CHUNK, axis=2)
    return jnp.asarray(band, BF16), jnp.asarray(cnt, F32)


def _rope_tables(n):
    t = jnp.arange(n)
    pos = jnp.stack([t // GRID_W, t % GRID_W], axis=-1).astype(F32)
    half = HEAD_DIM // 2
    inv = 1.0 / (ROPE_BASE ** (jnp.arange(0, half, 2, dtype=F32) / half))
    ang = pos[:, :, None] * inv
    cos, sin = jnp.cos(ang), jnp.sin(ang)
    cos_h = jnp.concatenate([cos[:, 0], cos[:, 0], cos[:, 1], cos[:, 1]], axis=-1)
    sin_h = jnp.concatenate([-sin[:, 0], sin[:, 0], -sin[:, 1], sin[:, 1]], axis=-1)
    return jnp.tile(cos_h, (1, 2)), jnp.tile(sin_h, (1, 2))


def _bias_table(rpb, n):
    n_layers, n_heads, n_d, n_e = rpb.shape
    cq = np.arange(GRID_W)
    cs = np.clip(cq - WIN_COLS // 2, 0, GRID_W - WIN_COLS)
    kc = np.arange(GRID_W)
    col_ok = (kc[None, :] >= cs[:, None]) & (kc[None, :] < cs[:, None] + WIN_COLS)
    col_mask = np.where(col_ok, 0.0, NEG_INF).astype(np.float32)
    dc_idx = np.clip(kc[None, :] - cq[:, None], -(WIN_COLS - 1), WIN_COLS - 1) + WIN_COLS - 1
    pick = (dc_idx[None, :, :] == np.arange(n_e)[:, None, None]).astype(np.float32)
    t = jnp.einsum("lhde,eck->lhcdk", rpb, jnp.asarray(pick), precision=lax.Precision.HIGHEST)
    t = t + jnp.asarray(col_mask)[:, None, :]
    flat = t.reshape(n_layers, n_heads, GRID_W, n_d * GRID_W)
    front = WIN_ROWS * GRID_W
    back = 2 * n - front - n_d * GRID_W
    return jnp.pad(flat, ((0, 0), (0, 0), (0, 0), (front, back)))


def kernel(x_prompt, x_sample, cache_k, cache_v, c, c_ctx, norm1_g, w_in, b_gate, q_norm_g, k_norm_g,
           rpb, gmlp_norm_g, w_spatial, b_spatial, w_pool, pool_scale, w_branch, w_out, norm2_g,
           w_mlp1, w_mlp2, w_ada, b_ada):
    n_b, seq, d = x_prompt.shape
    n_db, dec_seq, _ = x_sample.shape
    n_layers = w_in.shape[0]
    n_heads, past = cache_k.shape[2], cache_k.shape[3]
    bw = n_heads * HEAD_DIM
    n_ctx_rows = n_b * seq
    assert dec_seq == MIX_ROWS and MIX_ROWS % seq == 0 and n_ctx_rows % MIX_ROWS == 0
    assert dec_seq // GRID_W == 2 * WIN_ROWS and bw == 4 * LANES

    xa = x_prompt.reshape(n_ctx_rows, d)
    xb = x_sample.reshape(n_db * dec_seq, d)
    pad_rows = (-(1 + n_db)) % 8
    cond = jnp.concatenate([c_ctx[None, :], c, jnp.zeros((pad_rows, d), F32)], axis=0)
    mod = _ada(cond, w_ada, b_ada).reshape(n_layers, cond.shape[0], 6, d)

    n_mix_cols = 7 * bw
    w_mix_b = w_in[:, :, :n_mix_cols].astype(BF16)
    w_gate_b = w_in[:, :, n_mix_cols:].astype(BF16)
    wb_b, wo_b = w_branch.astype(BF16), w_out.astype(BF16)
    w1_b, w2_b = w_mlp1.astype(BF16), w_mlp2.astype(BF16)
    ws_b, wp_b = w_spatial.astype(BF16), w_pool.astype(BF16)
    bs_x = jnp.repeat(b_spatial.transpose(0, 2, 1), CHUNK, axis=2)
    qg = jnp.tile(q_norm_g, (1, 2))[:, None, :]
    kg = jnp.tile(k_norm_g, (1, 2))[:, None, :]
    kc = cache_k.transpose(0, 1, 3, 2, 4).reshape(n_db, n_layers, past, bw).astype(BF16)
    vc = cache_v.transpose(0, 1, 3, 2, 4).reshape(n_db, n_layers, past, bw).astype(BF16)
    dft, chan = _dft_tables(seq, MIX_ROWS)
    band, cnt = _pool_tables(seq, MIX_ROWS)
    cos, sin = _rope_tables(dec_seq)
    tab = _bias_table(rpb, dec_seq)
    n1 = norm1_g[:, None, :]
    n2 = norm2_g[:, None, :]
    gn = gmlp_norm_g[:, None, :]
    ps = pool_scale[:, None, :]
    bg = b_gate[:, None, :]

    new_k, new_v = [], []
    for l in range(n_layers):
        p_arr, gates = _inproj(xa, xb, mod, n1, w_mix_b, w_gate_b, bg, l, dec_seq)
        a_ctx, k_n, v_n = _attn_ctx(p_arr, qg, kg, l, n_b, seq)
        a_dec = _attn_dec(p_arr, qg, kg, cos, sin, kc, vc, tab, l, n_db, dec_seq, n_ctx_rows, past)
        bcd = _mix(p_arr, gn, ws_b, bs_x, dft, chan, band, cnt, wp_b, ps, l, n_ctx_rows)
        x1, h2 = _merge(xa, xb, mod, n2, gates, a_ctx, a_dec, bcd, wb_b, wo_b, l, dec_seq)
        xa, xb = _mlp(h2, x1, mod, w1_b, w2_b, l, n_ctx_rows, dec_seq)
        new_k.append(k_n.reshape(n_b, seq, n_heads, HEAD_DIM).transpose(0, 2, 1, 3))
        new_v.append(v_n.reshape(n_b, seq, n_heads, HEAD_DIM).transpose(0, 2, 1, 3))

    return (xa.reshape(n_b, seq, d), xb.reshape(n_db, dec_seq, d),
            jnp.stack(new_k, axis=1), jnp.stack(new_v, axis=1))
```

```python
import functools

import numpy as np
import jax
import jax.numpy as jnp
from jax import lax
from jax.experimental import pallas as pl
from jax.experimental.pallas import tpu as pltpu

F32 = jnp.float32
BF16 = jnp.bfloat16

GRID_W = 64
HEAD_DIM = 64
WIN_ROWS = 8
WIN_COLS = 16
ROPE_BASE = 10000.0
CHUNK = 128
POOL_WINDOWS = (2, 4, 8, 16)
EPS = 1e-6
NEG_INF = -1e30

LANES = 128
MIX_ROWS = 1024
VMEM_LIMIT = 56 * 1024 * 1024


def _cparams(sem):
    return pltpu.CompilerParams(dimension_semantics=sem, vmem_limit_bytes=VMEM_LIMIT)


def _mod_row(i, tm, n_ctx_rows, dec_seq):
    nc = n_ctx_rows // tm
    per = dec_seq // tm
    return jnp.where(i < nc, 0, 1 + (i - nc) // per)


def _ada_kernel(c_ref, w_ref, b_ref, o_ref):
    s = jax.nn.silu(c_ref[...]).astype(BF16)
    o_ref[...] = jnp.dot(s, w_ref[...].astype(BF16), preferred_element_type=F32) + b_ref[...]


def _ada(cond, w_ada, b_ada):
    n_layers, d, n = w_ada.shape
    rows = cond.shape[0]
    tn = 1024
    return pl.pallas_call(
        _ada_kernel,
        grid=(n_layers, n // tn),
        in_specs=[
            pl.BlockSpec((rows, d), lambda l, j: (0, 0)),
            pl.BlockSpec((None, d, tn), lambda l, j: (l, 0, j)),
            pl.BlockSpec((None, 1, tn), lambda l, j: (l, 0, j)),
        ],
        out_specs=pl.BlockSpec((None, rows, tn), lambda l, j: (l, 0, j)),
        out_shape=jax.ShapeDtypeStruct((n_layers, rows, n), F32),
        compiler_params=_cparams(("parallel", "parallel")),
        name="ada",
    )(cond, w_ada, b_ada.reshape(n_layers, 1, n))


def _modulated_norm(x, g, shift, scale):
    ms = jnp.mean(x * x, axis=-1, keepdims=True)
    return (x * lax.rsqrt(ms + EPS) * g) * (1.0 + scale) + shift


def _x_tile_copy(xa_hbm, xb_hbm, use_a, tile, nct, tm, xbuf, sem):
    if use_a:
        src = xa_hbm.at[pl.ds(tile * tm, tm), :]
    else:
        src = xb_hbm.at[pl.ds((tile - nct) * tm, tm), :]
    return pltpu.make_async_copy(src, xbuf, sem)


def _inproj_kernel(nct, n_a, n_g, xa_hbm, xb_hbm, mod_ref, modn_ref, g_ref, wa_ref, wgl_ref, wgh_ref, bg_ref,
                   os_ref, og_ref, xbuf, h0_ref, h1_ref, sem):
    i = pl.program_id(0)
    j = pl.program_id(1)
    n_tiles = pl.num_programs(0)
    tm = xbuf.shape[0]
    rc = tm // n_g
    slot = i % 2
    copy = functools.partial(_x_tile_copy, xa_hbm, xb_hbm, nct=nct, tm=tm, xbuf=xbuf, sem=sem)

    def start(tile):
        @pl.when(tile < nct)
        def _():
            copy(True, tile).start()

        @pl.when(tile >= nct)
        def _():
            copy(False, tile).start()

    @pl.when((i == 0) & (j == 0))
    def _():
        start(i)
        copy(True, 0).wait()
        h = _modulated_norm(xbuf[...], g_ref[...], mod_ref[0:1, :], mod_ref[1:2, :])
        h0_ref[...] = h.astype(BF16)

    @pl.when((j == 0) & (i + 1 < n_tiles))
    def _():
        start(i + 1)

    @pl.when((j == n_a) & (i + 1 < n_tiles))
    def _():
        copy(True, 0).wait()

    for cur, nxt in ((h0_ref, h1_ref), (h1_ref, h0_ref)):
        mine = slot == (0 if cur is h0_ref else 1)

        @pl.when((j < n_a) & mine)
        def _(cur=cur):
            os_ref[...] = jnp.dot(cur[...], wa_ref[...], preferred_element_type=F32).astype(BF16)

        @pl.when((j >= n_a) & mine)
        def _(cur=cur, nxt=nxt):
            rows = pl.ds(pl.multiple_of((j - n_a) * rc, rc), rc)
            hn = _modulated_norm(xbuf[rows, :], g_ref[...], modn_ref[0:1, :], modn_ref[1:2, :])
            nxt[rows, :] = hn.astype(BF16)
            half = wgl_ref.shape[1]
            for part, w_ref in enumerate((wgl_ref, wgh_ref)):
                cols = slice(part * half, (part + 1) * half)
                logits = jnp.dot(cur[...], w_ref[...], preferred_element_type=F32) + bg_ref[:, cols]
                og_ref[:, cols] = (0.5 * jnp.tanh(0.5 * logits) + 0.5).astype(BF16)


def _inproj(xa, xb, mod, norm_g, w, b_gate, layer, n_mix, dec_seq):
    d = xa.shape[1]
    n_ctx_rows = xa.shape[0]
    m = n_ctx_rows + xb.shape[0]
    n_gate = w.shape[2] - n_mix
    tm, ta, tg = 1024, 896, 1024
    half = tg // 2
    assert n_mix % ta == 0 and n_mix % half == 0 and n_gate % tg == 0
    n_a, n_g = n_mix // ta, n_gate // tg
    n_tiles = m // tm
    row = functools.partial(_mod_row, tm=tm, n_ctx_rows=n_ctx_rows, dec_seq=dec_seq)
    a_col = lambda j: jnp.minimum(j, n_a - 1)
    g_col = lambda j: jnp.maximum(j - n_a, 0)
    return pl.pallas_call(
        functools.partial(_inproj_kernel, n_ctx_rows // tm, n_a, n_g),
        grid=(n_tiles, n_a + n_g),
        in_specs=[
            pl.BlockSpec(memory_space=pl.ANY),
            pl.BlockSpec(memory_space=pl.ANY),
            pl.BlockSpec((None, None, 6, d), lambda i, j: (layer, row(i), 0, 0)),
            pl.BlockSpec((None, None, 6, d), lambda i, j: (layer, row(jnp.minimum(i + 1, n_tiles - 1)), 0, 0)),
            pl.BlockSpec((None, 1, d), lambda i, j: (layer, 0, 0)),
            pl.BlockSpec((None, d, ta), lambda i, j: (layer, 0, a_col(j))),
            pl.BlockSpec((None, d, half), lambda i, j: (layer, 0, n_mix // half + 2 * g_col(j))),
            pl.BlockSpec((None, d, half), lambda i, j: (layer, 0, n_mix // half + 2 * g_col(j) + 1)),
            pl.BlockSpec((None, 1, tg), lambda i, j: (layer, 0, g_col(j))),
        ],
        out_specs=[
            pl.BlockSpec((tm, ta), lambda i, j: (i, a_col(j))),
            pl.BlockSpec((tm, tg), lambda i, j: (i, g_col(j))),
        ],
        out_shape=[jax.ShapeDtypeStruct((m, n_mix), BF16), jax.ShapeDtypeStruct((m, n_gate), BF16)],
        scratch_shapes=[pltpu.VMEM((tm, d), F32), pltpu.VMEM((tm, d), BF16), pltpu.VMEM((tm, d), BF16),
                        pltpu.SemaphoreType.DMA(())],
        compiler_params=_cparams(("arbitrary", "arbitrary")),
        name="inproj",
    )(xa, xb, mod, mod, norm_g, w, w, w, b_gate)


def _head_norm(x, g, lane):
    row = lax.broadcasted_iota(jnp.int32, (LANES, LANES), 0)
    same_head = (row >= HEAD_DIM) == (lane >= HEAD_DIM)
    ones = jnp.where(same_head, 1.0, 0.0).astype(BF16)
    x2 = x * x
    hi = x2.astype(BF16)
    lo = (x2 - hi.astype(F32)).astype(BF16)
    ss = (jnp.dot(hi, ones, preferred_element_type=F32) + jnp.dot(lo, ones, preferred_element_type=F32))
    return x * lax.rsqrt(ss * (1.0 / HEAD_DIM) + EPS) * g


def _head_norm_xlu(x, g, lane):
    low = lane < HEAD_DIM
    x2 = x * x
    s_lo = jnp.sum(jnp.where(low, x2, 0.0), axis=-1, keepdims=True)
    s_hi = jnp.sum(jnp.where(low, 0.0, x2), axis=-1, keepdims=True)
    ss = jnp.where(low, s_lo, s_hi)
    return x * lax.rsqrt(ss * (1.0 / HEAD_DIM) + EPS) * g


def _rope(x, cos, sin_signed, lane):
    first = (lane & 16) == 0
    partner = jnp.where(first, pltpu.roll(x, LANES - 16, 1), pltpu.roll(x, 16, 1))
    return x * cos + partner * sin_signed


def _nt_dot(a, b):
    return lax.dot_general(a, b, (((1,), (1,)), ((), ())), preferred_element_type=F32)


def _head_mask(lane, h, value):
    return jnp.where((lane >= HEAD_DIM) == (h == 1), value, 0.0).astype(F32)


def _attn_ctx_kernel(first, q_ref, k_ref, v_ref, qg_ref, kg_ref, *rest):
    a_ref, kn_ref, vn_ref = rest[-3:]
    if first:
        for later in range(1, kn_ref.shape[0]):
            kn_ref[later] = jnp.zeros(kn_ref.shape[1:], F32)
            vn_ref[later] = jnp.zeros(vn_ref.shape[1:], F32)
        kn_ref, vn_ref = kn_ref.at[0], vn_ref.at[0]
    lane = lax.broadcasted_iota(jnp.int32, (1, LANES), 1)
    scale = HEAD_DIM ** -0.5
    for p in range(q_ref.shape[1] // LANES):
        cols = slice(p * LANES, (p + 1) * LANES)
        q = _head_norm_xlu(q_ref[:, cols].astype(F32), qg_ref[...], lane)
        k = _head_norm_xlu(k_ref[:, cols].astype(F32), kg_ref[...], lane)
        v = v_ref[:, cols].astype(F32)
        for h in range(2):
            kn_ref[2 * p + h] = k[:, h * HEAD_DIM:(h + 1) * HEAD_DIM]
            vn_ref[2 * p + h] = v[:, h * HEAD_DIM:(h + 1) * HEAD_DIM]
        kb = k.astype(BF16)
        acc = jnp.zeros(q.shape, F32)
        for h in range(2):
            s = _nt_dot((q * _head_mask(lane, h, scale)).astype(BF16), kb)
            e = jnp.exp(s - jnp.max(s, axis=-1, keepdims=True))
            o = jnp.dot(e.astype(BF16), (v * _head_mask(lane, h, 1.0)).astype(BF16),
                        preferred_element_type=F32)
            acc = acc + o / jnp.sum(e, axis=-1, keepdims=True)
        a_ref[:, cols] = acc.astype(BF16)


def _attn_ctx(p_arr, qg, kg, layer, n_layers, n_batch, seq, new_kv):
    w = 4 * LANES
    rows = n_batch * seq
    n_heads = w // HEAD_DIM
    blk = lambda c: pl.BlockSpec((seq, w), lambda b: (b, c))
    gain = pl.BlockSpec((None, 1, LANES), lambda b: (layer, 0, 0))
    first = new_kv is None
    assert first == (layer == 0)
    if first:
        cache = pl.BlockSpec((None, n_layers, n_heads, seq, HEAD_DIM), lambda b: (b, 0, 0, 0, 0))
    else:
        cache = pl.BlockSpec((None, None, n_heads, seq, HEAD_DIM), lambda b: (b, layer, 0, 0, 0))
    cache_shape = jax.ShapeDtypeStruct((n_batch, n_layers, n_heads, seq, HEAD_DIM), F32)
    carried = [] if first else list(new_kv)
    return pl.pallas_call(
        functools.partial(_attn_ctx_kernel, first),
        grid=(n_batch,),
        in_specs=[blk(0), blk(1), blk(2), gain, gain] + [pl.BlockSpec(memory_space=pl.ANY)] * len(carried),
        out_specs=[pl.BlockSpec((seq, w), lambda b: (b, 0)), cache, cache],
        out_shape=[jax.ShapeDtypeStruct((rows, w), BF16), cache_shape, cache_shape],
        input_output_aliases={5 + n: 1 + n for n in range(len(carried))},
        compiler_params=_cparams(("parallel",)),
        name="attn_ctx",
    )(p_arr, p_arr, p_arr, qg, kg, *carried)


def _row_window(r, n_rows):
    rs = min(max(r - WIN_ROWS // 2, 0), n_rows - WIN_ROWS)
    if rs % 2 == 0:
        return rs * GRID_W, WIN_ROWS * GRID_W, True
    return (rs - 1) * GRID_W, (WIN_ROWS + 2) * GRID_W, False


def _attn_dec_kernel(q_ref, k_ref, v_ref, qg_ref, kg_ref, cos_ref, sin_ref, kc_ref, vc_ref,
                     tab_ref, o_ref, s_ref, sc_ref, p_ref, pc_ref, li_ref, tsh_ref):
    n = q_ref.shape[0]
    n_rows = n // GRID_W
    lane = lax.broadcasted_iota(jnp.int32, (1, LANES), 1)
    scale = HEAD_DIM ** -0.5
    q = _head_norm(q_ref[...].astype(F32), qg_ref[...], lane)
    k = _head_norm(k_ref[...].astype(F32), kg_ref[...], lane)
    qr = _rope(q, cos_ref[...], sin_ref[...], lane)
    kr = _rope(k, cos_ref[...], sin_ref[...], lane).astype(BF16)
    v = v_ref[...].astype(F32)
    qs = (q * scale).astype(BF16)

    def on_head_lanes(x, h):
        r = lax.broadcasted_iota(jnp.int32, (HEAD_DIM, LANES), 0)
        c = lax.broadcasted_iota(jnp.int32, (HEAD_DIM, LANES), 1)
        sel = jnp.where(c == r + h * HEAD_DIM, 1.0, 0.0).astype(BF16)
        return jnp.dot(x.astype(BF16), sel, preferred_element_type=F32).astype(BF16)

    wide = (WIN_ROWS + 2) * GRID_W
    wl = lax.broadcasted_iota(jnp.int32, (1, wide), 1)
    edge_mask = jnp.where((wl < GRID_W) | (wl >= wide - GRID_W), NEG_INF, 0.0).astype(F32)
    acc = jnp.zeros((n, LANES), F32)
    for h in range(2):
        tsh_ref[...] = pltpu.roll(tab_ref[h], 2 * n - GRID_W, 1)
        s_ref[...] = _nt_dot((qr * _head_mask(lane, h, scale)).astype(BF16), kr)
        sc_ref[...] = _nt_dot(qs, on_head_lanes(kc_ref[h], h))
        for r in range(n_rows):
            rows = pl.ds(r * GRID_W, GRID_W)
            a, w, exact = _row_window(r, n_rows)
            if (n_rows - 1 - r) % 2 == 0:
                bias = tab_ref[h, :, (n_rows - 1 - r) * GRID_W + a:(n_rows - 1 - r) * GRID_W + a + w]
            else:
                bias = tsh_ref[:, (n_rows - 2 - r) * GRID_W + a:(n_rows - 2 - r) * GRID_W + a + w]
            sl = s_ref[rows, a:a + w] + bias
            if not exact:
                sl = sl + edge_mask
            sx = sc_ref[rows, :]
            m = jnp.maximum(jnp.max(sl, axis=-1, keepdims=True), jnp.max(sx, axis=-1, keepdims=True))
            el = jnp.exp(sl - m)
            ex = jnp.exp(sx - m)
            den = jnp.sum(el, axis=-1, keepdims=True) + jnp.sum(ex, axis=-1, keepdims=True)
            pieces = [el.astype(BF16)]
            if a > 0:
                pieces.insert(0, jnp.zeros((GRID_W, a), BF16))
            if a + w < n:
                pieces.append(jnp.zeros((GRID_W, n - a - w), BF16))
            p_ref[rows, :] = jnp.concatenate(pieces, axis=1) if len(pieces) > 1 else pieces[0]
            pc_ref[rows, :] = ex.astype(BF16)
            li_ref[rows, :] = jnp.broadcast_to(1.0 / den, (GRID_W, LANES))
        vm = _head_mask(lane, h, 1.0)
        o = jnp.dot(p_ref[...], (v * vm).astype(BF16), preferred_element_type=F32)
        o = o + jnp.dot(pc_ref[...], on_head_lanes(vc_ref[h], h), preferred_element_type=F32)
        acc = acc + o * li_ref[...]
    o_ref[...] = acc.astype(BF16)


def _attn_dec(p_arr, qg, kg, cos, sin, kc, vc, tab, layer, n_batch, seq, row0, past):
    rb0 = row0 // seq
    n_pairs = 4
    blk = lambda c: pl.BlockSpec((seq, LANES), lambda b, p: (rb0 + b, c + p))
    gain = pl.BlockSpec((None, 1, LANES), lambda b, p: (layer, 0, 0))
    const = pl.BlockSpec((seq, LANES), lambda b, p: (0, 0))
    cache = pl.BlockSpec((None, None, 2, past, HEAD_DIM), lambda b, p: (b, layer, p, 0, 0))
    return pl.pallas_call(
        _attn_dec_kernel,
        grid=(n_batch, n_pairs),
        in_specs=[
            blk(0), blk(n_pairs), blk(2 * n_pairs), gain, gain, const, const, cache, cache,
            pl.BlockSpec((None, 2, GRID_W, 2 * seq), lambda b, p: (layer, p, 0, 0)),
        ],
        out_specs=pl.BlockSpec((seq, LANES), lambda b, p: (b, p)),
        out_shape=jax.ShapeDtypeStruct((n_batch * seq, n_pairs * LANES), BF16),
        scratch_shapes=[
            pltpu.VMEM((seq, seq), F32), pltpu.VMEM((seq, past), F32),
            pltpu.VMEM((seq, seq), BF16), pltpu.VMEM((seq, past), BF16),
            pltpu.VMEM((seq, LANES), F32), pltpu.VMEM((GRID_W, 2 * seq), F32),
        ],
        compiler_params=_cparams(("parallel", "parallel")),
        name="attn_dec",
    )(p_arr, p_arr, p_arr, qg, kg, cos, sin, kc, vc, tab)


def _mix_kernel(u_ref, vg_ref, xf_ref, xp_ref, gn_ref, ws_ref, bs_ref, dft_ref, cs_ref, band_ref,
                cnt_ref, wp_ref, ps_ref, o_ref, vgn_ref, ab_ref):
    n = u_ref.shape[0]
    bw = u_ref.shape[1]
    n_groups = bw // CHUNK
    for c in range(n // CHUNK):
        rows = pl.ds(c * CHUNK, CHUNK)
        vg = jax.nn.gelu(vg_ref[rows, :].astype(F32))
        ms = jnp.mean(vg * vg, axis=-1, keepdims=True)
        vgn_ref[rows, :] = (vg * lax.rsqrt(ms + EPS) * gn_ref[...]).astype(BF16)
    for c in range(n // CHUNK):
        rows = pl.ds(c * CHUNK, CHUNK)
        parts = [jnp.dot(ws_ref[g], vgn_ref[rows, g * CHUNK:(g + 1) * CHUNK], preferred_element_type=F32)
                 for g in range(n_groups)]
        mixed = jnp.concatenate(parts, axis=1) + bs_ref[...]
        o_ref[rows, 0:bw] = (jax.nn.gelu(u_ref[rows, :].astype(F32)) * mixed).astype(BF16)
    ab_ref[...] = jnp.dot(dft_ref[...], xf_ref[...], preferred_element_type=F32).astype(BF16)
    for g in range(n_groups):
        gc = slice(g * CHUNK, (g + 1) * CHUNK)
        lhs = jnp.concatenate([ab_ref[0:n, gc], ab_ref[n:2 * n, gc]], axis=1)
        o_ref[:, bw + g * CHUNK:bw + (g + 1) * CHUNK] = jnp.dot(
            lhs, cs_ref[...], preferred_element_type=F32).astype(BF16)
    for g in range(n_groups):
        gc = slice(g * CHUNK, (g + 1) * CHUNK)
        xg = xp_ref[:, gc]
        wsum = jnp.dot(band_ref[g], xg, preferred_element_type=F32)
        pooled = wsum / cnt_ref[:, gc] - xg.astype(F32)
        y = jnp.dot(pooled.astype(BF16), wp_ref[g], preferred_element_type=F32) * ps_ref[:, gc]
        o_ref[:, 2 * bw + g * CHUNK:2 * bw + (g + 1) * CHUNK] = y.astype(BF16)


def _mix(p_arr, gn, ws, bs, dft, cs, band, cnt, wp, ps, layer, n_ctx_rows):
    m = p_arr.shape[0]
    bw = gn.shape[-1]
    n = MIX_ROWS
    n_groups = bw // CHUNK
    kind = lambda i: jnp.where(i < n_ctx_rows // n, 0, 1)
    blk = lambda c: pl.BlockSpec((n, bw), lambda i: (i, c))
    once = pl.Buffered(1)
    return pl.pallas_call(
        _mix_kernel,
        grid=(m // n,),
        in_specs=[
            blk(3), blk(4), blk(5), blk(6),
            pl.BlockSpec((None, 1, bw), lambda i: (layer, 0, 0)),
            pl.BlockSpec((None, n_groups, CHUNK, CHUNK), lambda i: (layer, 0, 0, 0)),
            pl.BlockSpec((None, CHUNK, bw), lambda i: (layer, 0, 0)),
            pl.BlockSpec((None, 2 * n, n), lambda i: (kind(i), 0, 0), pipeline_mode=once),
            pl.BlockSpec((2 * CHUNK, CHUNK), lambda i: (0, 0)),
            pl.BlockSpec((None, n_groups, n, n), lambda i: (kind(i), 0, 0, 0), pipeline_mode=once),
            pl.BlockSpec((None, n, bw), lambda i: (kind(i), 0, 0), pipeline_mode=once),
            pl.BlockSpec((None, n_groups, CHUNK, CHUNK), lambda i: (layer, 0, 0, 0)),
            pl.BlockSpec((None, 1, bw), lambda i: (layer, 0, 0)),
        ],
        out_specs=pl.BlockSpec((n, 3 * bw), lambda i: (i, 0)),
        out_shape=jax.ShapeDtypeStruct((m, 3 * bw), BF16),
        scratch_shapes=[pltpu.VMEM((n, bw), BF16), pltpu.VMEM((2 * n, bw), BF16)],
        compiler_params=_cparams(("parallel",)),
        name="mix",
    )(p_arr, p_arr, p_arr, p_arr, gn, ws, bs, dft, cs, band, cnt, wp, ps)


def _merge_kernel(n_ctx_tiles, xa_ref, xb_ref, mod_ref, g2_ref, gate_ref, actx_ref, adec_ref, bcd_ref,
                  wb_ref, wo_ref, o_ref, h_ref):
    d = xa_ref.shape[1]
    bw = actx_ref.shape[1]
    is_ctx = pl.program_id(0) < n_ctx_tiles
    merged = jnp.zeros(xa_ref.shape, F32)
    for b in range(d // bw):
        if b == 0:
            br = jnp.where(is_ctx, actx_ref[...], adec_ref[...])
        else:
            br = bcd_ref[:, (b - 1) * bw:b * bw]
        y = jnp.dot(br, wb_ref[b * bw:(b + 1) * bw, :], preferred_element_type=F32)
        merged = merged + gate_ref[:, b * d:(b + 1) * d].astype(F32) * y
    mix = jnp.dot(merged.astype(BF16), wo_ref[...], preferred_element_type=F32)
    x = jnp.where(is_ctx, xa_ref[...], xb_ref[...])
    x1 = x + mod_ref[2:3, :] * mix
    o_ref[...] = x1
    h_ref[...] = _modulated_norm(x1, g2_ref[...], mod_ref[3:4, :], mod_ref[4:5, :]).astype(BF16)


def _merge(xa, xb, mod, norm2_g, gates, a_ctx, a_dec, bcd, wb, wo, layer, dec_seq):
    d = xa.shape[1]
    n_ctx_rows = xa.shape[0]
    m = n_ctx_rows + xb.shape[0]
    bw = a_ctx.shape[1]
    tm = 256
    nct = n_ctx_rows // tm
    row = functools.partial(_mod_row, tm=tm, n_ctx_rows=n_ctx_rows, dec_seq=dec_seq)
    once = pl.Buffered(1)
    ctx_tile = lambda i: (jnp.minimum(i, nct - 1), 0)
    dec_tile = lambda i: (jnp.maximum(i - nct, 0), 0)
    return pl.pallas_call(
        functools.partial(_merge_kernel, nct),
        grid=(m // tm,),
        in_specs=[
            pl.BlockSpec((tm, d), ctx_tile),
            pl.BlockSpec((tm, d), dec_tile),
            pl.BlockSpec((None, None, 6, d), lambda i: (layer, row(i), 0, 0)),
            pl.BlockSpec((None, 1, d), lambda i: (layer, 0, 0)),
            pl.BlockSpec((tm, 4 * d), lambda i: (i, 0)),
            pl.BlockSpec((tm, bw), ctx_tile),
            pl.BlockSpec((tm, bw), dec_tile),
            pl.BlockSpec((tm, 3 * bw), lambda i: (i, 0)),
            pl.BlockSpec((None, d, d), lambda i: (layer, 0, 0), pipeline_mode=once),
            pl.BlockSpec((None, d, d), lambda i: (layer, 0, 0), pipeline_mode=once),
        ],
        out_specs=[pl.BlockSpec((tm, d), lambda i: (i, 0))] * 2,
        out_shape=[jax.ShapeDtypeStruct((m, d), F32), jax.ShapeDtypeStruct((m, d), BF16)],
        compiler_params=_cparams(("parallel",)),
        name="merge",
    )(xa, xb, mod, norm2_g, gates, a_ctx, a_dec, bcd, wb, wo)


def _mlp_kernel(n_ctx_tiles, n_up, h_ref, x_ref, mod_ref, w1_ref, w2_ref, oa_ref, ob_ref, a_ref):
    s = pl.program_id(1)
    is_ctx = pl.program_id(0) < n_ctx_tiles
    tf = w1_ref.shape[1]

    @pl.when(s < n_up)
    def _():
        a = jnp.dot(h_ref[...], w1_ref[...], preferred_element_type=F32)
        a_ref[s] = jnp.square(jnp.maximum(a, 0.0)).astype(BF16)

    def down():
        acc = jnp.dot(a_ref[0], w2_ref[0:tf, :], preferred_element_type=F32)
        for c in range(1, n_up):
            acc = acc + jnp.dot(a_ref[c], w2_ref[c * tf:(c + 1) * tf, :], preferred_element_type=F32)
        return x_ref[...] + mod_ref[5:6, :] * acc

    @pl.when((s >= n_up) & is_ctx)
    def _():
        oa_ref[...] = down()

    @pl.when((s >= n_up) & jnp.logical_not(is_ctx))
    def _():
        ob_ref[...] = down()


def _mlp(h, x, mod, w1, w2, layer, n_ctx_rows, dec_seq):
    m, d = x.shape
    dff = w1.shape[2]
    tm, tf, tn = 1024, 1024, 256
    n_up, n_down = dff // tf, d // tn
    nct = n_ctx_rows // tm
    row = functools.partial(_mod_row, tm=tm, n_ctx_rows=n_ctx_rows, dec_seq=dec_seq)
    col = lambda s: jnp.maximum(s - n_up, 0)
    ctx_out = lambda i, s: (jnp.minimum(i, nct - 1), jnp.where(i < nct, col(s), n_down - 1))
    dec_out = lambda i, s: (jnp.maximum(i - nct, 0), jnp.where(i < nct, 0, col(s)))
    return pl.pallas_call(
        functools.partial(_mlp_kernel, nct, n_up),
        grid=(m // tm, n_up + n_down),
        in_specs=[
            pl.BlockSpec((tm, d), lambda i, s: (i, 0)),
            pl.BlockSpec((tm, tn), lambda i, s: (i, col(s))),
            pl.BlockSpec((None, None, 6, tn), lambda i, s: (layer, row(i), 0, col(s))),
            pl.BlockSpec((None, d, tf), lambda i, s: (layer, 0, jnp.minimum(s, n_up - 1))),
            pl.BlockSpec((None, dff, tn), lambda i, s: (layer, 0, col(s))),
        ],
        out_specs=[pl.BlockSpec((tm, tn), ctx_out), pl.BlockSpec((tm, tn), dec_out)],
        out_shape=[jax.ShapeDtypeStruct((n_ctx_rows, d), F32),
                   jax.ShapeDtypeStruct((m - n_ctx_rows, d), F32)],
        scratch_shapes=[pltpu.VMEM((n_up, tm, tf), BF16)],
        compiler_params=_cparams(("arbitrary", "arbitrary")),
        name="mlp",
    )(h, x, mod, w1, w2)


def _dft_tables(n_ctx_seq, n):
    def cs(size):
        k = np.arange(size)
        ang = 2.0 * np.pi * ((k[:, None] * k[None, :]) % size) / size
        return np.cos(ang) / np.sqrt(size), np.sin(ang) / np.sqrt(size)

    c_s, s_s = cs(n_ctx_seq)
    eye = np.eye(n // n_ctx_seq)
    c_l, s_l = cs(n)
    pos = np.stack([np.concatenate([np.kron(eye, c_s), np.kron(eye, s_s)], axis=0),
                    np.concatenate([c_l, s_l], axis=0)])
    c_c, s_c = cs(CHUNK)
    chan = np.concatenate([c_c, -s_c], axis=0)
    return jnp.asarray(pos, F32).astype(BF16), jnp.asarray(chan, F32).astype(BF16)


def _pool_tables(n_ctx_seq, n):
    def one(size):
        t = np.arange(size)
        bands, counts = [], []
        for w in POOL_WINDOWS:
            lo = np.clip(t - w // 2, 0, size - 1)
            hi = np.clip(t + w // 2 - 1, 0, size - 1)
            s = np.arange(size)[None, :]
            bands.append(((s >= lo[:, None]) & (s <= hi[:, None])).astype(np.float32))
            counts.append((hi - lo + 1).astype(np.float32))
        return np.stack(bands), np.stack(counts)

    b_s, c_s = one(n_ctx_seq)
    rep = n // n_ctx_seq
    b_l, c_l = one(n)
    band = np.stack([np.stack([np.kron(np.eye(rep), b) for b in b_s]), b_l])
    cnt = np.stack([np.tile(c_s, (1, rep)), c_l])
    cnt = np.repeat(cnt.transpose(0, 2, 1), CHUNK, axis=2)
    return jnp.asarray(band, BF16), jnp.asarray(cnt, F32)


def _rope_tables(n):
    t = jnp.arange(n)
    pos = jnp.stack([t // GRID_W, t % GRID_W], axis=-1).astype(F32)
    half = HEAD_DIM // 2
    inv = 1.0 / (ROPE_BASE ** (jnp.arange(0, half, 2, dtype=F32) / half))
    ang = pos[:, :, None] * inv
    cos, sin = jnp.cos(ang), jnp.sin(ang)
    cos_h = jnp.concatenate([cos[:, 0], cos[:, 0], cos[:, 1], cos[:, 1]], axis=-1)
    sin_h = jnp.concatenate([-sin[:, 0], sin[:, 0], -sin[:, 1], sin[:, 1]], axis=-1)
    return jnp.tile(cos_h, (1, 2)), jnp.tile(sin_h, (1, 2))


def _bias_table(rpb, n):
    n_layers, n_heads, n_d, n_e = rpb.shape
    cq = np.arange(GRID_W)
    cs = np.clip(cq - WIN_COLS // 2, 0, GRID_W - WIN_COLS)
    kc = np.arange(GRID_W)
    col_ok = (kc[None, :] >= cs[:, None]) & (kc[None, :] < cs[:, None] + WIN_COLS)
    col_mask = np.where(col_ok, 0.0, NEG_INF).astype(np.float32)
    dc_idx = np.clip(kc[None, :] - cq[:, None], -(WIN_COLS - 1), WIN_COLS - 1) + WIN_COLS - 1
    pick = (dc_idx[None, :, :] == np.arange(n_e)[:, None, None]).astype(np.float32)
    t = jnp.einsum("lhde,eck->lhcdk", rpb, jnp.asarray(pick), precision=lax.Precision.HIGHEST)
    t = t + jnp.asarray(col_mask)[:, None, :]
    flat = t.reshape(n_layers, n_heads, GRID_W, n_d * GRID_W)
    front = WIN_ROWS * GRID_W
    back = 2 * n - front - n_d * GRID_W
    return jnp.pad(flat, ((0, 0), (0, 0), (0, 0), (front, back)))


def kernel(x_prompt, x_sample, cache_k, cache_v, c, c_ctx, norm1_g, w_in, b_gate, q_norm_g, k_norm_g,
           rpb, gmlp_norm_g, w_spatial, b_spatial, w_pool, pool_scale, w_branch, w_out, norm2_g,
           w_mlp1, w_mlp2, w_ada, b_ada):
    n_b, seq, d = x_prompt.shape
    n_db, dec_seq, _ = x_sample.shape
    n_layers = w_in.shape[0]
    n_heads, past = cache_k.shape[2], cache_k.shape[3]
    bw = n_heads * HEAD_DIM
    n_ctx_rows = n_b * seq
    assert dec_seq == MIX_ROWS and MIX_ROWS % seq == 0 and n_ctx_rows % MIX_ROWS == 0
    assert dec_seq // GRID_W == 2 * WIN_ROWS and bw == 4 * LANES

    xa = x_prompt.reshape(n_ctx_rows, d)
    xb = x_sample.reshape(n_db * dec_seq, d)
    pad_rows = (-(1 + n_db)) % 8
    cond = jnp.concatenate([c_ctx[None, :], c, jnp.zeros((pad_rows, d), F32)], axis=0)
    mod = _ada(cond, w_ada, b_ada).reshape(n_layers, cond.shape[0], 6, d)

    w_in_b = w_in.astype(BF16)
    wb_b, wo_b = w_branch.astype(BF16), w_out.astype(BF16)
    w1_b, w2_b = w_mlp1.astype(BF16), w_mlp2.astype(BF16)
    ws_b, wp_b = w_spatial.astype(BF16), w_pool.astype(BF16)
    bs_x = jnp.repeat(b_spatial.transpose(0, 2, 1), CHUNK, axis=2)
    qg = jnp.tile(q_norm_g, (1, 2))[:, None, :]
    kg = jnp.tile(k_norm_g, (1, 2))[:, None, :]
    dft, chan = _dft_tables(seq, MIX_ROWS)
    band, cnt = _pool_tables(seq, MIX_ROWS)
    cos, sin = _rope_tables(dec_seq)
    tab = _bias_table(rpb, dec_seq)
    n1 = norm1_g[:, None, :]
    n2 = norm2_g[:, None, :]
    gn = gmlp_norm_g[:, None, :]
    ps = pool_scale[:, None, :]
    bg = b_gate[:, None, :]

    new_kv = None
    for l in range(n_layers):
        p_arr, gates = _inproj(xa, xb, mod, n1, w_in_b, bg, l, 7 * bw, dec_seq)
        a_ctx, *new_kv = _attn_ctx(p_arr, qg, kg, l, n_layers, n_b, seq, new_kv)
        a_dec = _attn_dec(p_arr, qg, kg, cos, sin, cache_k, cache_v, tab, l, n_db, dec_seq, n_ctx_rows, past)
        bcd = _mix(p_arr, gn, ws_b, bs_x, dft, chan, band, cnt, wp_b, ps, l, n_ctx_rows)
        x1, h2 = _merge(xa, xb, mod, n2, gates, a_ctx, a_dec, bcd, wb_b, wo_b, l, dec_seq)
        xa, xb = _mlp(h2, x1, mod, w1_b, w2_b, l, n_ctx_rows, dec_seq)

    return xa.reshape(n_b, seq, d), xb.reshape(n_db, dec_seq, d), new_kv[0], new_kv[1]
```

```python
import functools

import numpy as np
import jax
import jax.numpy as jnp
from jax import lax
from jax.experimental import pallas as pl
from jax.experimental.pallas import tpu as pltpu

F32 = jnp.float32
BF16 = jnp.bfloat16

GRID_W = 64
HEAD_DIM = 64
WIN_ROWS = 8
WIN_COLS = 16
ROPE_BASE = 10000.0
CHUNK = 128
POOL_WINDOWS = (2, 4, 8, 16)
EPS = 1e-6
NEG_INF = -1e30

LANES = 128
MIX_ROWS = 1024
VMEM_LIMIT = 56 * 1024 * 1024


def _cparams(sem):
    return pltpu.CompilerParams(dimension_semantics=sem, vmem_limit_bytes=VMEM_LIMIT)


def _mod_row(i, tm, n_ctx_rows, dec_seq):
    nc = n_ctx_rows // tm
    per = dec_seq // tm
    return jnp.where(i < nc, 0, 1 + (i - nc) // per)


def _ada_kernel(c_ref, w_ref, b_ref, o_ref):
    s = jax.nn.silu(c_ref[...]).astype(BF16)
    o_ref[...] = jnp.dot(s, w_ref[...].astype(BF16), preferred_element_type=F32) + b_ref[...]


def _ada(cond, w_ada, b_ada):
    n_layers, d, n = w_ada.shape
    rows = cond.shape[0]
    tn = 1024
    return pl.pallas_call(
        _ada_kernel,
        grid=(n_layers, n // tn),
        in_specs=[
            pl.BlockSpec((rows, d), lambda l, j: (0, 0)),
            pl.BlockSpec((None, d, tn), lambda l, j: (l, 0, j)),
            pl.BlockSpec((None, 1, tn), lambda l, j: (l, 0, j)),
        ],
        out_specs=pl.BlockSpec((None, rows, tn), lambda l, j: (l, 0, j)),
        out_shape=jax.ShapeDtypeStruct((n_layers, rows, n), F32),
        compiler_params=_cparams(("parallel", "parallel")),
        name="ada",
    )(cond, w_ada, b_ada.reshape(n_layers, 1, n))


def _modulated_norm(x, g, shift, scale):
    ms = jnp.mean(x * x, axis=-1, keepdims=True)
    return (x * lax.rsqrt(ms + EPS) * g) * (1.0 + scale) + shift


SIDE_BLOCKS = 128


def _cast_blocks(src_refs, dst_refs):
    for src, dst in zip(src_refs, dst_refs):
        dst[...] = src[...].astype(BF16)


def _side_specs(side, n_steps, step_of):
    in_specs, out_specs, out_shapes = [], [], []
    for arr, layer in side:
        _, r, c = arr.shape
        assert SIDE_BLOCKS <= n_steps and r % (16 * SIDE_BLOCKS) == 0
        rb = r // SIDE_BLOCKS
        blk = lambda *ids: jnp.minimum(step_of(*ids), SIDE_BLOCKS - 1)
        in_specs.append(pl.BlockSpec((None, rb, c), lambda *ids, layer=layer: (layer, blk(*ids), 0)))
        out_specs.append(pl.BlockSpec((rb, c), lambda *ids: (blk(*ids), 0)))
        out_shapes.append(jax.ShapeDtypeStruct((r, c), BF16))
    return in_specs, out_specs, out_shapes


def _x_tile_copy(xa_hbm, xb_hbm, use_a, tile, nct, tm, xbuf, sem):
    if use_a:
        src = xa_hbm.at[pl.ds(tile * tm, tm), :]
    else:
        src = xb_hbm.at[pl.ds((tile - nct) * tm, tm), :]
    return pltpu.make_async_copy(src, xbuf, sem)


def _inproj_kernel(nct, n_a, n_g, n_side, xa_hbm, xb_hbm, mod_ref, modn_ref, g_ref, wa_ref, wgl_ref, wgh_ref,
                   bg_ref, *refs):
    side_in = refs[:n_side]
    os_ref, og_ref = refs[n_side:n_side + 2]
    side_out = refs[n_side + 2:2 * n_side + 2]
    xbuf, h0_ref, h1_ref, sem = refs[2 * n_side + 2:]
    i = pl.program_id(0)
    j = pl.program_id(1)
    n_tiles = pl.num_programs(0)
    tm = xbuf.shape[0]
    rc = tm // n_g
    slot = i % 2
    copy = functools.partial(_x_tile_copy, xa_hbm, xb_hbm, nct=nct, tm=tm, xbuf=xbuf, sem=sem)

    def start(tile):
        @pl.when(tile < nct)
        def _():
            copy(True, tile).start()

        @pl.when(tile >= nct)
        def _():
            copy(False, tile).start()

    @pl.when((i == 0) & (j == 0))
    def _():
        start(i)
        copy(True, 0).wait()
        h = _modulated_norm(xbuf[...], g_ref[...], mod_ref[0:1, :], mod_ref[1:2, :])
        h0_ref[...] = h.astype(BF16)

    @pl.when((j == 0) & (i + 1 < n_tiles))
    def _():
        start(i + 1)

    @pl.when((j == n_a) & (i + 1 < n_tiles))
    def _():
        copy(True, 0).wait()

    for cur, nxt in ((h0_ref, h1_ref), (h1_ref, h0_ref)):
        mine = slot == (0 if cur is h0_ref else 1)

        @pl.when((j < n_a) & mine)
        def _(cur=cur):
            _cast_blocks(side_in, side_out)
            os_ref[...] = jnp.dot(cur[...], wa_ref[...], preferred_element_type=F32).astype(BF16)

        @pl.when((j >= n_a) & mine)
        def _(cur=cur, nxt=nxt):
            _cast_blocks(side_in, side_out)
            rows = pl.ds(pl.multiple_of((j - n_a) * rc, rc), rc)
            hn = _modulated_norm(xbuf[rows, :], g_ref[...], modn_ref[0:1, :], modn_ref[1:2, :])
            nxt[rows, :] = hn.astype(BF16)
            half = wgl_ref.shape[1]
            for part, w_ref in enumerate((wgl_ref, wgh_ref)):
                cols = slice(part * half, (part + 1) * half)
                logits = jnp.dot(cur[...], w_ref[...], preferred_element_type=F32) + bg_ref[:, cols]
                og_ref[:, cols] = (0.5 * jnp.tanh(0.5 * logits) + 0.5).astype(BF16)


def _inproj(xa, xb, mod, norm_g, w, b_gate, layer, n_mix, dec_seq, side):
    d = xa.shape[1]
    n_ctx_rows = xa.shape[0]
    m = n_ctx_rows + xb.shape[0]
    n_gate = w.shape[1] - n_mix
    tm, ta, tg = 1024, 512, 1024
    half = tg // 2
    assert n_mix % ta == 0 and n_mix % half == 0 and n_gate % tg == 0
    n_a, n_g = n_mix // ta, n_gate // tg
    n_tiles = m // tm
    row = functools.partial(_mod_row, tm=tm, n_ctx_rows=n_ctx_rows, dec_seq=dec_seq)
    a_col = lambda j: jnp.minimum(j, n_a - 1)
    g_col = lambda j: jnp.maximum(j - n_a, 0)
    n_inner = n_a + n_g
    side_in, side_out, side_shapes = _side_specs(side, n_tiles * n_inner, lambda i, j: i * n_inner + j)
    return pl.pallas_call(
        functools.partial(_inproj_kernel, n_ctx_rows // tm, n_a, n_g, len(side)),
        grid=(n_tiles, n_inner),
        in_specs=[
            pl.BlockSpec(memory_space=pl.ANY),
            pl.BlockSpec(memory_space=pl.ANY),
            pl.BlockSpec((None, None, 6, d), lambda i, j: (layer, row(i), 0, 0)),
            pl.BlockSpec((None, None, 6, d), lambda i, j: (layer, row(jnp.minimum(i + 1, n_tiles - 1)), 0, 0)),
            pl.BlockSpec((None, 1, d), lambda i, j: (layer, 0, 0)),
            pl.BlockSpec((d, ta), lambda i, j: (0, a_col(j))),
            pl.BlockSpec((d, half), lambda i, j: (0, n_mix // half + 2 * g_col(j))),
            pl.BlockSpec((d, half), lambda i, j: (0, n_mix // half + 2 * g_col(j) + 1)),
            pl.BlockSpec((None, 1, tg), lambda i, j: (layer, 0, g_col(j))),
        ] + side_in,
        out_specs=[
            pl.BlockSpec((tm, ta), lambda i, j: (i, a_col(j))),
            pl.BlockSpec((tm, tg), lambda i, j: (i, g_col(j))),
        ] + side_out,
        out_shape=[jax.ShapeDtypeStruct((m, n_mix), BF16), jax.ShapeDtypeStruct((m, n_gate), BF16)] + side_shapes,
        scratch_shapes=[pltpu.VMEM((tm, d), F32), pltpu.VMEM((tm, d), BF16), pltpu.VMEM((tm, d), BF16),
                        pltpu.SemaphoreType.DMA(())],
        compiler_params=_cparams(("arbitrary", "arbitrary")),
        name="inproj",
    )(xa, xb, mod, mod, norm_g, w, w, w, b_gate, *[arr for arr, _ in side])


def _head_norm(x, g, lane):
    row = lax.broadcasted_iota(jnp.int32, (LANES, LANES), 0)
    same_head = (row >= HEAD_DIM) == (lane >= HEAD_DIM)
    ones = jnp.where(same_head, 1.0, 0.0).astype(BF16)
    x2 = x * x
    hi = x2.astype(BF16)
    lo = (x2 - hi.astype(F32)).astype(BF16)
    ss = (jnp.dot(hi, ones, preferred_element_type=F32) + jnp.dot(lo, ones, preferred_element_type=F32))
    return x * lax.rsqrt(ss * (1.0 / HEAD_DIM) + EPS) * g


def _head_norm_xlu(x, g, lane):
    low = lane < HEAD_DIM
    x2 = x * x
    s_lo = jnp.sum(jnp.where(low, x2, 0.0), axis=-1, keepdims=True)
    s_hi = jnp.sum(jnp.where(low, 0.0, x2), axis=-1, keepdims=True)
    ss = jnp.where(low, s_lo, s_hi)
    return x * lax.rsqrt(ss * (1.0 / HEAD_DIM) + EPS) * g


def _rope(x, cos, sin_signed, lane):
    first = (lane & 16) == 0
    partner = jnp.where(first, pltpu.roll(x, LANES - 16, 1), pltpu.roll(x, 16, 1))
    return x * cos + partner * sin_signed


def _nt_dot(a, b):
    return lax.dot_general(a, b, (((1,), (1,)), ((), ())), preferred_element_type=F32)


def _head_mask(lane, h, value):
    return jnp.where((lane >= HEAD_DIM) == (h == 1), value, 0.0).astype(F32)


def _attn_ctx_kernel(first, q_ref, k_ref, v_ref, qg_ref, kg_ref, *rest):
    a_ref, kn_ref, vn_ref = rest[-3:]
    if first:
        for later in range(1, kn_ref.shape[0]):
            kn_ref[later] = jnp.zeros(kn_ref.shape[1:], F32)
            vn_ref[later] = jnp.zeros(vn_ref.shape[1:], F32)
        kn_ref, vn_ref = kn_ref.at[0], vn_ref.at[0]
    lane = lax.broadcasted_iota(jnp.int32, (1, LANES), 1)
    scale = HEAD_DIM ** -0.5
    for p in range(q_ref.shape[1] // LANES):
        cols = slice(p * LANES, (p + 1) * LANES)
        q = _head_norm_xlu(q_ref[:, cols].astype(F32), qg_ref[...], lane)
        k = _head_norm_xlu(k_ref[:, cols].astype(F32), kg_ref[...], lane)
        v = v_ref[:, cols].astype(F32)
        for h in range(2):
            kn_ref[2 * p + h] = k[:, h * HEAD_DIM:(h + 1) * HEAD_DIM]
            vn_ref[2 * p + h] = v[:, h * HEAD_DIM:(h + 1) * HEAD_DIM]
        kb = k.astype(BF16)
        acc = jnp.zeros(q.shape, F32)
        for h in range(2):
            s = _nt_dot((q * _head_mask(lane, h, scale)).astype(BF16), kb)
            e = jnp.exp(s - jnp.max(s, axis=-1, keepdims=True))
            o = jnp.dot(e.astype(BF16), (v * _head_mask(lane, h, 1.0)).astype(BF16),
                        preferred_element_type=F32)
            acc = acc + o / jnp.sum(e, axis=-1, keepdims=True)
        a_ref[:, cols] = acc.astype(BF16)


def _attn_ctx(p_arr, qg, kg, layer, n_layers, n_batch, seq, new_kv):
    w = 4 * LANES
    rows = n_batch * seq
    n_heads = w // HEAD_DIM
    blk = lambda c: pl.BlockSpec((seq, w), lambda b: (b, c))
    gain = pl.BlockSpec((None, 1, LANES), lambda b: (layer, 0, 0))
    first = new_kv is None
    assert first == (layer == 0)
    if first:
        cache = pl.BlockSpec((None, n_layers, n_heads, seq, HEAD_DIM), lambda b: (b, 0, 0, 0, 0))
    else:
        cache = pl.BlockSpec((None, None, n_heads, seq, HEAD_DIM), lambda b: (b, layer, 0, 0, 0))
    cache_shape = jax.ShapeDtypeStruct((n_batch, n_layers, n_heads, seq, HEAD_DIM), F32)
    carried = [] if first else list(new_kv)
    return pl.pallas_call(
        functools.partial(_attn_ctx_kernel, first),
        grid=(n_batch,),
        in_specs=[blk(0), blk(1), blk(2), gain, gain] + [pl.BlockSpec(memory_space=pl.ANY)] * len(carried),
        out_specs=[pl.BlockSpec((seq, w), lambda b: (b, 0)), cache, cache],
        out_shape=[jax.ShapeDtypeStruct((rows, w), BF16), cache_shape, cache_shape],
        input_output_aliases={5 + n: 1 + n for n in range(len(carried))},
        compiler_params=_cparams(("parallel",)),
        name="attn_ctx",
    )(p_arr, p_arr, p_arr, qg, kg, *carried)


def _row_window(r, n_rows):
    rs = min(max(r - WIN_ROWS // 2, 0), n_rows - WIN_ROWS)
    if rs % 2 == 0:
        return rs * GRID_W, WIN_ROWS * GRID_W, True
    return (rs - 1) * GRID_W, (WIN_ROWS + 2) * GRID_W, False


def _attn_dec_kernel(q_ref, k_ref, v_ref, qg_ref, kg_ref, cos_ref, sin_ref, kc_ref, vc_ref,
                     tab_ref, o_ref, s_ref, sc_ref, p_ref, pc_ref, li_ref, tsh_ref):
    n = q_ref.shape[0]
    n_rows = n // GRID_W
    lane = lax.broadcasted_iota(jnp.int32, (1, LANES), 1)
    scale = HEAD_DIM ** -0.5
    q = _head_norm(q_ref[...].astype(F32), qg_ref[...], lane)
    k = _head_norm(k_ref[...].astype(F32), kg_ref[...], lane)
    qr = _rope(q, cos_ref[...], sin_ref[...], lane)
    kr = _rope(k, cos_ref[...], sin_ref[...], lane).astype(BF16)
    v = v_ref[...].astype(F32)
    qs = (q * scale).astype(BF16)

    def on_head_rows(xt, h):
        xb = xt.astype(BF16)
        zero = jnp.zeros_like(xb)
        return jnp.concatenate([xb, zero] if h == 0 else [zero, xb], axis=0)

    wide = (WIN_ROWS + 2) * GRID_W
    wl = lax.broadcasted_iota(jnp.int32, (1, wide), 1)
    edge_mask = jnp.where((wl < GRID_W) | (wl >= wide - GRID_W), NEG_INF, 0.0).astype(F32)
    acc = jnp.zeros((n, LANES), F32)
    for h in range(2):
        tsh_ref[...] = pltpu.roll(tab_ref[h], 2 * n - GRID_W, 1)
        s_ref[...] = _nt_dot((qr * _head_mask(lane, h, scale)).astype(BF16), kr)
        sc_ref[...] = jnp.dot(qs, on_head_rows(kc_ref[h], h), preferred_element_type=F32)
        for r in range(n_rows):
            rows = pl.ds(r * GRID_W, GRID_W)
            a, w, exact = _row_window(r, n_rows)
            if (n_rows - 1 - r) % 2 == 0:
                bias = tab_ref[h, :, (n_rows - 1 - r) * GRID_W + a:(n_rows - 1 - r) * GRID_W + a + w]
            else:
                bias = tsh_ref[:, (n_rows - 2 - r) * GRID_W + a:(n_rows - 2 - r) * GRID_W + a + w]
            sl = s_ref[rows, a:a + w] + bias
            if not exact:
                sl = sl + edge_mask
            sx = sc_ref[rows, :]
            m = jnp.maximum(jnp.max(sl, axis=-1, keepdims=True), jnp.max(sx, axis=-1, keepdims=True))
            el = jnp.exp(sl - m)
            ex = jnp.exp(sx - m)
            den = jnp.sum(el, axis=-1, keepdims=True) + jnp.sum(ex, axis=-1, keepdims=True)
            pieces = [el.astype(BF16)]
            if a > 0:
                pieces.insert(0, jnp.zeros((GRID_W, a), BF16))
            if a + w < n:
                pieces.append(jnp.zeros((GRID_W, n - a - w), BF16))
            p_ref[rows, :] = jnp.concatenate(pieces, axis=1) if len(pieces) > 1 else pieces[0]
            pc_ref[rows, :] = ex.astype(BF16)
            li_ref[rows, :] = jnp.broadcast_to(1.0 / den, (GRID_W, LANES))
        vm = _head_mask(lane, h, 1.0)
        o = jnp.dot(p_ref[...], (v * vm).astype(BF16), preferred_element_type=F32)
        o = o + _nt_dot(pc_ref[...], on_head_rows(vc_ref[h], h))
        acc = acc + o * li_ref[...]
    o_ref[...] = acc.astype(BF16)


def _attn_dec(p_arr, qg, kg, cos, sin, kc, vc, tab, layer, n_batch, seq, row0, past):
    rb0 = row0 // seq
    n_pairs = 4
    blk = lambda c: pl.BlockSpec((seq, LANES), lambda b, p: (rb0 + b, c + p))
    gain = pl.BlockSpec((None, 1, LANES), lambda b, p: (layer, 0, 0))
    const = pl.BlockSpec((seq, LANES), lambda b, p: (0, 0))
    cache = pl.BlockSpec((None, None, 2, HEAD_DIM, past), lambda b, p: (b, layer, p, 0, 0))
    return pl.pallas_call(
        _attn_dec_kernel,
        grid=(n_batch, n_pairs),
        in_specs=[
            blk(0), blk(n_pairs), blk(2 * n_pairs), gain, gain, const, const, cache, cache,
            pl.BlockSpec((None, 2, GRID_W, 2 * seq), lambda b, p: (layer, p, 0, 0)),
        ],
        out_specs=pl.BlockSpec((seq, LANES), lambda b, p: (b, p)),
        out_shape=jax.ShapeDtypeStruct((n_batch * seq, n_pairs * LANES), BF16),
        scratch_shapes=[
            pltpu.VMEM((seq, seq), F32), pltpu.VMEM((seq, past), F32),
            pltpu.VMEM((seq, seq), BF16), pltpu.VMEM((seq, past), BF16),
            pltpu.VMEM((seq, LANES), F32), pltpu.VMEM((GRID_W, 2 * seq), F32),
        ],
        compiler_params=_cparams(("parallel", "parallel")),
        name="attn_dec",
    )(p_arr, p_arr, p_arr, qg, kg, cos, sin, kc, vc, tab)


def _mix_kernel(u_ref, vg_ref, xf_ref, xp_ref, gn_ref, ws_ref, bs_ref, dft_ref, cs_ref, band_ref,
                cnt_ref, wp_ref, ps_ref, o_ref, vgn_ref, ab_ref):
    n = u_ref.shape[0]
    bw = u_ref.shape[1]
    n_groups = bw // CHUNK
    for c in range(n // CHUNK):
        rows = pl.ds(c * CHUNK, CHUNK)
        vg = jax.nn.gelu(vg_ref[rows, :].astype(F32))
        ms = jnp.mean(vg * vg, axis=-1, keepdims=True)
        vgn_ref[rows, :] = (vg * lax.rsqrt(ms + EPS) * gn_ref[...]).astype(BF16)
    for c in range(n // CHUNK):
        rows = pl.ds(c * CHUNK, CHUNK)
        parts = [jnp.dot(ws_ref[g], vgn_ref[rows, g * CHUNK:(g + 1) * CHUNK], preferred_element_type=F32)
                 for g in range(n_groups)]
        mixed = jnp.concatenate(parts, axis=1) + bs_ref[...]
        o_ref[rows, 0:bw] = (jax.nn.gelu(u_ref[rows, :].astype(F32)) * mixed).astype(BF16)
    ab_ref[...] = jnp.dot(dft_ref[...], xf_ref[...], preferred_element_type=F32).astype(BF16)
    for g in range(n_groups):
        gc = slice(g * CHUNK, (g + 1) * CHUNK)
        lhs = jnp.concatenate([ab_ref[0:n, gc], ab_ref[n:2 * n, gc]], axis=1)
        o_ref[:, bw + g * CHUNK:bw + (g + 1) * CHUNK] = jnp.dot(
            lhs, cs_ref[...], preferred_element_type=F32).astype(BF16)
    for g in range(n_groups):
        gc = slice(g * CHUNK, (g + 1) * CHUNK)
        xg = xp_ref[:, gc]
        wsum = jnp.dot(band_ref[g], xg, preferred_element_type=F32)
        pooled = wsum / cnt_ref[:, gc] - xg.astype(F32)
        y = jnp.dot(pooled.astype(BF16), wp_ref[g], preferred_element_type=F32) * ps_ref[:, gc]
        o_ref[:, 2 * bw + g * CHUNK:2 * bw + (g + 1) * CHUNK] = y.astype(BF16)


def _mix(p_arr, gn, ws, bs, dft, cs, band, cnt, wp, ps, layer, n_ctx_rows):
    m = p_arr.shape[0]
    bw = gn.shape[-1]
    n = MIX_ROWS
    n_groups = bw // CHUNK
    kind = lambda i: jnp.where(i < n_ctx_rows // n, 0, 1)
    blk = lambda c: pl.BlockSpec((n, bw), lambda i: (i, c))
    once = pl.Buffered(1)
    return pl.pallas_call(
        _mix_kernel,
        grid=(m // n,),
        in_specs=[
            blk(3), blk(4), blk(5), blk(6),
            pl.BlockSpec((None, 1, bw), lambda i: (layer, 0, 0)),
            pl.BlockSpec((None, n_groups, CHUNK, CHUNK), lambda i: (layer, 0, 0, 0)),
            pl.BlockSpec((None, CHUNK, bw), lambda i: (layer, 0, 0)),
            pl.BlockSpec((None, 2 * n, n), lambda i: (kind(i), 0, 0), pipeline_mode=once),
            pl.BlockSpec((2 * CHUNK, CHUNK), lambda i: (0, 0)),
            pl.BlockSpec((None, n_groups, n, n), lambda i: (kind(i), 0, 0, 0), pipeline_mode=once),
            pl.BlockSpec((None, n, bw), lambda i: (kind(i), 0, 0), pipeline_mode=once),
            pl.BlockSpec((None, n_groups, CHUNK, CHUNK), lambda i: (layer, 0, 0, 0)),
            pl.BlockSpec((None, 1, bw), lambda i: (layer, 0, 0)),
        ],
        out_specs=pl.BlockSpec((n, 3 * bw), lambda i: (i, 0)),
        out_shape=jax.ShapeDtypeStruct((m, 3 * bw), BF16),
        scratch_shapes=[pltpu.VMEM((n, bw), BF16), pltpu.VMEM((2 * n, bw), BF16)],
        compiler_params=_cparams(("parallel",)),
        name="mix",
    )(p_arr, p_arr, p_arr, p_arr, gn, ws, bs, dft, cs, band, cnt, wp, ps)


def _merge_kernel(n_ctx_tiles, xa_ref, xb_ref, mod_ref, g2_ref, gate_ref, actx_ref, adec_ref, bcd_ref,
                  wb_ref, wo_ref, o_ref, h_ref):
    d = xa_ref.shape[1]
    bw = actx_ref.shape[1]
    is_ctx = pl.program_id(0) < n_ctx_tiles
    merged = jnp.zeros(xa_ref.shape, F32)
    for b in range(d // bw):
        if b == 0:
            br = jnp.where(is_ctx, actx_ref[...], adec_ref[...])
        else:
            br = bcd_ref[:, (b - 1) * bw:b * bw]
        y = jnp.dot(br, wb_ref[b * bw:(b + 1) * bw, :], preferred_element_type=F32)
        merged = merged + gate_ref[:, b * d:(b + 1) * d].astype(F32) * y
    mix = jnp.dot(merged.astype(BF16), wo_ref[...], preferred_element_type=F32)
    x = jnp.where(is_ctx, xa_ref[...], xb_ref[...])
    x1 = x + mod_ref[2:3, :] * mix
    o_ref[...] = x1
    h_ref[...] = _modulated_norm(x1, g2_ref[...], mod_ref[3:4, :], mod_ref[4:5, :]).astype(BF16)


def _merge(xa, xb, mod, norm2_g, gates, a_ctx, a_dec, bcd, wb, wo, layer, dec_seq):
    d = xa.shape[1]
    n_ctx_rows = xa.shape[0]
    m = n_ctx_rows + xb.shape[0]
    bw = a_ctx.shape[1]
    tm = 256
    nct = n_ctx_rows // tm
    row = functools.partial(_mod_row, tm=tm, n_ctx_rows=n_ctx_rows, dec_seq=dec_seq)
    once = pl.Buffered(1)
    ctx_tile = lambda i: (jnp.minimum(i, nct - 1), 0)
    dec_tile = lambda i: (jnp.maximum(i - nct, 0), 0)
    return pl.pallas_call(
        functools.partial(_merge_kernel, nct),
        grid=(m // tm,),
        in_specs=[
            pl.BlockSpec((tm, d), ctx_tile),
            pl.BlockSpec((tm, d), dec_tile),
            pl.BlockSpec((None, None, 6, d), lambda i: (layer, row(i), 0, 0)),
            pl.BlockSpec((None, 1, d), lambda i: (layer, 0, 0)),
            pl.BlockSpec((tm, 4 * d), lambda i: (i, 0)),
            pl.BlockSpec((tm, bw), ctx_tile),
            pl.BlockSpec((tm, bw), dec_tile),
            pl.BlockSpec((tm, 3 * bw), lambda i: (i, 0)),
            pl.BlockSpec((d, d), lambda i: (0, 0), pipeline_mode=once),
            pl.BlockSpec((d, d), lambda i: (0, 0), pipeline_mode=once),
        ],
        out_specs=[pl.BlockSpec((tm, d), lambda i: (i, 0))] * 2,
        out_shape=[jax.ShapeDtypeStruct((m, d), F32), jax.ShapeDtypeStruct((m, d), BF16)],
        compiler_params=_cparams(("parallel",)),
        name="merge",
    )(xa, xb, mod, norm2_g, gates, a_ctx, a_dec, bcd, wb, wo)


def _mlp_kernel(n_ctx_tiles, n_up, n_side, h_ref, x_ref, mod_ref, w1_ref, w2_ref, *refs):
    side_in = refs[:n_side]
    oa_ref, ob_ref = refs[n_side:n_side + 2]
    side_out = refs[n_side + 2:2 * n_side + 2]
    a_ref = refs[2 * n_side + 2]
    s = pl.program_id(1)
    is_ctx = pl.program_id(0) < n_ctx_tiles
    tf = w1_ref.shape[1]

    @pl.when(s < n_up)
    def _():
        _cast_blocks(side_in, side_out)
        a = jnp.dot(h_ref[...], w1_ref[...], preferred_element_type=F32)
        a_ref[s] = jnp.square(jnp.maximum(a, 0.0)).astype(BF16)

    def down():
        _cast_blocks(side_in, side_out)
        acc = jnp.dot(a_ref[0], w2_ref[0:tf, :], preferred_element_type=F32)
        for c in range(1, n_up):
            acc = acc + jnp.dot(a_ref[c], w2_ref[c * tf:(c + 1) * tf, :], preferred_element_type=F32)
        return x_ref[...] + mod_ref[5:6, :] * acc

    @pl.when((s >= n_up) & is_ctx)
    def _():
        oa_ref[...] = down()

    @pl.when((s >= n_up) & jnp.logical_not(is_ctx))
    def _():
        ob_ref[...] = down()


def _mlp(h, x, mod, w1, w2, layer, n_ctx_rows, dec_seq, side):
    m, d = x.shape
    dff = w1.shape[1]
    tm, tf, tn = 1024, 1024, 256
    n_up, n_down = dff // tf, d // tn
    nct = n_ctx_rows // tm
    row = functools.partial(_mod_row, tm=tm, n_ctx_rows=n_ctx_rows, dec_seq=dec_seq)
    col = lambda s: jnp.maximum(s - n_up, 0)
    ctx_out = lambda i, s: (jnp.minimum(i, nct - 1), jnp.where(i < nct, col(s), n_down - 1))
    dec_out = lambda i, s: (jnp.maximum(i - nct, 0), jnp.where(i < nct, 0, col(s)))
    n_inner = n_up + n_down
    side_in, side_out, side_shapes = _side_specs(side, (m // tm) * n_inner, lambda i, s: i * n_inner + s)
    return pl.pallas_call(
        functools.partial(_mlp_kernel, nct, n_up, len(side)),
        grid=(m // tm, n_inner),
        in_specs=[
            pl.BlockSpec((tm, d), lambda i, s: (i, 0)),
            pl.BlockSpec((tm, tn), lambda i, s: (i, col(s))),
            pl.BlockSpec((None, None, 6, tn), lambda i, s: (layer, row(i), 0, col(s))),
            pl.BlockSpec((d, tf), lambda i, s: (0, jnp.minimum(s, n_up - 1))),
            pl.BlockSpec((dff, tn), lambda i, s: (0, col(s))),
        ] + side_in,
        out_specs=[pl.BlockSpec((tm, tn), ctx_out), pl.BlockSpec((tm, tn), dec_out)] + side_out,
        out_shape=[jax.ShapeDtypeStruct((n_ctx_rows, d), F32),
                   jax.ShapeDtypeStruct((m - n_ctx_rows, d), F32)] + side_shapes,
        scratch_shapes=[pltpu.VMEM((n_up, tm, tf), BF16)],
        compiler_params=_cparams(("arbitrary", "arbitrary")),
        name="mlp",
    )(h, x, mod, w1, w2, *[arr for arr, _ in side])


def _dft_tables(n_ctx_seq, n):
    def cs(size):
        k = np.arange(size)
        ang = 2.0 * np.pi * ((k[:, None] * k[None, :]) % size) / size
        return np.cos(ang) / np.sqrt(size), np.sin(ang) / np.sqrt(size)

    c_s, s_s = cs(n_ctx_seq)
    eye = np.eye(n // n_ctx_seq)
    c_l, s_l = cs(n)
    pos = np.stack([np.concatenate([np.kron(eye, c_s), np.kron(eye, s_s)], axis=0),
                    np.concatenate([c_l, s_l], axis=0)])
    c_c, s_c = cs(CHUNK)
    chan = np.concatenate([c_c, -s_c], axis=0)
    return jnp.asarray(pos, F32).astype(BF16), jnp.asarray(chan, F32).astype(BF16)


def _pool_tables(n_ctx_seq, n):
    def one(size):
        t = np.arange(size)
        bands, counts = [], []
        for w in POOL_WINDOWS:
            lo = np.clip(t - w // 2, 0, size - 1)
            hi = np.clip(t + w // 2 - 1, 0, size - 1)
            s = np.arange(size)[None, :]
            bands.append(((s >= lo[:, None]) & (s <= hi[:, None])).astype(np.float32))
            counts.append((hi - lo + 1).astype(np.float32))
        return np.stack(bands), np.stack(counts)

    b_s, c_s = one(n_ctx_seq)
    rep = n // n_ctx_seq
    b_l, c_l = one(n)
    band = np.stack([np.stack([np.kron(np.eye(rep), b) for b in b_s]), b_l])
    cnt = np.stack([np.tile(c_s, (1, rep)), c_l])
    cnt = np.repeat(cnt.transpose(0, 2, 1), CHUNK, axis=2)
    return jnp.asarray(band, BF16), jnp.asarray(cnt, F32)


def _rope_tables(n):
    t = jnp.arange(n)
    pos = jnp.stack([t // GRID_W, t % GRID_W], axis=-1).astype(F32)
    half = HEAD_DIM // 2
    inv = 1.0 / (ROPE_BASE ** (jnp.arange(0, half, 2, dtype=F32) / half))
    ang = pos[:, :, None] * inv
    cos, sin = jnp.cos(ang), jnp.sin(ang)
    cos_h = jnp.concatenate([cos[:, 0], cos[:, 0], cos[:, 1], cos[:, 1]], axis=-1)
    sin_h = jnp.concatenate([-sin[:, 0], sin[:, 0], -sin[:, 1], sin[:, 1]], axis=-1)
    return jnp.tile(cos_h, (1, 2)), jnp.tile(sin_h, (1, 2))


def _bias_table(rpb, n):
    n_layers, n_heads, n_d, n_e = rpb.shape
    cq = np.arange(GRID_W)
    cs = np.clip(cq - WIN_COLS // 2, 0, GRID_W - WIN_COLS)
    kc = np.arange(GRID_W)
    col_ok = (kc[None, :] >= cs[:, None]) & (kc[None, :] < cs[:, None] + WIN_COLS)
    col_mask = np.where(col_ok, 0.0, NEG_INF).astype(np.float32)
    dc_idx = np.clip(kc[None, :] - cq[:, None], -(WIN_COLS - 1), WIN_COLS - 1) + WIN_COLS - 1
    pick = (dc_idx[None, :, :] == np.arange(n_e)[:, None, None]).astype(np.float32)
    t = jnp.einsum("lhde,eck->lhcdk", rpb, jnp.asarray(pick), precision=lax.Precision.HIGHEST)
    t = t + jnp.asarray(col_mask)[:, None, :]
    flat = t.reshape(n_layers, n_heads, GRID_W, n_d * GRID_W)
    front = WIN_ROWS * GRID_W
    back = 2 * n - front - n_d * GRID_W
    return jnp.pad(flat, ((0, 0), (0, 0), (0, 0), (front, back)))


def kernel(x_prompt, x_sample, cache_k, cache_v, c, c_ctx, norm1_g, w_in, b_gate, q_norm_g, k_norm_g,
           rpb, gmlp_norm_g, w_spatial, b_spatial, w_pool, pool_scale, w_branch, w_out, norm2_g,
           w_mlp1, w_mlp2, w_ada, b_ada):
    n_b, seq, d = x_prompt.shape
    n_db, dec_seq, _ = x_sample.shape
    n_layers = w_in.shape[0]
    n_heads, past = cache_k.shape[2], cache_k.shape[3]
    bw = n_heads * HEAD_DIM
    n_ctx_rows = n_b * seq
    assert dec_seq == MIX_ROWS and MIX_ROWS % seq == 0 and n_ctx_rows % MIX_ROWS == 0
    assert dec_seq // GRID_W == 2 * WIN_ROWS and bw == 4 * LANES

    xa = x_prompt.reshape(n_ctx_rows, d)
    xb = x_sample.reshape(n_db * dec_seq, d)
    pad_rows = (-(1 + n_db)) % 8
    cond = jnp.concatenate([c_ctx[None, :], c, jnp.zeros((pad_rows, d), F32)], axis=0)
    mod = _ada(cond, w_ada, b_ada).reshape(n_layers, cond.shape[0], 6, d)

    w_in_l = w_in[0].astype(BF16)
    kct = cache_k.transpose(0, 1, 2, 4, 3)
    vct = cache_v.transpose(0, 1, 2, 4, 3)
    ws_b, wp_b = w_spatial.astype(BF16), w_pool.astype(BF16)
    bs_x = jnp.repeat(b_spatial.transpose(0, 2, 1), CHUNK, axis=2)
    qg = jnp.tile(q_norm_g, (1, 2))[:, None, :]
    kg = jnp.tile(k_norm_g, (1, 2))[:, None, :]
    dft, chan = _dft_tables(seq, MIX_ROWS)
    band, cnt = _pool_tables(seq, MIX_ROWS)
    cos, sin = _rope_tables(dec_seq)
    tab = _bias_table(rpb, dec_seq)
    n1 = norm1_g[:, None, :]
    n2 = norm2_g[:, None, :]
    gn = gmlp_norm_g[:, None, :]
    ps = pool_scale[:, None, :]
    bg = b_gate[:, None, :]

    new_kv = None
    for l in range(n_layers):
        p_arr, gates, w1_l, w2_l, wb_l, wo_l = _inproj(
            xa, xb, mod, n1, w_in_l, bg, l, 7 * bw, dec_seq,
            side=[(w_mlp1, l), (w_mlp2, l), (w_branch, l), (w_out, l)])
        a_ctx, *new_kv = _attn_ctx(p_arr, qg, kg, l, n_layers, n_b, seq, new_kv)
        a_dec = _attn_dec(p_arr, qg, kg, cos, sin, kct, vct, tab, l, n_db, dec_seq, n_ctx_rows, past)
        bcd = _mix(p_arr, gn, ws_b, bs_x, dft, chan, band, cnt, wp_b, ps, l, n_ctx_rows)
        x1, h2 = _merge(xa, xb, mod, n2, gates, a_ctx, a_dec, bcd, wb_l, wo_l, l, dec_seq)
        next_w_in = [(w_in, l + 1)] if l + 1 < n_layers else []
        xa, xb, *w_next = _mlp(h2, x1, mod, w1_l, w2_l, l, n_ctx_rows, dec_seq, side=next_w_in)
        if w_next:
            w_in_l = w_next[0]

    return xa.reshape(n_b, seq, d), xb.reshape(n_db, dec_seq, d), new_kv[0], new_kv[1]
```

```python
import functools

import numpy as np
import jax
import jax.numpy as jnp
from jax import lax
from jax.experimental import pallas as pl
from jax.experimental.pallas import tpu as pltpu

F32 = jnp.float32
BF16 = jnp.bfloat16

GRID_W = 64
HEAD_DIM = 64
WIN_ROWS = 8
WIN_COLS = 16
ROPE_BASE = 10000.0
CHUNK = 128
POOL_WINDOWS = (2, 4, 8, 16)
EPS = 1e-6
NEG_INF = -1e30

LANES = 128
MIX_ROWS = 1024
VMEM_LIMIT = 60 * 1024 * 1024


def _cparams(sem):
    return pltpu.CompilerParams(dimension_semantics=sem, vmem_limit_bytes=VMEM_LIMIT)


def _mod_row(i, tm, n_ctx_rows, dec_seq):
    nc = n_ctx_rows // tm
    per = dec_seq // tm
    return jnp.where(i < nc, 0, 1 + (i - nc) // per)


def _ada_kernel(c_ref, w_ref, b_ref, o_ref):
    s = jax.nn.silu(c_ref[...]).astype(BF16)
    o_ref[...] = jnp.dot(s, w_ref[...].astype(BF16), preferred_element_type=F32) + b_ref[...]


def _ada(cond, w_ada, b_ada):
    n_layers, d, n = w_ada.shape
    rows = cond.shape[0]
    tn = 1024
    return pl.pallas_call(
        _ada_kernel,
        grid=(n_layers, n // tn),
        in_specs=[
            pl.BlockSpec((rows, d), lambda l, j: (0, 0)),
            pl.BlockSpec((None, d, tn), lambda l, j: (l, 0, j)),
            pl.BlockSpec((None, 1, tn), lambda l, j: (l, 0, j)),
        ],
        out_specs=pl.BlockSpec((None, rows, tn), lambda l, j: (l, 0, j)),
        out_shape=jax.ShapeDtypeStruct((n_layers, rows, n), F32),
        compiler_params=_cparams(("parallel", "parallel")),
        name="ada",
    )(cond, w_ada, b_ada.reshape(n_layers, 1, n))


def _modulated_norm(x, g, shift, scale):
    ms = jnp.mean(x * x, axis=-1, keepdims=True)
    return (x * lax.rsqrt(ms + EPS) * g) * (1.0 + scale) + shift


SIDE_BLOCKS = 128


def _cast_blocks(src_refs, dst_refs):
    for src, dst in zip(src_refs, dst_refs):
        dst[...] = src[...].astype(BF16)


def _side_specs(side, n_steps, step_of):
    in_specs, out_specs, out_shapes = [], [], []
    for arr, layer in side:
        _, r, c = arr.shape
        assert SIDE_BLOCKS <= n_steps and r % (16 * SIDE_BLOCKS) == 0
        rb = r // SIDE_BLOCKS
        blk = lambda *ids: jnp.minimum(step_of(*ids), SIDE_BLOCKS - 1)
        in_specs.append(pl.BlockSpec((None, rb, c), lambda *ids, layer=layer: (layer, blk(*ids), 0)))
        out_specs.append(pl.BlockSpec((rb, c), lambda *ids: (blk(*ids), 0)))
        out_shapes.append(jax.ShapeDtypeStruct((r, c), BF16))
    return in_specs, out_specs, out_shapes


def _x_tile_copy(xa_hbm, xb_hbm, use_a, tile, nct, tm, xbuf, sem):
    if use_a:
        src = xa_hbm.at[pl.ds(tile * tm, tm), :]
    else:
        src = xb_hbm.at[pl.ds((tile - nct) * tm, tm), :]
    return pltpu.make_async_copy(src, xbuf, sem)


def _inproj_kernel(nct, n_a, n_g, n_parts, n_side, xa_hbm, xb_hbm, mod_ref, modn_ref, g_ref, wa_ref, *refs):
    wg_refs, bg_ref, refs = refs[:n_parts], refs[n_parts], refs[n_parts + 1:]
    side_in = refs[:n_side]
    os_ref, og_ref = refs[n_side:n_side + 2]
    side_out = refs[n_side + 2:2 * n_side + 2]
    xbuf, h0_ref, h1_ref, sem = refs[2 * n_side + 2:]
    i = pl.program_id(0)
    j = pl.program_id(1)
    n_tiles = pl.num_programs(0)
    tm = xbuf.shape[0]
    rc = tm // n_g
    slot = i % 2
    copy = functools.partial(_x_tile_copy, xa_hbm, xb_hbm, nct=nct, tm=tm, xbuf=xbuf, sem=sem)

    def start(tile):
        @pl.when(tile < nct)
        def _():
            copy(True, tile).start()

        @pl.when(tile >= nct)
        def _():
            copy(False, tile).start()

    @pl.when((i == 0) & (j == 0))
    def _():
        start(i)
        copy(True, 0).wait()
        h = _modulated_norm(xbuf[...], g_ref[...], mod_ref[0:1, :], mod_ref[1:2, :])
        h0_ref[...] = h.astype(BF16)

    @pl.when((j == 0) & (i + 1 < n_tiles))
    def _():
        start(i + 1)

    @pl.when((j == n_a) & (i + 1 < n_tiles))
    def _():
        copy(True, 0).wait()

    for cur, nxt in ((h0_ref, h1_ref), (h1_ref, h0_ref)):
        mine = slot == (0 if cur is h0_ref else 1)

        @pl.when((j < n_a) & mine)
        def _(cur=cur):
            _cast_blocks(side_in, side_out)
            os_ref[...] = jnp.dot(cur[...], wa_ref[...], preferred_element_type=F32).astype(BF16)

        @pl.when((j >= n_a) & mine)
        def _(cur=cur, nxt=nxt):
            _cast_blocks(side_in, side_out)
            rows = pl.ds(pl.multiple_of((j - n_a) * rc, rc), rc)
            hn = _modulated_norm(xbuf[rows, :], g_ref[...], modn_ref[0:1, :], modn_ref[1:2, :])
            nxt[rows, :] = hn.astype(BF16)
            wpart = wg_refs[0].shape[1]
            for part, w_ref in enumerate(wg_refs):
                cols = slice(part * wpart, (part + 1) * wpart)
                logits = jnp.dot(cur[...], w_ref[...], preferred_element_type=F32) + bg_ref[:, cols]
                og_ref[:, cols] = (0.5 * jnp.tanh(0.5 * logits) + 0.5).astype(BF16)


def _inproj(xa, xb, mod, norm_g, w, b_gate, layer, n_mix, dec_seq, side):
    d = xa.shape[1]
    n_ctx_rows = xa.shape[0]
    m = n_ctx_rows + xb.shape[0]
    n_gate = w.shape[1] - n_mix
    tm, ta, tg, wpart = 1024, 512, 2048, 512
    n_parts = tg // wpart
    assert n_mix % ta == 0 and n_mix % wpart == 0 and n_gate % tg == 0
    n_a, n_g = n_mix // ta, n_gate // tg
    n_tiles = m // tm
    row = functools.partial(_mod_row, tm=tm, n_ctx_rows=n_ctx_rows, dec_seq=dec_seq)
    a_col = lambda j: jnp.minimum(j, n_a - 1)
    g_col = lambda j: jnp.maximum(j - n_a, 0)
    n_inner = n_a + n_g
    side_in, side_out, side_shapes = _side_specs(side, n_tiles * n_inner, lambda i, j: i * n_inner + j)
    return pl.pallas_call(
        functools.partial(_inproj_kernel, n_ctx_rows // tm, n_a, n_g, n_parts, len(side)),
        grid=(n_tiles, n_inner),
        in_specs=[
            pl.BlockSpec(memory_space=pl.ANY),
            pl.BlockSpec(memory_space=pl.ANY),
            pl.BlockSpec((None, None, 6, d), lambda i, j: (layer, row(i), 0, 0)),
            pl.BlockSpec((None, None, 6, d), lambda i, j: (layer, row(jnp.minimum(i + 1, n_tiles - 1)), 0, 0)),
            pl.BlockSpec((None, 1, d), lambda i, j: (layer, 0, 0)),
            pl.BlockSpec((d, ta), lambda i, j: (0, a_col(j))),
        ] + [
            pl.BlockSpec((d, wpart), lambda i, j, part=part: (0, n_mix // wpart + n_parts * g_col(j) + part))
            for part in range(n_parts)
        ] + [
            pl.BlockSpec((None, 1, tg), lambda i, j: (layer, 0, g_col(j))),
        ] + side_in,
        out_specs=[
            pl.BlockSpec((tm, ta), lambda i, j: (i, a_col(j))),
            pl.BlockSpec((tm, tg), lambda i, j: (i, g_col(j))),
        ] + side_out,
        out_shape=[jax.ShapeDtypeStruct((m, n_mix), BF16), jax.ShapeDtypeStruct((m, n_gate), BF16)] + side_shapes,
        scratch_shapes=[pltpu.VMEM((tm, d), F32), pltpu.VMEM((tm, d), BF16), pltpu.VMEM((tm, d), BF16),
                        pltpu.SemaphoreType.DMA(())],
        compiler_params=_cparams(("arbitrary", "arbitrary")),
        name="inproj",
    )(xa, xb, mod, mod, norm_g, w, *([w] * n_parts), b_gate, *[arr for arr, _ in side])


def _head_norm(x, g, lane):
    row = lax.broadcasted_iota(jnp.int32, (LANES, LANES), 0)
    same_head = (row >= HEAD_DIM) == (lane >= HEAD_DIM)
    ones = jnp.where(same_head, 1.0, 0.0).astype(BF16)
    x2 = x * x
    hi = x2.astype(BF16)
    lo = (x2 - hi.astype(F32)).astype(BF16)
    ss = (jnp.dot(hi, ones, preferred_element_type=F32) + jnp.dot(lo, ones, preferred_element_type=F32))
    return x * lax.rsqrt(ss * (1.0 / HEAD_DIM) + EPS) * g


def _head_norm_xlu(x, g, lane):
    low = lane < HEAD_DIM
    x2 = x * x
    s_lo = jnp.sum(jnp.where(low, x2, 0.0), axis=-1, keepdims=True)
    s_hi = jnp.sum(jnp.where(low, 0.0, x2), axis=-1, keepdims=True)
    ss = jnp.where(low, s_lo, s_hi)
    return x * lax.rsqrt(ss * (1.0 / HEAD_DIM) + EPS) * g


def _rope(x, cos, sin_signed, lane):
    first = (lane & 16) == 0
    partner = jnp.where(first, pltpu.roll(x, LANES - 16, 1), pltpu.roll(x, 16, 1))
    return x * cos + partner * sin_signed


def _nt_dot(a, b):
    return lax.dot_general(a, b, (((1,), (1,)), ((), ())), preferred_element_type=F32)


def _head_mask(lane, h, value):
    return jnp.where((lane >= HEAD_DIM) == (h == 1), value, 0.0).astype(F32)


def _attn_ctx_kernel(first, q_ref, k_ref, v_ref, qg_ref, kg_ref, *rest):
    a_ref, kn_ref, vn_ref = rest[-3:]
    if first:
        for later in range(1, kn_ref.shape[0]):
            kn_ref[later] = jnp.zeros(kn_ref.shape[1:], F32)
            vn_ref[later] = jnp.zeros(vn_ref.shape[1:], F32)
        kn_ref, vn_ref = kn_ref.at[0], vn_ref.at[0]
    lane = lax.broadcasted_iota(jnp.int32, (1, LANES), 1)
    scale = HEAD_DIM ** -0.5
    for p in range(q_ref.shape[1] // LANES):
        cols = slice(p * LANES, (p + 1) * LANES)
        q = _head_norm_xlu(q_ref[:, cols].astype(F32), qg_ref[...], lane)
        k = _head_norm_xlu(k_ref[:, cols].astype(F32), kg_ref[...], lane)
        v = v_ref[:, cols].astype(F32)
        for h in range(2):
            kn_ref[2 * p + h] = k[:, h * HEAD_DIM:(h + 1) * HEAD_DIM]
            vn_ref[2 * p + h] = v[:, h * HEAD_DIM:(h + 1) * HEAD_DIM]
        kb = k.astype(BF16)
        acc = jnp.zeros(q.shape, F32)
        for h in range(2):
            s = _nt_dot((q * _head_mask(lane, h, scale)).astype(BF16), kb)
            e = jnp.exp(s - jnp.max(s, axis=-1, keepdims=True))
            o = jnp.dot(e.astype(BF16), (v * _head_mask(lane, h, 1.0)).astype(BF16),
                        preferred_element_type=F32)
            acc = acc + o / jnp.sum(e, axis=-1, keepdims=True)
        a_ref[:, cols] = acc.astype(BF16)


def _attn_ctx(p_arr, qg, kg, layer, n_layers, n_batch, seq, new_kv):
    w = 4 * LANES
    rows = n_batch * seq
    n_heads = w // HEAD_DIM
    blk = lambda c: pl.BlockSpec((seq, w), lambda b: (b, c))
    gain = pl.BlockSpec((None, 1, LANES), lambda b: (layer, 0, 0))
    first = new_kv is None
    assert first == (layer == 0)
    if first:
        cache = pl.BlockSpec((None, n_layers, n_heads, seq, HEAD_DIM), lambda b: (b, 0, 0, 0, 0))
    else:
        cache = pl.BlockSpec((None, None, n_heads, seq, HEAD_DIM), lambda b: (b, layer, 0, 0, 0))
    cache_shape = jax.ShapeDtypeStruct((n_batch, n_layers, n_heads, seq, HEAD_DIM), F32)
    carried = [] if first else list(new_kv)
    return pl.pallas_call(
        functools.partial(_attn_ctx_kernel, first),
        grid=(n_batch,),
        in_specs=[blk(0), blk(1), blk(2), gain, gain] + [pl.BlockSpec(memory_space=pl.ANY)] * len(carried),
        out_specs=[pl.BlockSpec((seq, w), lambda b: (b, 0)), cache, cache],
        out_shape=[jax.ShapeDtypeStruct((rows, w), BF16), cache_shape, cache_shape],
        input_output_aliases={5 + n: 1 + n for n in range(len(carried))},
        compiler_params=_cparams(("parallel",)),
        name="attn_ctx",
    )(p_arr, p_arr, p_arr, qg, kg, *carried)


def _row_window(r, n_rows):
    rs = min(max(r - WIN_ROWS // 2, 0), n_rows - WIN_ROWS)
    if rs % 2 == 0:
        return rs * GRID_W, WIN_ROWS * GRID_W, True
    return (rs - 1) * GRID_W, (WIN_ROWS + 2) * GRID_W, False


def _attn_dec_kernel(q_ref, k_ref, v_ref, qg_ref, kg_ref, cos_ref, sin_ref, kc_ref, vc_ref,
                     tab_ref, o_ref, s_ref, sc_ref, p_ref, pc_ref, li_ref, tsh_ref):
    n = q_ref.shape[0]
    n_rows = n // GRID_W
    lane = lax.broadcasted_iota(jnp.int32, (1, LANES), 1)
    scale = HEAD_DIM ** -0.5
    q = _head_norm(q_ref[...].astype(F32), qg_ref[...], lane)
    k = _head_norm(k_ref[...].astype(F32), kg_ref[...], lane)
    qr = _rope(q, cos_ref[...], sin_ref[...], lane)
    kr = _rope(k, cos_ref[...], sin_ref[...], lane).astype(BF16)
    v = v_ref[...].astype(F32)
    qs = (q * scale).astype(BF16)

    def on_head_rows(xt, h):
        xb = xt.astype(BF16)
        zero = jnp.zeros_like(xb)
        return jnp.concatenate([xb, zero] if h == 0 else [zero, xb], axis=0)

    wide = (WIN_ROWS + 2) * GRID_W
    wl = lax.broadcasted_iota(jnp.int32, (1, wide), 1)
    edge_mask = jnp.where((wl < GRID_W) | (wl >= wide - GRID_W), NEG_INF, 0.0).astype(F32)
    heads = []
    for h in range(2):
        tsh_ref[h] = pltpu.roll(tab_ref[h], 2 * n - GRID_W, 1)
        heads.append(((qr * _head_mask(lane, h, scale)).astype(BF16),
                      (v * _head_mask(lane, h, 1.0)).astype(BF16),
                      on_head_rows(kc_ref[h], h), on_head_rows(vc_ref[h], h)))
    chunk_rows = s_ref.shape[1] // GRID_W
    for c in range(n_rows // chunk_rows):
        wins = [_row_window(r, n_rows) for r in range(c * chunk_rows, (c + 1) * chunk_rows)]
        a_c = min(a for a, _, _ in wins)
        e_c = max(a + w for a, w, _ in wins)
        w_c = e_c - a_c
        qrows = slice(c * chunk_rows * GRID_W, (c + 1) * chunk_rows * GRID_W)
        acc = jnp.zeros((chunk_rows * GRID_W, LANES), F32)
        for h in range(2):
            qh, vh, kch, vch = heads[h]
            slot = h
            s_ref[slot, :, 0:w_c] = _nt_dot(qh[qrows], kr[a_c:e_c])
            sc_ref[slot] = jnp.dot(qs[qrows], kch, preferred_element_type=F32)
            for rr in range(chunk_rows):
                r = c * chunk_rows + rr
                rows = pl.ds(rr * GRID_W, GRID_W)
                a, w, exact = wins[rr]
                if (n_rows - 1 - r) % 2 == 0:
                    bias = tab_ref[h, :, (n_rows - 1 - r) * GRID_W + a:(n_rows - 1 - r) * GRID_W + a + w]
                else:
                    bias = tsh_ref[h, :, (n_rows - 2 - r) * GRID_W + a:(n_rows - 2 - r) * GRID_W + a + w]
                sl = s_ref[slot, rows, a - a_c:a - a_c + w] + bias
                if not exact:
                    sl = sl + edge_mask
                sx = sc_ref[slot, rows, :]
                m = jnp.maximum(jnp.max(sl, axis=-1, keepdims=True), jnp.max(sx, axis=-1, keepdims=True))
                el = jnp.exp(sl - m)
                ex = jnp.exp(sx - m)
                den = jnp.sum(el, axis=-1, keepdims=True) + jnp.sum(ex, axis=-1, keepdims=True)
                pieces = [el.astype(BF16)]
                if a > a_c:
                    pieces.insert(0, jnp.zeros((GRID_W, a - a_c), BF16))
                if a + w < e_c:
                    pieces.append(jnp.zeros((GRID_W, e_c - a - w), BF16))
                p_ref[slot, rows, 0:w_c] = jnp.concatenate(pieces, axis=1) if len(pieces) > 1 else pieces[0]
                pc_ref[slot, rows, :] = ex.astype(BF16)
                li_ref[slot, rows, :] = jnp.broadcast_to(1.0 / den, (GRID_W, LANES))
            o = jnp.dot(p_ref[slot, :, 0:w_c], vh[a_c:e_c], preferred_element_type=F32)
            o = o + _nt_dot(pc_ref[slot], vch)
            acc = acc + o * li_ref[slot]
        o_ref[qrows, :] = acc.astype(BF16)


def _attn_dec(p_arr, qg, kg, cos, sin, kc, vc, tab, layer, n_batch, seq, row0, past):
    rb0 = row0 // seq
    n_pairs = 4
    chunk = 4 * GRID_W
    win = (WIN_ROWS + 4) * GRID_W
    blk = lambda c: pl.BlockSpec((seq, LANES), lambda b, p: (rb0 + b, c + p))
    gain = pl.BlockSpec((None, 1, LANES), lambda b, p: (layer, 0, 0))
    const = pl.BlockSpec((seq, LANES), lambda b, p: (0, 0))
    cache = pl.BlockSpec((None, None, 2, HEAD_DIM, past), lambda b, p: (b, layer, p, 0, 0))
    return pl.pallas_call(
        _attn_dec_kernel,
        grid=(n_batch, n_pairs),
        in_specs=[
            blk(0), blk(n_pairs), blk(2 * n_pairs), gain, gain, const, const, cache, cache,
            pl.BlockSpec((None, 2, GRID_W, 2 * seq), lambda b, p: (layer, p, 0, 0)),
        ],
        out_specs=pl.BlockSpec((seq, LANES), lambda b, p: (b, p)),
        out_shape=jax.ShapeDtypeStruct((n_batch * seq, n_pairs * LANES), BF16),
        scratch_shapes=[
            pltpu.VMEM((2, chunk, win), F32), pltpu.VMEM((2, chunk, past), F32),
            pltpu.VMEM((2, chunk, win), BF16), pltpu.VMEM((2, chunk, past), BF16),
            pltpu.VMEM((2, chunk, LANES), F32), pltpu.VMEM((2, GRID_W, 2 * seq), F32),
        ],
        compiler_params=_cparams(("parallel", "parallel")),
        name="attn_dec",
    )(p_arr, p_arr, p_arr, qg, kg, cos, sin, kc, vc, tab)


def _mix_kernel(u_ref, vg_ref, xf_ref, xp_ref, gn_ref, ws_ref, bs_ref, dft_ref, cs_ref, band_ref,
                cnt_ref, wp_ref, ps_ref, o_ref, vgn_ref, ab_ref):
    n = u_ref.shape[0]
    bw = u_ref.shape[1]
    n_groups = bw // CHUNK
    for c in range(n // CHUNK):
        rows = pl.ds(c * CHUNK, CHUNK)
        vg = jax.nn.gelu(vg_ref[rows, :].astype(F32))
        ms = jnp.mean(vg * vg, axis=-1, keepdims=True)
        vgn_ref[rows, :] = (vg * lax.rsqrt(ms + EPS) * gn_ref[...]).astype(BF16)
    for c in range(n // CHUNK):
        rows = pl.ds(c * CHUNK, CHUNK)
        parts = [jnp.dot(ws_ref[g], vgn_ref[rows, g * CHUNK:(g + 1) * CHUNK], preferred_element_type=F32)
                 for g in range(n_groups)]
        mixed = jnp.concatenate(parts, axis=1) + bs_ref[...]
        o_ref[rows, 0:bw] = (jax.nn.gelu(u_ref[rows, :].astype(F32)) * mixed).astype(BF16)
    ab_ref[...] = jnp.dot(dft_ref[...], xf_ref[...], preferred_element_type=F32).astype(BF16)
    for g in range(n_groups):
        gc = slice(g * CHUNK, (g + 1) * CHUNK)
        lhs = jnp.concatenate([ab_ref[0:n, gc], ab_ref[n:2 * n, gc]], axis=1)
        o_ref[:, bw + g * CHUNK:bw + (g + 1) * CHUNK] = jnp.dot(
            lhs, cs_ref[...], preferred_element_type=F32).astype(BF16)
    for g in range(n_groups):
        gc = slice(g * CHUNK, (g + 1) * CHUNK)
        xg = xp_ref[:, gc]
        wsum = jnp.dot(band_ref[g], xg, preferred_element_type=F32)
        pooled = wsum / cnt_ref[:, gc] - xg.astype(F32)
        y = jnp.dot(pooled.astype(BF16), wp_ref[g], preferred_element_type=F32) * ps_ref[:, gc]
        o_ref[:, 2 * bw + g * CHUNK:2 * bw + (g + 1) * CHUNK] = y.astype(BF16)


def _mix(p_arr, gn, ws, bs, dft, cs, band, cnt, wp, ps, layer, n_ctx_rows):
    m = p_arr.shape[0]
    bw = gn.shape[-1]
    n = MIX_ROWS
    n_groups = bw // CHUNK
    kind = lambda i: jnp.where(i < n_ctx_rows // n, 0, 1)
    blk = lambda c: pl.BlockSpec((n, bw), lambda i: (i, c))
    once = pl.Buffered(1)
    return pl.pallas_call(
        _mix_kernel,
        grid=(m // n,),
        in_specs=[
            blk(3), blk(4), blk(5), blk(6),
            pl.BlockSpec((None, 1, bw), lambda i: (layer, 0, 0)),
            pl.BlockSpec((None, n_groups, CHUNK, CHUNK), lambda i: (layer, 0, 0, 0)),
            pl.BlockSpec((None, CHUNK, bw), lambda i: (layer, 0, 0)),
            pl.BlockSpec((None, 2 * n, n), lambda i: (kind(i), 0, 0), pipeline_mode=once),
            pl.BlockSpec((2 * CHUNK, CHUNK), lambda i: (0, 0)),
            pl.BlockSpec((None, n_groups, n, n), lambda i: (kind(i), 0, 0, 0), pipeline_mode=once),
            pl.BlockSpec((None, n, bw), lambda i: (kind(i), 0, 0), pipeline_mode=once),
            pl.BlockSpec((None, n_groups, CHUNK, CHUNK), lambda i: (layer, 0, 0, 0)),
            pl.BlockSpec((None, 1, bw), lambda i: (layer, 0, 0)),
        ],
        out_specs=pl.BlockSpec((n, 3 * bw), lambda i: (i, 0)),
        out_shape=jax.ShapeDtypeStruct((m, 3 * bw), BF16),
        scratch_shapes=[pltpu.VMEM((n, bw), BF16), pltpu.VMEM((2 * n, bw), BF16)],
        compiler_params=_cparams(("parallel",)),
        name="mix",
    )(p_arr, p_arr, p_arr, p_arr, gn, ws, bs, dft, cs, band, cnt, wp, ps)


def _merge_kernel(n_ctx_tiles, xa_ref, xb_ref, mod_ref, g2_ref, gate_ref, actx_ref, adec_ref, bcd_ref,
                  wb_ref, wo_ref, o_ref, h_ref):
    d = xa_ref.shape[1]
    bw = actx_ref.shape[1]
    is_ctx = pl.program_id(0) < n_ctx_tiles
    merged = jnp.zeros(xa_ref.shape, F32)
    for b in range(d // bw):
        if b == 0:
            br = jnp.where(is_ctx, actx_ref[...], adec_ref[...])
        else:
            br = bcd_ref[:, (b - 1) * bw:b * bw]
        y = jnp.dot(br, wb_ref[b * bw:(b + 1) * bw, :], preferred_element_type=F32)
        merged = merged + gate_ref[:, b * d:(b + 1) * d].astype(F32) * y
    mix = jnp.dot(merged.astype(BF16), wo_ref[...], preferred_element_type=F32)
    x = jnp.where(is_ctx, xa_ref[...], xb_ref[...])
    x1 = x + mod_ref[2:3, :] * mix
    o_ref[...] = x1
    h_ref[...] = _modulated_norm(x1, g2_ref[...], mod_ref[3:4, :], mod_ref[4:5, :]).astype(BF16)


def _merge(xa, xb, mod, norm2_g, gates, a_ctx, a_dec, bcd, wb, wo, layer, dec_seq):
    d = xa.shape[1]
    n_ctx_rows = xa.shape[0]
    m = n_ctx_rows + xb.shape[0]
    bw = a_ctx.shape[1]
    tm = 256
    nct = n_ctx_rows // tm
    row = functools.partial(_mod_row, tm=tm, n_ctx_rows=n_ctx_rows, dec_seq=dec_seq)
    once = pl.Buffered(1)
    ctx_tile = lambda i: (jnp.minimum(i, nct - 1), 0)
    dec_tile = lambda i: (jnp.maximum(i - nct, 0), 0)
    return pl.pallas_call(
        functools.partial(_merge_kernel, nct),
        grid=(m // tm,),
        in_specs=[
            pl.BlockSpec((tm, d), ctx_tile),
            pl.BlockSpec((tm, d), dec_tile),
            pl.BlockSpec((None, None, 6, d), lambda i: (layer, row(i), 0, 0)),
            pl.BlockSpec((None, 1, d), lambda i: (layer, 0, 0)),
            pl.BlockSpec((tm, 4 * d), lambda i: (i, 0)),
            pl.BlockSpec((tm, bw), ctx_tile),
            pl.BlockSpec((tm, bw), dec_tile),
            pl.BlockSpec((tm, 3 * bw), lambda i: (i, 0)),
            pl.BlockSpec((d, d), lambda i: (0, 0), pipeline_mode=once),
            pl.BlockSpec((d, d), lambda i: (0, 0), pipeline_mode=once),
        ],
        out_specs=[pl.BlockSpec((tm, d), lambda i: (i, 0))] * 2,
        out_shape=[jax.ShapeDtypeStruct((m, d), F32), jax.ShapeDtypeStruct((m, d), BF16)],
        compiler_params=_cparams(("parallel",)),
        name="merge",
    )(xa, xb, mod, norm2_g, gates, a_ctx, a_dec, bcd, wb, wo)


def _mlp_kernel(n_ctx_tiles, n_up, n_side, h_ref, x_ref, mod_ref, w1_ref, w2_ref, *refs):
    side_in = refs[:n_side]
    oa_ref, ob_ref = refs[n_side:n_side + 2]
    side_out = refs[n_side + 2:2 * n_side + 2]
    a_ref = refs[2 * n_side + 2]
    s = pl.program_id(1)
    is_ctx = pl.program_id(0) < n_ctx_tiles
    tf = w1_ref.shape[1]

    @pl.when(s < n_up)
    def _():
        _cast_blocks(side_in, side_out)
        a = jnp.dot(h_ref[...], w1_ref[...], preferred_element_type=F32)
        a_ref[s] = jnp.square(jnp.maximum(a, 0.0)).astype(BF16)

    def down():
        _cast_blocks(side_in, side_out)
        acc = jnp.dot(a_ref[0], w2_ref[0:tf, :], preferred_element_type=F32)
        for c in range(1, n_up):
            acc = acc + jnp.dot(a_ref[c], w2_ref[c * tf:(c + 1) * tf, :], preferred_element_type=F32)
        return x_ref[...] + mod_ref[5:6, :] * acc

    @pl.when((s >= n_up) & is_ctx)
    def _():
        oa_ref[...] = down()

    @pl.when((s >= n_up) & jnp.logical_not(is_ctx))
    def _():
        ob_ref[...] = down()


def _mlp(h, x, mod, w1, w2, layer, n_ctx_rows, dec_seq, side):
    m, d = x.shape
    dff = w1.shape[1]
    tm, tf, tn = 1024, 1024, 256
    n_up, n_down = dff // tf, d // tn
    nct = n_ctx_rows // tm
    row = functools.partial(_mod_row, tm=tm, n_ctx_rows=n_ctx_rows, dec_seq=dec_seq)
    col = lambda s: jnp.maximum(s - n_up, 0)
    ctx_out = lambda i, s: (jnp.minimum(i, nct - 1), jnp.where(i < nct, col(s), n_down - 1))
    dec_out = lambda i, s: (jnp.maximum(i - nct, 0), jnp.where(i < nct, 0, col(s)))
    n_inner = n_up + n_down
    side_in, side_out, side_shapes = _side_specs(side, (m // tm) * n_inner, lambda i, s: i * n_inner + s)
    return pl.pallas_call(
        functools.partial(_mlp_kernel, nct, n_up, len(side)),
        grid=(m // tm, n_inner),
        in_specs=[
            pl.BlockSpec((tm, d), lambda i, s: (i, 0)),
            pl.BlockSpec((tm, tn), lambda i, s: (i, col(s))),
            pl.BlockSpec((None, None, 6, tn), lambda i, s: (layer, row(i), 0, col(s))),
            pl.BlockSpec((d, tf), lambda i, s: (0, jnp.minimum(s, n_up - 1))),
            pl.BlockSpec((dff, tn), lambda i, s: (0, col(s))),
        ] + side_in,
        out_specs=[pl.BlockSpec((tm, tn), ctx_out), pl.BlockSpec((tm, tn), dec_out)] + side_out,
        out_shape=[jax.ShapeDtypeStruct((n_ctx_rows, d), F32),
                   jax.ShapeDtypeStruct((m - n_ctx_rows, d), F32)] + side_shapes,
        scratch_shapes=[pltpu.VMEM((n_up, tm, tf), BF16)],
        compiler_params=_cparams(("arbitrary", "arbitrary")),
        name="mlp",
    )(h, x, mod, w1, w2, *[arr for arr, _ in side])


def _dft_tables(n_ctx_seq, n):
    def cs(size):
        k = np.arange(size)
        ang = 2.0 * np.pi * ((k[:, None] * k[None, :]) % size) / size
        return np.cos(ang) / np.sqrt(size), np.sin(ang) / np.sqrt(size)

    c_s, s_s = cs(n_ctx_seq)
    eye = np.eye(n // n_ctx_seq)
    c_l, s_l = cs(n)
    pos = np.stack([np.concatenate([np.kron(eye, c_s), np.kron(eye, s_s)], axis=0),
                    np.concatenate([c_l, s_l], axis=0)])
    c_c, s_c = cs(CHUNK)
    chan = np.concatenate([c_c, -s_c], axis=0)
    return jnp.asarray(pos, F32).astype(BF16), jnp.asarray(chan, F32).astype(BF16)


def _pool_tables(n_ctx_seq, n):
    def one(size):
        t = np.arange(size)
        bands, counts = [], []
        for w in POOL_WINDOWS:
            lo = np.clip(t - w // 2, 0, size - 1)
            hi = np.clip(t + w // 2 - 1, 0, size - 1)
            s = np.arange(size)[None, :]
            bands.append(((s >= lo[:, None]) & (s <= hi[:, None])).astype(np.float32))
            counts.append((hi - lo + 1).astype(np.float32))
        return np.stack(bands), np.stack(counts)

    b_s, c_s = one(n_ctx_seq)
    rep = n // n_ctx_seq
    b_l, c_l = one(n)
    band = np.stack([np.stack([np.kron(np.eye(rep), b) for b in b_s]), b_l])
    cnt = np.stack([np.tile(c_s, (1, rep)), c_l])
    cnt = np.repeat(cnt.transpose(0, 2, 1), CHUNK, axis=2)
    return jnp.asarray(band, BF16), jnp.asarray(cnt, F32)


def _rope_tables(n):
    t = jnp.arange(n)
    pos = jnp.stack([t // GRID_W, t % GRID_W], axis=-1).astype(F32)
    half = HEAD_DIM // 2
    inv = 1.0 / (ROPE_BASE ** (jnp.arange(0, half, 2, dtype=F32) / half))
    ang = pos[:, :, None] * inv
    cos, sin = jnp.cos(ang), jnp.sin(ang)
    cos_h = jnp.concatenate([cos[:, 0], cos[:, 0], cos[:, 1], cos[:, 1]], axis=-1)
    sin_h = jnp.concatenate([-sin[:, 0], sin[:, 0], -sin[:, 1], sin[:, 1]], axis=-1)
    return jnp.tile(cos_h, (1, 2)), jnp.tile(sin_h, (1, 2))


def _bias_table(rpb, n):
    n_layers, n_heads, n_d, n_e = rpb.shape
    cq = np.arange(GRID_W)
    cs = np.clip(cq - WIN_COLS // 2, 0, GRID_W - WIN_COLS)
    kc = np.arange(GRID_W)
    col_ok = (kc[None, :] >= cs[:, None]) & (kc[None, :] < cs[:, None] + WIN_COLS)
    col_mask = np.where(col_ok, 0.0, NEG_INF).astype(np.float32)
    dc_idx = np.clip(kc[None, :] - cq[:, None], -(WIN_COLS - 1), WIN_COLS - 1) + WIN_COLS - 1
    pick = (dc_idx[None, :, :] == np.arange(n_e)[:, None, None]).astype(np.float32)
    t = jnp.einsum("lhde,eck->lhcdk", rpb, jnp.asarray(pick), precision=lax.Precision.HIGHEST)
    t = t + jnp.asarray(col_mask)[:, None, :]
    flat = t.reshape(n_layers, n_heads, GRID_W, n_d * GRID_W)
    front = WIN_ROWS * GRID_W
    back = 2 * n - front - n_d * GRID_W
    return jnp.pad(flat, ((0, 0), (0, 0), (0, 0), (front, back)))


def kernel(x_prompt, x_sample, cache_k, cache_v, c, c_ctx, norm1_g, w_in, b_gate, q_norm_g, k_norm_g,
           rpb, gmlp_norm_g, w_spatial, b_spatial, w_pool, pool_scale, w_branch, w_out, norm2_g,
           w_mlp1, w_mlp2, w_ada, b_ada):
    n_b, seq, d = x_prompt.shape
    n_db, dec_seq, _ = x_sample.shape
    n_layers = w_in.shape[0]
    n_heads, past = cache_k.shape[2], cache_k.shape[3]
    bw = n_heads * HEAD_DIM
    n_ctx_rows = n_b * seq
    assert dec_seq == MIX_ROWS and MIX_ROWS % seq == 0 and n_ctx_rows % MIX_ROWS == 0
    assert dec_seq // GRID_W == 2 * WIN_ROWS and bw == 4 * LANES

    xa = x_prompt.reshape(n_ctx_rows, d)
    xb = x_sample.reshape(n_db * dec_seq, d)
    pad_rows = (-(1 + n_db)) % 8
    cond = jnp.concatenate([c_ctx[None, :], c, jnp.zeros((pad_rows, d), F32)], axis=0)
    mod = _ada(cond, w_ada, b_ada).reshape(n_layers, cond.shape[0], 6, d)

    w_in_l = w_in[0].astype(BF16)
    kct = cache_k.transpose(0, 1, 2, 4, 3)
    vct = cache_v.transpose(0, 1, 2, 4, 3)
    ws_b, wp_b = w_spatial.astype(BF16), w_pool.astype(BF16)
    bs_x = jnp.repeat(b_spatial.transpose(0, 2, 1), CHUNK, axis=2)
    qg = jnp.tile(q_norm_g, (1, 2))[:, None, :]
    kg = jnp.tile(k_norm_g, (1, 2))[:, None, :]
    dft, chan = _dft_tables(seq, MIX_ROWS)
    band, cnt = _pool_tables(seq, MIX_ROWS)
    cos, sin = _rope_tables(dec_seq)
    tab = _bias_table(rpb, dec_seq)
    n1 = norm1_g[:, None, :]
    n2 = norm2_g[:, None, :]
    gn = gmlp_norm_g[:, None, :]
    ps = pool_scale[:, None, :]
    bg = b_gate[:, None, :]

    new_kv = None
    for l in range(n_layers):
        p_arr, gates, w1_l, w2_l, wb_l, wo_l = _inproj(
            xa, xb, mod, n1, w_in_l, bg, l, 7 * bw, dec_seq,
            side=[(w_mlp1, l), (w_mlp2, l), (w_branch, l), (w_out, l)])
        a_ctx, *new_kv = _attn_ctx(p_arr, qg, kg, l, n_layers, n_b, seq, new_kv)
        a_dec = _attn_dec(p_arr, qg, kg, cos, sin, kct, vct, tab, l, n_db, dec_seq, n_ctx_rows, past)
        bcd = _mix(p_arr, gn, ws_b, bs_x, dft, chan, band, cnt, wp_b, ps, l, n_ctx_rows)
        x1, h2 = _merge(xa, xb, mod, n2, gates, a_ctx, a_dec, bcd, wb_l, wo_l, l, dec_seq)
        next_w_in = [(w_in, l + 1)] if l + 1 < n_layers else []
        xa, xb, *w_next = _mlp(h2, x1, mod, w1_l, w2_l, l, n_ctx_rows, dec_seq, side=next_w_in)
        if w_next:
            w_in_l = w_next[0]

    return xa.reshape(n_b, seq, d), xb.reshape(n_db, dec_seq, d), new_kv[0], new_kv[1]
```

```python
import functools

import numpy as np
import jax
import jax.numpy as jnp
from jax import lax
from jax.experimental import pallas as pl
from jax.experimental.pallas import tpu as pltpu

F32 = jnp.float32
BF16 = jnp.bfloat16

GRID_W = 64
HEAD_DIM = 64
WIN_ROWS = 8
WIN_COLS = 16
ROPE_BASE = 10000.0
CHUNK = 128
POOL_WINDOWS = (2, 4, 8, 16)
EPS = 1e-6
NEG_INF = -1e30

LANES = 128
MIX_ROWS = 1024
VMEM_LIMIT = 60 * 1024 * 1024


def _cparams(sem):
    return pltpu.CompilerParams(dimension_semantics=sem, vmem_limit_bytes=VMEM_LIMIT)


def _mod_row(i, tm, n_ctx_rows, dec_seq):
    nc = n_ctx_rows // tm
    per = dec_seq // tm
    return jnp.where(i < nc, 0, 1 + (i - nc) // per)


def _ada_kernel(c_ref, w_ref, b_ref, o_ref):
    s = jax.nn.silu(c_ref[...]).astype(BF16)
    o_ref[...] = jnp.dot(s, w_ref[...].astype(BF16), preferred_element_type=F32) + b_ref[...]


def _ada(cond, w_ada, b_ada):
    n_layers, d, n = w_ada.shape
    rows = cond.shape[0]
    tn = 1024
    return pl.pallas_call(
        _ada_kernel,
        grid=(n_layers, n // tn),
        in_specs=[
            pl.BlockSpec((rows, d), lambda l, j: (0, 0)),
            pl.BlockSpec((None, d, tn), lambda l, j: (l, 0, j)),
            pl.BlockSpec((None, 1, tn), lambda l, j: (l, 0, j)),
        ],
        out_specs=pl.BlockSpec((None, rows, tn), lambda l, j: (l, 0, j)),
        out_shape=jax.ShapeDtypeStruct((n_layers, rows, n), F32),
        compiler_params=_cparams(("parallel", "parallel")),
        name="ada",
    )(cond, w_ada, b_ada.reshape(n_layers, 1, n))


def _modulated_norm(x, g, shift, scale):
    ms = jnp.mean(x * x, axis=-1, keepdims=True)
    return (x * lax.rsqrt(ms + EPS) * g) * (1.0 + scale) + shift


def _cast_blocks(src_refs, dst_refs):
    for src, dst in zip(src_refs, dst_refs):
        dst[...] = src[...].astype(BF16)


def _side_specs(side, n_blocks, step_of):
    in_specs, out_specs, out_shapes = [], [], []
    for arr, layer in side:
        _, r, c = arr.shape
        assert r % (16 * n_blocks) == 0
        rb = r // n_blocks
        blk = lambda *ids: jnp.minimum(step_of(*ids), n_blocks - 1)
        in_specs.append(pl.BlockSpec((None, rb, c), lambda *ids, layer=layer: (layer, blk(*ids), 0)))
        out_specs.append(pl.BlockSpec((rb, c), lambda *ids: (blk(*ids), 0)))
        out_shapes.append(jax.ShapeDtypeStruct((r, c), BF16))
    return in_specs, out_specs, out_shapes


def _x_tile_copy(xa_hbm, xb_hbm, use_a, tile, nct, tm, xbuf, sem):
    if use_a:
        src = xa_hbm.at[pl.ds(tile * tm, tm), :]
    else:
        src = xb_hbm.at[pl.ds((tile - nct) * tm, tm), :]
    return pltpu.make_async_copy(src, xbuf, sem)


def _inproj_kernel(nct, n_a, n_g, n_parts, n_side, xa_hbm, xb_hbm, mod_ref, modn_ref, g_ref, wa_ref, *refs):
    wg_refs, bg_ref, refs = refs[:n_parts], refs[n_parts], refs[n_parts + 1:]
    side_in = refs[:n_side]
    os_ref, og_ref = refs[n_side:n_side + 2]
    side_out = refs[n_side + 2:2 * n_side + 2]
    xbuf, h0_ref, h1_ref, sem = refs[2 * n_side + 2:]
    i = pl.program_id(0)
    j = pl.program_id(1)
    n_tiles = pl.num_programs(0)
    tm = xbuf.shape[0]
    rc = tm // n_g
    slot = i % 2
    copy = functools.partial(_x_tile_copy, xa_hbm, xb_hbm, nct=nct, tm=tm, xbuf=xbuf, sem=sem)

    def start(tile):
        @pl.when(tile < nct)
        def _():
            copy(True, tile).start()

        @pl.when(tile >= nct)
        def _():
            copy(False, tile).start()

    @pl.when((i == 0) & (j == 0))
    def _():
        start(i)
        copy(True, 0).wait()
        h = _modulated_norm(xbuf[...], g_ref[...], mod_ref[0:1, :], mod_ref[1:2, :])
        h0_ref[...] = h.astype(BF16)

    @pl.when((j == 0) & (i + 1 < n_tiles))
    def _():
        start(i + 1)

    @pl.when((j == n_a) & (i + 1 < n_tiles))
    def _():
        copy(True, 0).wait()

    for cur, nxt in ((h0_ref, h1_ref), (h1_ref, h0_ref)):
        mine = slot == (0 if cur is h0_ref else 1)

        @pl.when((j < n_a) & mine)
        def _(cur=cur):
            _cast_blocks(side_in, side_out)
            os_ref[...] = jnp.dot(cur[...], wa_ref[...], preferred_element_type=F32).astype(BF16)

        @pl.when((j >= n_a) & mine)
        def _(cur=cur, nxt=nxt):
            _cast_blocks(side_in, side_out)
            rows = pl.ds(pl.multiple_of((j - n_a) * rc, rc), rc)
            hn = _modulated_norm(xbuf[rows, :], g_ref[...], modn_ref[0:1, :], modn_ref[1:2, :])
            nxt[rows, :] = hn.astype(BF16)
            wpart = wg_refs[0].shape[1]
            for part, w_ref in enumerate(wg_refs):
                cols = slice(part * wpart, (part + 1) * wpart)
                logits = jnp.dot(cur[...], w_ref[...], preferred_element_type=F32) + bg_ref[:, cols]
                og_ref[:, cols] = (0.5 * jnp.tanh(0.5 * logits) + 0.5).astype(BF16)


def _inproj(xa, xb, mod, norm_g, w, b_gate, layer, n_mix, dec_seq, side):
    d = xa.shape[1]
    n_ctx_rows = xa.shape[0]
    m = n_ctx_rows + xb.shape[0]
    n_gate = w.shape[1] - n_mix
    tm, ta, tg, wpart = 1024, 512, 2048, 512
    n_parts = tg // wpart
    assert n_mix % ta == 0 and n_mix % wpart == 0 and n_gate % tg == 0
    n_a, n_g = n_mix // ta, n_gate // tg
    n_tiles = m // tm
    row = functools.partial(_mod_row, tm=tm, n_ctx_rows=n_ctx_rows, dec_seq=dec_seq)
    a_col = lambda j: jnp.minimum(j, n_a - 1)
    g_col = lambda j: jnp.maximum(j - n_a, 0)
    n_inner = n_a + n_g
    side_blocks = 128
    assert side_blocks <= n_tiles * n_inner
    side_in, side_out, side_shapes = _side_specs(side, side_blocks, lambda i, j: i * n_inner + j)
    return pl.pallas_call(
        functools.partial(_inproj_kernel, n_ctx_rows // tm, n_a, n_g, n_parts, len(side)),
        grid=(n_tiles, n_inner),
        in_specs=[
            pl.BlockSpec(memory_space=pl.ANY),
            pl.BlockSpec(memory_space=pl.ANY),
            pl.BlockSpec((None, None, 6, d), lambda i, j: (layer, row(i), 0, 0)),
            pl.BlockSpec((None, None, 6, d), lambda i, j: (layer, row(jnp.minimum(i + 1, n_tiles - 1)), 0, 0)),
            pl.BlockSpec((None, 1, d), lambda i, j: (layer, 0, 0)),
            pl.BlockSpec((d, ta), lambda i, j: (0, a_col(j))),
        ] + [
            pl.BlockSpec((d, wpart), lambda i, j, part=part: (0, n_mix // wpart + n_parts * g_col(j) + part))
            for part in range(n_parts)
        ] + [
            pl.BlockSpec((None, 1, tg), lambda i, j: (layer, 0, g_col(j))),
        ] + side_in,
        out_specs=[
            pl.BlockSpec((tm, ta), lambda i, j: (i, a_col(j))),
            pl.BlockSpec((tm, tg), lambda i, j: (i, g_col(j))),
        ] + side_out,
        out_shape=[jax.ShapeDtypeStruct((m, n_mix), BF16), jax.ShapeDtypeStruct((m, n_gate), BF16)] + side_shapes,
        scratch_shapes=[pltpu.VMEM((tm, d), F32), pltpu.VMEM((tm, d), BF16), pltpu.VMEM((tm, d), BF16),
                        pltpu.SemaphoreType.DMA(())],
        compiler_params=_cparams(("arbitrary", "arbitrary")),
        name="inproj",
    )(xa, xb, mod, mod, norm_g, w, *([w] * n_parts), b_gate, *[arr for arr, _ in side])


def _head_norm(x, g, lane):
    row = lax.broadcasted_iota(jnp.int32, (LANES, LANES), 0)
    same_head = (row >= HEAD_DIM) == (lane >= HEAD_DIM)
    ones = jnp.where(same_head, 1.0, 0.0).astype(BF16)
    x2 = x * x
    hi = x2.astype(BF16)
    lo = (x2 - hi.astype(F32)).astype(BF16)
    ss = (jnp.dot(hi, ones, preferred_element_type=F32) + jnp.dot(lo, ones, preferred_element_type=F32))
    return x * lax.rsqrt(ss * (1.0 / HEAD_DIM) + EPS) * g


def _head_norm_xlu(x, g, lane):
    low = lane < HEAD_DIM
    x2 = x * x
    s_lo = jnp.sum(jnp.where(low, x2, 0.0), axis=-1, keepdims=True)
    s_hi = jnp.sum(jnp.where(low, 0.0, x2), axis=-1, keepdims=True)
    ss = jnp.where(low, s_lo, s_hi)
    return x * lax.rsqrt(ss * (1.0 / HEAD_DIM) + EPS) * g


def _rope(x, cos, sin_signed, lane):
    first = (lane & 16) == 0
    partner = jnp.where(first, pltpu.roll(x, LANES - 16, 1), pltpu.roll(x, 16, 1))
    return x * cos + partner * sin_signed


def _nt_dot(a, b):
    return lax.dot_general(a, b, (((1,), (1,)), ((), ())), preferred_element_type=F32)


def _head_mask(lane, h, value):
    return jnp.where((lane >= HEAD_DIM) == (h == 1), value, 0.0).astype(F32)


def _attn_ctx_kernel(first, q_ref, k_ref, v_ref, qg_ref, kg_ref, *rest):
    a_ref, kn_ref, vn_ref = rest[-3:]
    if first:
        for later in range(1, kn_ref.shape[0]):
            kn_ref[later] = jnp.zeros(kn_ref.shape[1:], F32)
            vn_ref[later] = jnp.zeros(vn_ref.shape[1:], F32)
        kn_ref, vn_ref = kn_ref.at[0], vn_ref.at[0]
    lane = lax.broadcasted_iota(jnp.int32, (1, LANES), 1)
    scale = HEAD_DIM ** -0.5
    for p in range(q_ref.shape[1] // LANES):
        cols = slice(p * LANES, (p + 1) * LANES)
        q = _head_norm_xlu(q_ref[:, cols].astype(F32), qg_ref[...], lane)
        k = _head_norm_xlu(k_ref[:, cols].astype(F32), kg_ref[...], lane)
        v = v_ref[:, cols].astype(F32)
        for h in range(2):
            kn_ref[2 * p + h] = k[:, h * HEAD_DIM:(h + 1) * HEAD_DIM]
            vn_ref[2 * p + h] = v[:, h * HEAD_DIM:(h + 1) * HEAD_DIM]
        kb = k.astype(BF16)
        acc = jnp.zeros(q.shape, F32)
        for h in range(2):
            s = _nt_dot((q * _head_mask(lane, h, scale)).astype(BF16), kb)
            e = jnp.exp(s - jnp.max(s, axis=-1, keepdims=True))
            o = jnp.dot(e.astype(BF16), (v * _head_mask(lane, h, 1.0)).astype(BF16),
                        preferred_element_type=F32)
            acc = acc + o / jnp.sum(e, axis=-1, keepdims=True)
        a_ref[:, cols] = acc.astype(BF16)


def _attn_ctx(p_arr, qg, kg, layer, n_layers, n_batch, seq, new_kv):
    w = 4 * LANES
    rows = n_batch * seq
    n_heads = w // HEAD_DIM
    blk = lambda c: pl.BlockSpec((seq, w), lambda b: (b, c))
    gain = pl.BlockSpec((None, 1, LANES), lambda b: (layer, 0, 0))
    first = new_kv is None
    assert first == (layer == 0)
    if first:
        cache = pl.BlockSpec((None, n_layers, n_heads, seq, HEAD_DIM), lambda b: (b, 0, 0, 0, 0))
    else:
        cache = pl.BlockSpec((None, None, n_heads, seq, HEAD_DIM), lambda b: (b, layer, 0, 0, 0))
    cache_shape = jax.ShapeDtypeStruct((n_batch, n_layers, n_heads, seq, HEAD_DIM), F32)
    carried = [] if first else list(new_kv)
    return pl.pallas_call(
        functools.partial(_attn_ctx_kernel, first),
        grid=(n_batch,),
        in_specs=[blk(0), blk(1), blk(2), gain, gain] + [pl.BlockSpec(memory_space=pl.ANY)] * len(carried),
        out_specs=[pl.BlockSpec((seq, w), lambda b: (b, 0)), cache, cache],
        out_shape=[jax.ShapeDtypeStruct((rows, w), BF16), cache_shape, cache_shape],
        input_output_aliases={5 + n: 1 + n for n in range(len(carried))},
        compiler_params=_cparams(("parallel",)),
        name="attn_ctx",
    )(p_arr, p_arr, p_arr, qg, kg, *carried)


def _row_window(r, n_rows):
    rs = min(max(r - WIN_ROWS // 2, 0), n_rows - WIN_ROWS)
    if rs % 2 == 0:
        return rs * GRID_W, WIN_ROWS * GRID_W, True
    return (rs - 1) * GRID_W, (WIN_ROWS + 2) * GRID_W, False


def _attn_dec_kernel(n_side, q_ref, k_ref, v_ref, qg_ref, kg_ref, cos_ref, sin_ref, kc_ref, vc_ref,
                     tab_ref, *refs):
    side_in = refs[:n_side]
    o_ref = refs[n_side]
    side_out = refs[n_side + 1:2 * n_side + 1]
    s_ref, sc_ref, p_ref, pc_ref, li_ref, tsh_ref = refs[2 * n_side + 1:]
    _cast_blocks(side_in, side_out)
    n = q_ref.shape[0]
    n_rows = n // GRID_W
    lane = lax.broadcasted_iota(jnp.int32, (1, LANES), 1)
    scale = HEAD_DIM ** -0.5
    q = _head_norm(q_ref[...].astype(F32), qg_ref[...], lane)
    k = _head_norm(k_ref[...].astype(F32), kg_ref[...], lane)
    qr = _rope(q, cos_ref[...], sin_ref[...], lane)
    kr = _rope(k, cos_ref[...], sin_ref[...], lane).astype(BF16)
    v = v_ref[...].astype(F32)
    qs = (q * scale).astype(BF16)

    def on_head_rows(xt, h):
        xb = xt.astype(BF16)
        zero = jnp.zeros_like(xb)
        return jnp.concatenate([xb, zero] if h == 0 else [zero, xb], axis=0)

    wide = (WIN_ROWS + 2) * GRID_W
    wl = lax.broadcasted_iota(jnp.int32, (1, wide), 1)
    edge_mask = jnp.where((wl < GRID_W) | (wl >= wide - GRID_W), NEG_INF, 0.0).astype(F32)
    heads = []
    for h in range(2):
        tsh_ref[h] = pltpu.roll(tab_ref[h], 2 * n - GRID_W, 1)
        heads.append(((qr * _head_mask(lane, h, scale)).astype(BF16),
                      (v * _head_mask(lane, h, 1.0)).astype(BF16),
                      on_head_rows(kc_ref[h], h), on_head_rows(vc_ref[h], h)))
    chunk_rows = s_ref.shape[1] // GRID_W
    for c in range(n_rows // chunk_rows):
        wins = [_row_window(r, n_rows) for r in range(c * chunk_rows, (c + 1) * chunk_rows)]
        a_c = min(a for a, _, _ in wins)
        e_c = max(a + w for a, w, _ in wins)
        w_c = e_c - a_c
        qrows = slice(c * chunk_rows * GRID_W, (c + 1) * chunk_rows * GRID_W)
        acc = jnp.zeros((chunk_rows * GRID_W, LANES), F32)
        for h in range(2):
            qh, vh, kch, vch = heads[h]
            slot = h
            s_ref[slot, :, 0:w_c] = _nt_dot(qh[qrows], kr[a_c:e_c])
            sc_ref[slot] = jnp.dot(qs[qrows], kch, preferred_element_type=F32)
            for rr in range(chunk_rows):
                r = c * chunk_rows + rr
                rows = pl.ds(rr * GRID_W, GRID_W)
                a, w, exact = wins[rr]
                if (n_rows - 1 - r) % 2 == 0:
                    bias = tab_ref[h, :, (n_rows - 1 - r) * GRID_W + a:(n_rows - 1 - r) * GRID_W + a + w]
                else:
                    bias = tsh_ref[h, :, (n_rows - 2 - r) * GRID_W + a:(n_rows - 2 - r) * GRID_W + a + w]
                sl = s_ref[slot, rows, a - a_c:a - a_c + w] + bias
                if not exact:
                    sl = sl + edge_mask
                sx = sc_ref[slot, rows, :]
                m = jnp.maximum(jnp.max(sl, axis=-1, keepdims=True), jnp.max(sx, axis=-1, keepdims=True))
                el = jnp.exp(sl - m)
                ex = jnp.exp(sx - m)
                den = jnp.sum(el, axis=-1, keepdims=True) + jnp.sum(ex, axis=-1, keepdims=True)
                pieces = [el.astype(BF16)]
                if a > a_c:
                    pieces.insert(0, jnp.zeros((GRID_W, a - a_c), BF16))
                if a + w < e_c:
                    pieces.append(jnp.zeros((GRID_W, e_c - a - w), BF16))
                p_ref[slot, rows, 0:w_c] = jnp.concatenate(pieces, axis=1) if len(pieces) > 1 else pieces[0]
                pc_ref[slot, rows, :] = ex.astype(BF16)
                li_ref[slot, rows, :] = jnp.broadcast_to(1.0 / den, (GRID_W, LANES))
            o = jnp.dot(p_ref[slot, :, 0:w_c], vh[a_c:e_c], preferred_element_type=F32)
            o = o + _nt_dot(pc_ref[slot], vch)
            acc = acc + o * li_ref[slot]
        o_ref[qrows, :] = acc.astype(BF16)


def _attn_dec(p_arr, qg, kg, cos, sin, kc, vc, tab, layer, n_batch, seq, row0, past, side):
    rb0 = row0 // seq
    n_pairs = 4
    side_in, side_out, side_shapes = _side_specs(side, n_batch * n_pairs, lambda b, p: b * n_pairs + p)
    chunk = 4 * GRID_W
    win = (WIN_ROWS + 4) * GRID_W
    blk = lambda c: pl.BlockSpec((seq, LANES), lambda b, p: (rb0 + b, c + p))
    gain = pl.BlockSpec((None, 1, LANES), lambda b, p: (layer, 0, 0))
    const = pl.BlockSpec((seq, LANES), lambda b, p: (0, 0))
    cache = pl.BlockSpec((None, None, 2, HEAD_DIM, past), lambda b, p: (b, layer, p, 0, 0))
    return pl.pallas_call(
        functools.partial(_attn_dec_kernel, len(side)),
        grid=(n_batch, n_pairs),
        in_specs=[
            blk(0), blk(n_pairs), blk(2 * n_pairs), gain, gain, const, const, cache, cache,
            pl.BlockSpec((None, 2, GRID_W, 2 * seq), lambda b, p: (layer, p, 0, 0)),
        ] + side_in,
        out_specs=[pl.BlockSpec((seq, LANES), lambda b, p: (b, p))] + side_out,
        out_shape=[jax.ShapeDtypeStruct((n_batch * seq, n_pairs * LANES), BF16)] + side_shapes,
        scratch_shapes=[
            pltpu.VMEM((2, chunk, win), F32), pltpu.VMEM((2, chunk, past), F32),
            pltpu.VMEM((2, chunk, win), BF16), pltpu.VMEM((2, chunk, past), BF16),
            pltpu.VMEM((2, chunk, LANES), F32), pltpu.VMEM((2, GRID_W, 2 * seq), F32),
        ],
        compiler_params=_cparams(("parallel", "parallel")),
        name="attn_dec",
    )(p_arr, p_arr, p_arr, qg, kg, cos, sin, kc, vc, tab, *[arr for arr, _ in side])


def _mix_kernel(u_ref, vg_ref, xf_ref, xp_ref, gn_ref, ws_ref, bs_ref, dft_ref, cs_ref, band_ref,
                cnt_ref, wp_ref, ps_ref, o_ref, vgn_ref, ab_ref):
    n = u_ref.shape[0]
    bw = u_ref.shape[1]
    n_groups = bw // CHUNK
    for c in range(n // CHUNK):
        rows = pl.ds(c * CHUNK, CHUNK)
        vg = jax.nn.gelu(vg_ref[rows, :].astype(F32))
        ms = jnp.mean(vg * vg, axis=-1, keepdims=True)
        vgn_ref[rows, :] = (vg * lax.rsqrt(ms + EPS) * gn_ref[...]).astype(BF16)
    for c in range(n // CHUNK):
        rows = pl.ds(c * CHUNK, CHUNK)
        parts = [jnp.dot(ws_ref[g], vgn_ref[rows, g * CHUNK:(g + 1) * CHUNK], preferred_element_type=F32)
                 for g in range(n_groups)]
        mixed = jnp.concatenate(parts, axis=1) + bs_ref[...]
        o_ref[rows, 0:bw] = (jax.nn.gelu(u_ref[rows, :].astype(F32)) * mixed).astype(BF16)
    ab_ref[...] = jnp.dot(dft_ref[...], xf_ref[...], preferred_element_type=F32).astype(BF16)
    for g in range(n_groups):
        gc = slice(g * CHUNK, (g + 1) * CHUNK)
        lhs = jnp.concatenate([ab_ref[0:n, gc], ab_ref[n:2 * n, gc]], axis=1)
        o_ref[:, bw + g * CHUNK:bw + (g + 1) * CHUNK] = jnp.dot(
            lhs, cs_ref[...], preferred_element_type=F32).astype(BF16)
    for g in range(n_groups):
        gc = slice(g * CHUNK, (g + 1) * CHUNK)
        xg = xp_ref[:, gc]
        wsum = jnp.dot(band_ref[g], xg, preferred_element_type=F32)
        pooled = wsum / cnt_ref[:, gc] - xg.astype(F32)
        y = jnp.dot(pooled.astype(BF16), wp_ref[g], preferred_element_type=F32) * ps_ref[:, gc]
        o_ref[:, 2 * bw + g * CHUNK:2 * bw + (g + 1) * CHUNK] = y.astype(BF16)


def _mix(p_arr, gn, ws, bs, dft, cs, band, cnt, wp, ps, layer, n_ctx_rows):
    m = p_arr.shape[0]
    bw = gn.shape[-1]
    n = MIX_ROWS
    n_groups = bw // CHUNK
    kind = lambda i: jnp.where(i < n_ctx_rows // n, 0, 1)
    blk = lambda c: pl.BlockSpec((n, bw), lambda i: (i, c))
    return pl.pallas_call(
        _mix_kernel,
        grid=(m // n,),
        in_specs=[
            blk(3), blk(4), blk(5), blk(6),
            pl.BlockSpec((None, 1, bw), lambda i: (layer, 0, 0)),
            pl.BlockSpec((None, n_groups, CHUNK, CHUNK), lambda i: (layer, 0, 0, 0)),
            pl.BlockSpec((None, CHUNK, bw), lambda i: (layer, 0, 0)),
            pl.BlockSpec((None, 2 * n, n), lambda i: (kind(i), 0, 0)),
            pl.BlockSpec((2 * CHUNK, CHUNK), lambda i: (0, 0)),
            pl.BlockSpec((None, n_groups, n, n), lambda i: (kind(i), 0, 0, 0)),
            pl.BlockSpec((None, n, bw), lambda i: (kind(i), 0, 0)),
            pl.BlockSpec((None, n_groups, CHUNK, CHUNK), lambda i: (layer, 0, 0, 0)),
            pl.BlockSpec((None, 1, bw), lambda i: (layer, 0, 0)),
        ],
        out_specs=pl.BlockSpec((n, 3 * bw), lambda i: (i, 0)),
        out_shape=jax.ShapeDtypeStruct((m, 3 * bw), BF16),
        scratch_shapes=[pltpu.VMEM((n, bw), BF16), pltpu.VMEM((2 * n, bw), BF16)],
        compiler_params=_cparams(("parallel",)),
        name="mix",
    )(p_arr, p_arr, p_arr, p_arr, gn, ws, bs, dft, cs, band, cnt, wp, ps)


def _merge_kernel(n_ctx_tiles, xa_ref, xb_ref, mod_ref, g2_ref, gate_ref, actx_ref, adec_ref, bcd_ref,
                  wb_ref, wo_ref, o_ref, h_ref):
    d = xa_ref.shape[1]
    bw = actx_ref.shape[1]
    is_ctx = pl.program_id(0) < n_ctx_tiles
    merged = jnp.zeros(xa_ref.shape, F32)
    for b in range(d // bw):
        if b == 0:
            br = jnp.where(is_ctx, actx_ref[...], adec_ref[...])
        else:
            br = bcd_ref[:, (b - 1) * bw:b * bw]
        y = jnp.dot(br, wb_ref[b * bw:(b + 1) * bw, :], preferred_element_type=F32)
        merged = merged + gate_ref[:, b * d:(b + 1) * d].astype(F32) * y
    mix = jnp.dot(merged.astype(BF16), wo_ref[...], preferred_element_type=F32)
    x = jnp.where(is_ctx, xa_ref[...], xb_ref[...])
    x1 = x + mod_ref[2:3, :] * mix
    o_ref[...] = x1
    h_ref[...] = _modulated_norm(x1, g2_ref[...], mod_ref[3:4, :], mod_ref[4:5, :]).astype(BF16)


def _merge(xa, xb, mod, norm2_g, gates, a_ctx, a_dec, bcd, wb, wo, layer, dec_seq):
    d = xa.shape[1]
    n_ctx_rows = xa.shape[0]
    m = n_ctx_rows + xb.shape[0]
    bw = a_ctx.shape[1]
    tm = 256
    nct = n_ctx_rows // tm
    row = functools.partial(_mod_row, tm=tm, n_ctx_rows=n_ctx_rows, dec_seq=dec_seq)
    once = pl.Buffered(1)
    ctx_tile = lambda i: (jnp.minimum(i, nct - 1), 0)
    dec_tile = lambda i: (jnp.maximum(i - nct, 0), 0)
    return pl.pallas_call(
        functools.partial(_merge_kernel, nct),
        grid=(m // tm,),
        in_specs=[
            pl.BlockSpec((tm, d), ctx_tile),
            pl.BlockSpec((tm, d), dec_tile),
            pl.BlockSpec((None, None, 6, d), lambda i: (layer, row(i), 0, 0)),
            pl.BlockSpec((None, 1, d), lambda i: (layer, 0, 0)),
            pl.BlockSpec((tm, 4 * d), lambda i: (i, 0)),
            pl.BlockSpec((tm, bw), ctx_tile),
            pl.BlockSpec((tm, bw), dec_tile),
            pl.BlockSpec((tm, 3 * bw), lambda i: (i, 0)),
            pl.BlockSpec((d, d), lambda i: (0, 0), pipeline_mode=once),
            pl.BlockSpec((d, d), lambda i: (0, 0), pipeline_mode=once),
        ],
        out_specs=[pl.BlockSpec((tm, d), lambda i: (i, 0))] * 2,
        out_shape=[jax.ShapeDtypeStruct((m, d), F32), jax.ShapeDtypeStruct((m, d), BF16)],
        compiler_params=_cparams(("parallel",)),
        name="merge",
    )(xa, xb, mod, norm2_g, gates, a_ctx, a_dec, bcd, wb, wo)


def _mlp_kernel(n_ctx_tiles, n_up, n_side, h_ref, x_ref, mod_ref, w1_ref, w2_ref, *refs):
    side_in = refs[:n_side]
    oa_ref, ob_ref = refs[n_side:n_side + 2]
    side_out = refs[n_side + 2:2 * n_side + 2]
    a_ref = refs[2 * n_side + 2]
    s = pl.program_id(1)
    is_ctx = pl.program_id(0) < n_ctx_tiles
    tf = w1_ref.shape[1]

    @pl.when(s < n_up)
    def _():
        _cast_blocks(side_in, side_out)
        a = jnp.dot(h_ref[...], w1_ref[...], preferred_element_type=F32)
        a_ref[s] = jnp.square(jnp.maximum(a, 0.0)).astype(BF16)

    def down():
        _cast_blocks(side_in, side_out)
        acc = jnp.dot(a_ref[0], w2_ref[0:tf, :], preferred_element_type=F32)
        for c in range(1, n_up):
            acc = acc + jnp.dot(a_ref[c], w2_ref[c * tf:(c + 1) * tf, :], preferred_element_type=F32)
        return x_ref[...] + mod_ref[5:6, :] * acc

    @pl.when((s >= n_up) & is_ctx)
    def _():
        oa_ref[...] = down()

    @pl.when((s >= n_up) & jnp.logical_not(is_ctx))
    def _():
        ob_ref[...] = down()


def _mlp(h, x, mod, w1, w2, layer, n_ctx_rows, dec_seq, side):
    m, d = x.shape
    dff = w1.shape[1]
    tm, tf, tn = 1024, 1024, 256
    n_up, n_down = dff // tf, d // tn
    nct = n_ctx_rows // tm
    row = functools.partial(_mod_row, tm=tm, n_ctx_rows=n_ctx_rows, dec_seq=dec_seq)
    col = lambda s: jnp.maximum(s - n_up, 0)
    ctx_out = lambda i, s: (jnp.minimum(i, nct - 1), jnp.where(i < nct, col(s), n_down - 1))
    dec_out = lambda i, s: (jnp.maximum(i - nct, 0), jnp.where(i < nct, 0, col(s)))
    n_inner = n_up + n_down
    side_blocks = 128
    assert side_blocks <= (m // tm) * n_inner
    side_in, side_out, side_shapes = _side_specs(side, side_blocks, lambda i, s: i * n_inner + s)
    return pl.pallas_call(
        functools.partial(_mlp_kernel, nct, n_up, len(side)),
        grid=(m // tm, n_inner),
        in_specs=[
            pl.BlockSpec((tm, d), lambda i, s: (i, 0)),
            pl.BlockSpec((tm, tn), lambda i, s: (i, col(s))),
            pl.BlockSpec((None, None, 6, tn), lambda i, s: (layer, row(i), 0, col(s))),
            pl.BlockSpec((d, tf), lambda i, s: (0, jnp.minimum(s, n_up - 1))),
            pl.BlockSpec((dff, tn), lambda i, s: (0, col(s))),
        ] + side_in,
        out_specs=[pl.BlockSpec((tm, tn), ctx_out), pl.BlockSpec((tm, tn), dec_out)] + side_out,
        out_shape=[jax.ShapeDtypeStruct((n_ctx_rows, d), F32),
                   jax.ShapeDtypeStruct((m - n_ctx_rows, d), F32)] + side_shapes,
        scratch_shapes=[pltpu.VMEM((n_up, tm, tf), BF16)],
        compiler_params=_cparams(("arbitrary", "arbitrary")),
        name="mlp",
    )(h, x, mod, w1, w2, *[arr for arr, _ in side])


def _dft_tables(n_ctx_seq, n):
    def cs(size):
        k = np.arange(size)
        ang = 2.0 * np.pi * ((k[:, None] * k[None, :]) % size) / size
        return np.cos(ang) / np.sqrt(size), np.sin(ang) / np.sqrt(size)

    c_s, s_s = cs(n_ctx_seq)
    eye = np.eye(n // n_ctx_seq)
    c_l, s_l = cs(n)
    pos = np.stack([np.concatenate([np.kron(eye, c_s), np.kron(eye, s_s)], axis=0),
                    np.concatenate([c_l, s_l], axis=0)])
    c_c, s_c = cs(CHUNK)
    chan = np.concatenate([c_c, -s_c], axis=0)
    return jnp.asarray(pos, F32).astype(BF16), jnp.asarray(chan, F32).astype(BF16)


def _pool_tables(n_ctx_seq, n):
    def one(size):
        t = np.arange(size)
        bands, counts = [], []
        for w in POOL_WINDOWS:
            lo = np.clip(t - w // 2, 0, size - 1)
            hi = np.clip(t + w // 2 - 1, 0, size - 1)
            s = np.arange(size)[None, :]
            bands.append(((s >= lo[:, None]) & (s <= hi[:, None])).astype(np.float32))
            counts.append((hi - lo + 1).astype(np.float32))
        return np.stack(bands), np.stack(counts)

    b_s, c_s = one(n_ctx_seq)
    rep = n // n_ctx_seq
    b_l, c_l = one(n)
    band = np.stack([np.stack([np.kron(np.eye(rep), b) for b in b_s]), b_l])
    cnt = np.stack([np.tile(c_s, (1, rep)), c_l])
    cnt = np.repeat(cnt.transpose(0, 2, 1), CHUNK, axis=2)
    return jnp.asarray(band, BF16), jnp.asarray(cnt, F32)


def _rope_tables(n):
    t = jnp.arange(n)
    pos = jnp.stack([t // GRID_W, t % GRID_W], axis=-1).astype(F32)
    half = HEAD_DIM // 2
    inv = 1.0 / (ROPE_BASE ** (jnp.arange(0, half, 2, dtype=F32) / half))
    ang = pos[:, :, None] * inv
    cos, sin = jnp.cos(ang), jnp.sin(ang)
    cos_h = jnp.concatenate([cos[:, 0], cos[:, 0], cos[:, 1], cos[:, 1]], axis=-1)
    sin_h = jnp.concatenate([-sin[:, 0], sin[:, 0], -sin[:, 1], sin[:, 1]], axis=-1)
    return jnp.tile(cos_h, (1, 2)), jnp.tile(sin_h, (1, 2))


def _bias_table(rpb, n):
    n_layers, n_heads, n_d, n_e = rpb.shape
    cq = np.arange(GRID_W)
    cs = np.clip(cq - WIN_COLS // 2, 0, GRID_W - WIN_COLS)
    kc = np.arange(GRID_W)
    col_ok = (kc[None, :] >= cs[:, None]) & (kc[None, :] < cs[:, None] + WIN_COLS)
    col_mask = np.where(col_ok, 0.0, NEG_INF).astype(np.float32)
    dc_idx = np.clip(kc[None, :] - cq[:, None], -(WIN_COLS - 1), WIN_COLS - 1) + WIN_COLS - 1
    pick = (dc_idx[None, :, :] == np.arange(n_e)[:, None, None]).astype(np.float32)
    t = jnp.einsum("lhde,eck->lhcdk", rpb, jnp.asarray(pick), precision=lax.Precision.HIGHEST)
    t = t + jnp.asarray(col_mask)[:, None, :]
    flat = t.reshape(n_layers, n_heads, GRID_W, n_d * GRID_W)
    front = WIN_ROWS * GRID_W
    back = 2 * n - front - n_d * GRID_W
    return jnp.pad(flat, ((0, 0), (0, 0), (0, 0), (front, back)))


def kernel(x_prompt, x_sample, cache_k, cache_v, c, c_ctx, norm1_g, w_in, b_gate, q_norm_g, k_norm_g,
           rpb, gmlp_norm_g, w_spatial, b_spatial, w_pool, pool_scale, w_branch, w_out, norm2_g,
           w_mlp1, w_mlp2, w_ada, b_ada):
    n_b, seq, d = x_prompt.shape
    n_db, dec_seq, _ = x_sample.shape
    n_layers = w_in.shape[0]
    n_heads, past = cache_k.shape[2], cache_k.shape[3]
    bw = n_heads * HEAD_DIM
    n_ctx_rows = n_b * seq
    assert dec_seq == MIX_ROWS and MIX_ROWS % seq == 0 and n_ctx_rows % MIX_ROWS == 0
    assert dec_seq // GRID_W == 2 * WIN_ROWS and bw == 4 * LANES

    xa = x_prompt.reshape(n_ctx_rows, d)
    xb = x_sample.reshape(n_db * dec_seq, d)
    pad_rows = (-(1 + n_db)) % 8
    cond = jnp.concatenate([c_ctx[None, :], c, jnp.zeros((pad_rows, d), F32)], axis=0)
    mod = _ada(cond, w_ada, b_ada).reshape(n_layers, cond.shape[0], 6, d)

    w_in_l = w_in[0].astype(BF16)
    kct = cache_k.transpose(0, 1, 2, 4, 3)
    vct = cache_v.transpose(0, 1, 2, 4, 3)
    ws_b, wp_b = w_spatial.astype(BF16), w_pool.astype(BF16)
    bs_x = jnp.repeat(b_spatial.transpose(0, 2, 1), CHUNK, axis=2)
    qg = jnp.tile(q_norm_g, (1, 2))[:, None, :]
    kg = jnp.tile(k_norm_g, (1, 2))[:, None, :]
    dft, chan = _dft_tables(seq, MIX_ROWS)
    band, cnt = _pool_tables(seq, MIX_ROWS)
    cos, sin = _rope_tables(dec_seq)
    tab = _bias_table(rpb, dec_seq)
    n1 = norm1_g[:, None, :]
    n2 = norm2_g[:, None, :]
    gn = gmlp_norm_g[:, None, :]
    ps = pool_scale[:, None, :]
    bg = b_gate[:, None, :]

    new_kv = None
    for l in range(n_layers):
        p_arr, gates = _inproj(xa, xb, mod, n1, w_in_l, bg, l, 7 * bw, dec_seq, side=[])
        a_ctx, *new_kv = _attn_ctx(p_arr, qg, kg, l, n_layers, n_b, seq, new_kv)
        own = [(w_mlp1, l), (w_mlp2, l), (w_branch, l), (w_out, l)] if l == 0 else []
        a_dec, *cast = _attn_dec(p_arr, qg, kg, cos, sin, kct, vct, tab, l, n_db, dec_seq, n_ctx_rows, past,
                                 side=own)
        if cast:
            w1_l, w2_l, wb_l, wo_l = cast
        bcd = _mix(p_arr, gn, ws_b, bs_x, dft, chan, band, cnt, wp_b, ps, l, n_ctx_rows)
        x1, h2 = _merge(xa, xb, mod, n2, gates, a_ctx, a_dec, bcd, wb_l, wo_l, l, dec_seq)
        nxt = l + 1
        ahead = [(w_in, nxt), (w_mlp1, nxt), (w_mlp2, nxt), (w_branch, nxt), (w_out, nxt)] if nxt < n_layers else []
        xa, xb, *cast = _mlp(h2, x1, mod, w1_l, w2_l, l, n_ctx_rows, dec_seq, side=ahead)
        if cast:
            w_in_l, w1_l, w2_l, wb_l, wo_l = cast

    return xa.reshape(n_b, seq, d), xb.reshape(n_db, dec_seq, d), new_kv[0], new_kv[1]
```

```python
import functools

import numpy as np
import jax
import jax.numpy as jnp
from jax import lax
from jax.experimental import pallas as pl
from jax.experimental.pallas import tpu as pltpu

F32 = jnp.float32
BF16 = jnp.bfloat16

GRID_W = 64
HEAD_DIM = 64
WIN_ROWS = 8
WIN_COLS = 16
ROPE_BASE = 10000.0
CHUNK = 128
POOL_WINDOWS = (2, 4, 8, 16)
EPS = 1e-6
NEG_INF = -1e30

LANES = 128
MIX_ROWS = 1024
VMEM_LIMIT = 60 * 1024 * 1024


def _cparams(sem):
    return pltpu.CompilerParams(dimension_semantics=sem, vmem_limit_bytes=VMEM_LIMIT)


def _mod_row(i, tm, n_ctx_rows, dec_seq):
    nc = n_ctx_rows // tm
    per = dec_seq // tm
    return jnp.where(i < nc, 0, 1 + (i - nc) // per)


def _ada_kernel(c_ref, w_ref, b_ref, o_ref):
    s = jax.nn.silu(c_ref[...]).astype(BF16)
    o_ref[...] = jnp.dot(s, w_ref[...].astype(BF16), preferred_element_type=F32) + b_ref[...]


def _ada(cond, w_ada, b_ada):
    n_layers, d, n = w_ada.shape
    rows = cond.shape[0]
    tn = 1024
    return pl.pallas_call(
        _ada_kernel,
        grid=(n_layers, n // tn),
        in_specs=[
            pl.BlockSpec((rows, d), lambda l, j: (0, 0)),
            pl.BlockSpec((None, d, tn), lambda l, j: (l, 0, j)),
            pl.BlockSpec((None, 1, tn), lambda l, j: (l, 0, j)),
        ],
        out_specs=pl.BlockSpec((None, rows, tn), lambda l, j: (l, 0, j)),
        out_shape=jax.ShapeDtypeStruct((n_layers, rows, n), F32),
        compiler_params=_cparams(("parallel", "parallel")),
        name="ada",
    )(cond, w_ada, b_ada.reshape(n_layers, 1, n))


def _modulated_norm(x, g, shift, scale):
    ms = jnp.mean(x * x, axis=-1, keepdims=True)
    return (x * lax.rsqrt(ms + EPS) * g) * (1.0 + scale) + shift


def _cast_blocks(src_refs, dst_refs):
    for src, dst in zip(src_refs, dst_refs):
        dst[...] = src[...].astype(BF16)


def _side_specs(side, n_blocks, step_of):
    in_specs, out_specs, out_shapes = [], [], []
    for arr, layer in side:
        _, r, c = arr.shape
        assert r % (16 * n_blocks) == 0
        rb = r // n_blocks
        blk = lambda *ids: jnp.minimum(step_of(*ids), n_blocks - 1)
        in_specs.append(pl.BlockSpec((None, rb, c), lambda *ids, layer=layer: (layer, blk(*ids), 0)))
        out_specs.append(pl.BlockSpec((rb, c), lambda *ids: (blk(*ids), 0)))
        out_shapes.append(jax.ShapeDtypeStruct((r, c), BF16))
    return in_specs, out_specs, out_shapes


def _x_tile_copy(xa_hbm, xb_hbm, use_a, tile, nct, tm, xbuf, sem):
    if use_a:
        src = xa_hbm.at[pl.ds(tile * tm, tm), :]
    else:
        src = xb_hbm.at[pl.ds((tile - nct) * tm, tm), :]
    return pltpu.make_async_copy(src, xbuf, sem)


def _inproj_kernel(nct, n_a, n_g, n_parts, n_side, xa_hbm, xb_hbm, mod_ref, modn_ref, g_ref, wa_ref, *refs):
    wg_refs, bg_ref, refs = refs[:n_parts], refs[n_parts], refs[n_parts + 1:]
    side_in = refs[:n_side]
    os_ref, og_ref = refs[n_side:n_side + 2]
    side_out = refs[n_side + 2:2 * n_side + 2]
    xbuf, h0_ref, h1_ref, sem = refs[2 * n_side + 2:]
    i = pl.program_id(0)
    j = pl.program_id(1)
    n_tiles = pl.num_programs(0)
    tm = xbuf.shape[0]
    rc = tm // n_g
    slot = i % 2
    copy = functools.partial(_x_tile_copy, xa_hbm, xb_hbm, nct=nct, tm=tm, xbuf=xbuf, sem=sem)

    def start(tile):
        @pl.when(tile < nct)
        def _():
            copy(True, tile).start()

        @pl.when(tile >= nct)
        def _():
            copy(False, tile).start()

    @pl.when((i == 0) & (j == 0))
    def _():
        start(i)
        copy(True, 0).wait()
        h = _modulated_norm(xbuf[...], g_ref[...], mod_ref[0:1, :], mod_ref[1:2, :])
        h0_ref[...] = h.astype(BF16)

    @pl.when((j == 0) & (i + 1 < n_tiles))
    def _():
        start(i + 1)

    @pl.when((j == n_a) & (i + 1 < n_tiles))
    def _():
        copy(True, 0).wait()

    for cur, nxt in ((h0_ref, h1_ref), (h1_ref, h0_ref)):
        mine = slot == (0 if cur is h0_ref else 1)

        @pl.when((j < n_a) & mine)
        def _(cur=cur):
            _cast_blocks(side_in, side_out)
            os_ref[...] = jnp.dot(cur[...], wa_ref[...], preferred_element_type=F32).astype(BF16)

        @pl.when((j >= n_a) & mine)
        def _(cur=cur, nxt=nxt):
            _cast_blocks(side_in, side_out)
            rows = pl.ds(pl.multiple_of((j - n_a) * rc, rc), rc)
            hn = _modulated_norm(xbuf[rows, :], g_ref[...], modn_ref[0:1, :], modn_ref[1:2, :])
            nxt[rows, :] = hn.astype(BF16)
            wpart = wg_refs[0].shape[1]
            for part, w_ref in enumerate(wg_refs):
                cols = slice(part * wpart, (part + 1) * wpart)
                logits = jnp.dot(cur[...], w_ref[...], preferred_element_type=F32) + bg_ref[:, cols]
                og_ref[:, cols] = (0.5 * jnp.tanh(0.5 * logits) + 0.5).astype(BF16)


def _inproj(xa, xb, mod, norm_g, w, b_gate, layer, n_mix, dec_seq, side):
    d = xa.shape[1]
    n_ctx_rows = xa.shape[0]
    m = n_ctx_rows + xb.shape[0]
    n_gate = w.shape[1] - n_mix
    tm, ta, tg, wpart = 1024, 512, 2048, 512
    n_parts = tg // wpart
    assert n_mix % ta == 0 and n_mix % wpart == 0 and n_gate % tg == 0
    n_a, n_g = n_mix // ta, n_gate // tg
    n_tiles = m // tm
    row = functools.partial(_mod_row, tm=tm, n_ctx_rows=n_ctx_rows, dec_seq=dec_seq)
    a_col = lambda j: jnp.minimum(j, n_a - 1)
    g_col = lambda j: jnp.maximum(j - n_a, 0)
    n_inner = n_a + n_g
    side_blocks = 128
    assert side_blocks <= n_tiles * n_inner
    side_in, side_out, side_shapes = _side_specs(side, side_blocks, lambda i, j: i * n_inner + j)
    return pl.pallas_call(
        functools.partial(_inproj_kernel, n_ctx_rows // tm, n_a, n_g, n_parts, len(side)),
        grid=(n_tiles, n_inner),
        in_specs=[
            pl.BlockSpec(memory_space=pl.ANY),
            pl.BlockSpec(memory_space=pl.ANY),
            pl.BlockSpec((None, None, 6, d), lambda i, j: (layer, row(i), 0, 0)),
            pl.BlockSpec((None, None, 6, d), lambda i, j: (layer, row(jnp.minimum(i + 1, n_tiles - 1)), 0, 0)),
            pl.BlockSpec((None, 1, d), lambda i, j: (layer, 0, 0)),
            pl.BlockSpec((d, ta), lambda i, j: (0, a_col(j))),
        ] + [
            pl.BlockSpec((d, wpart), lambda i, j, part=part: (0, n_mix // wpart + n_parts * g_col(j) + part))
            for part in range(n_parts)
        ] + [
            pl.BlockSpec((None, 1, tg), lambda i, j: (layer, 0, g_col(j))),
        ] + side_in,
        out_specs=[
            pl.BlockSpec((tm, ta), lambda i, j: (i, a_col(j))),
            pl.BlockSpec((tm, tg), lambda i, j: (i, g_col(j))),
        ] + side_out,
        out_shape=[jax.ShapeDtypeStruct((m, n_mix), BF16), jax.ShapeDtypeStruct((m, n_gate), BF16)] + side_shapes,
        scratch_shapes=[pltpu.VMEM((tm, d), F32), pltpu.VMEM((tm, d), BF16), pltpu.VMEM((tm, d), BF16),
                        pltpu.SemaphoreType.DMA(())],
        compiler_params=_cparams(("arbitrary", "arbitrary")),
        name="inproj",
    )(xa, xb, mod, mod, norm_g, w, *([w] * n_parts), b_gate, *[arr for arr, _ in side])


def _head_norm(x, g, lane):
    row = lax.broadcasted_iota(jnp.int32, (LANES, LANES), 0)
    same_head = (row >= HEAD_DIM) == (lane >= HEAD_DIM)
    ones = jnp.where(same_head, 1.0, 0.0).astype(BF16)
    x2 = x * x
    hi = x2.astype(BF16)
    lo = (x2 - hi.astype(F32)).astype(BF16)
    ss = (jnp.dot(hi, ones, preferred_element_type=F32) + jnp.dot(lo, ones, preferred_element_type=F32))
    return x * lax.rsqrt(ss * (1.0 / HEAD_DIM) + EPS) * g


def _head_norm_xlu(x, g, lane):
    low = lane < HEAD_DIM
    x2 = x * x
    s_lo = jnp.sum(jnp.where(low, x2, 0.0), axis=-1, keepdims=True)
    s_hi = jnp.sum(jnp.where(low, 0.0, x2), axis=-1, keepdims=True)
    ss = jnp.where(low, s_lo, s_hi)
    return x * lax.rsqrt(ss * (1.0 / HEAD_DIM) + EPS) * g


def _rope(x, cos, sin_signed, lane):
    first = (lane & 16) == 0
    partner = jnp.where(first, pltpu.roll(x, LANES - 16, 1), pltpu.roll(x, 16, 1))
    return x * cos + partner * sin_signed


def _nt_dot(a, b):
    return lax.dot_general(a, b, (((1,), (1,)), ((), ())), preferred_element_type=F32)


def _head_mask(lane, h, value):
    return jnp.where((lane >= HEAD_DIM) == (h == 1), value, 0.0).astype(F32)


def _attn_ctx_kernel(first, q_ref, k_ref, v_ref, qg_ref, kg_ref, *rest):
    a_ref, kn_ref, vn_ref = rest[-3:]
    if first:
        for later in range(1, kn_ref.shape[0]):
            kn_ref[later] = jnp.zeros(kn_ref.shape[1:], F32)
            vn_ref[later] = jnp.zeros(vn_ref.shape[1:], F32)
        kn_ref, vn_ref = kn_ref.at[0], vn_ref.at[0]
    lane = lax.broadcasted_iota(jnp.int32, (1, LANES), 1)
    scale = HEAD_DIM ** -0.5
    for p in range(q_ref.shape[1] // LANES):
        cols = slice(p * LANES, (p + 1) * LANES)
        q = _head_norm_xlu(q_ref[:, cols].astype(F32), qg_ref[...], lane)
        k = _head_norm_xlu(k_ref[:, cols].astype(F32), kg_ref[...], lane)
        v = v_ref[:, cols].astype(F32)
        kt, vt = k.T, v.T
        for h in range(2):
            kn_ref[2 * p + h] = kt[h * HEAD_DIM:(h + 1) * HEAD_DIM, :]
            vn_ref[2 * p + h] = vt[h * HEAD_DIM:(h + 1) * HEAD_DIM, :]
        kb = k.astype(BF16)
        acc = jnp.zeros(q.shape, F32)
        for h in range(2):
            s = _nt_dot((q * _head_mask(lane, h, scale)).astype(BF16), kb)
            e = jnp.exp(s - jnp.max(s, axis=-1, keepdims=True))
            o = jnp.dot(e.astype(BF16), (v * _head_mask(lane, h, 1.0)).astype(BF16),
                        preferred_element_type=F32)
            acc = acc + o / jnp.sum(e, axis=-1, keepdims=True)
        a_ref[:, cols] = acc.astype(BF16)


def _attn_ctx(p_arr, qg, kg, layer, n_layers, n_batch, seq, new_kv):
    w = 4 * LANES
    rows = n_batch * seq
    n_heads = w // HEAD_DIM
    blk = lambda c: pl.BlockSpec((seq, w), lambda b: (b, c))
    gain = pl.BlockSpec((None, 1, LANES), lambda b: (layer, 0, 0))
    first = new_kv is None
    assert first == (layer == 0)
    if first:
        cache = pl.BlockSpec((None, n_layers, n_heads, HEAD_DIM, seq), lambda b: (b, 0, 0, 0, 0))
    else:
        cache = pl.BlockSpec((None, None, n_heads, HEAD_DIM, seq), lambda b: (b, layer, 0, 0, 0))
    cache_shape = jax.ShapeDtypeStruct((n_batch, n_layers, n_heads, HEAD_DIM, seq), F32)
    carried = [] if first else list(new_kv)
    return pl.pallas_call(
        functools.partial(_attn_ctx_kernel, first),
        grid=(n_batch,),
        in_specs=[blk(0), blk(1), blk(2), gain, gain] + [pl.BlockSpec(memory_space=pl.ANY)] * len(carried),
        out_specs=[pl.BlockSpec((seq, w), lambda b: (b, 0)), cache, cache],
        out_shape=[jax.ShapeDtypeStruct((rows, w), BF16), cache_shape, cache_shape],
        input_output_aliases={5 + n: 1 + n for n in range(len(carried))},
        compiler_params=_cparams(("parallel",)),
        name="attn_ctx",
    )(p_arr, p_arr, p_arr, qg, kg, *carried)


def _row_window(r, n_rows):
    rs = min(max(r - WIN_ROWS // 2, 0), n_rows - WIN_ROWS)
    if rs % 2 == 0:
        return rs * GRID_W, WIN_ROWS * GRID_W, True
    return (rs - 1) * GRID_W, (WIN_ROWS + 2) * GRID_W, False


def _attn_dec_kernel(n_side, q_ref, k_ref, v_ref, qg_ref, kg_ref, cos_ref, sin_ref, kc_ref, vc_ref,
                     tab_ref, *refs):
    side_in = refs[:n_side]
    o_ref = refs[n_side]
    side_out = refs[n_side + 1:2 * n_side + 1]
    s_ref, sc_ref, p_ref, pc_ref, li_ref, tsh_ref = refs[2 * n_side + 1:]
    _cast_blocks(side_in, side_out)
    n = q_ref.shape[0]
    n_rows = n // GRID_W
    lane = lax.broadcasted_iota(jnp.int32, (1, LANES), 1)
    scale = HEAD_DIM ** -0.5
    q = _head_norm(q_ref[...].astype(F32), qg_ref[...], lane)
    k = _head_norm(k_ref[...].astype(F32), kg_ref[...], lane)
    qr = _rope(q, cos_ref[...], sin_ref[...], lane)
    kr = _rope(k, cos_ref[...], sin_ref[...], lane).astype(BF16)
    v = v_ref[...].astype(F32)
    qs = (q * scale).astype(BF16)

    def on_head_rows(xt, h):
        xb = xt.astype(BF16)
        zero = jnp.zeros_like(xb)
        return jnp.concatenate([xb, zero] if h == 0 else [zero, xb], axis=0)

    wide = (WIN_ROWS + 2) * GRID_W
    wl = lax.broadcasted_iota(jnp.int32, (1, wide), 1)
    edge_mask = jnp.where((wl < GRID_W) | (wl >= wide - GRID_W), NEG_INF, 0.0).astype(F32)
    heads = []
    for h in range(2):
        tsh_ref[h] = pltpu.roll(tab_ref[h], 2 * n - GRID_W, 1)
        heads.append(((qr * _head_mask(lane, h, scale)).astype(BF16),
                      (v * _head_mask(lane, h, 1.0)).astype(BF16),
                      on_head_rows(kc_ref[h], h), on_head_rows(vc_ref[h], h)))
    chunk_rows = s_ref.shape[1] // GRID_W
    for c in range(n_rows // chunk_rows):
        wins = [_row_window(r, n_rows) for r in range(c * chunk_rows, (c + 1) * chunk_rows)]
        a_c = min(a for a, _, _ in wins)
        e_c = max(a + w for a, w, _ in wins)
        w_c = e_c - a_c
        qrows = slice(c * chunk_rows * GRID_W, (c + 1) * chunk_rows * GRID_W)
        acc = jnp.zeros((chunk_rows * GRID_W, LANES), F32)
        for h in range(2):
            qh, vh, kch, vch = heads[h]
            slot = h
            s_ref[slot, :, 0:w_c] = _nt_dot(qh[qrows], kr[a_c:e_c])
            sc_ref[slot] = jnp.dot(qs[qrows], kch, preferred_element_type=F32)
            for rr in range(chunk_rows):
                r = c * chunk_rows + rr
                rows = pl.ds(rr * GRID_W, GRID_W)
                a, w, exact = wins[rr]
                if (n_rows - 1 - r) % 2 == 0:
                    bias = tab_ref[h, :, (n_rows - 1 - r) * GRID_W + a:(n_rows - 1 - r) * GRID_W + a + w]
                else:
                    bias = tsh_ref[h, :, (n_rows - 2 - r) * GRID_W + a:(n_rows - 2 - r) * GRID_W + a + w]
                sl = s_ref[slot, rows, a - a_c:a - a_c + w] + bias
                if not exact:
                    sl = sl + edge_mask
                sx = sc_ref[slot, rows, :]
                m = jnp.maximum(jnp.max(sl, axis=-1, keepdims=True), jnp.max(sx, axis=-1, keepdims=True))
                el = jnp.exp(sl - m)
                ex = jnp.exp(sx - m)
                den = jnp.sum(el, axis=-1, keepdims=True) + jnp.sum(ex, axis=-1, keepdims=True)
                pieces = [el.astype(BF16)]
                if a > a_c:
                    pieces.insert(0, jnp.zeros((GRID_W, a - a_c), BF16))
                if a + w < e_c:
                    pieces.append(jnp.zeros((GRID_W, e_c - a - w), BF16))
                p_ref[slot, rows, 0:w_c] = jnp.concatenate(pieces, axis=1) if len(pieces) > 1 else pieces[0]
                pc_ref[slot, rows, :] = ex.astype(BF16)
                li_ref[slot, rows, :] = jnp.broadcast_to(1.0 / den, (GRID_W, LANES))
            o = jnp.dot(p_ref[slot, :, 0:w_c], vh[a_c:e_c], preferred_element_type=F32)
            o = o + _nt_dot(pc_ref[slot], vch)
            acc = acc + o * li_ref[slot]
        o_ref[qrows, :] = acc.astype(BF16)


def _attn_dec(p_arr, qg, kg, cos, sin, kc, vc, tab, layer, n_batch, seq, row0, past, side):
    rb0 = row0 // seq
    n_pairs = 4
    side_in, side_out, side_shapes = _side_specs(side, n_batch * n_pairs, lambda b, p: b * n_pairs + p)
    chunk = 4 * GRID_W
    win = (WIN_ROWS + 4) * GRID_W
    blk = lambda c: pl.BlockSpec((seq, LANES), lambda b, p: (rb0 + b, c + p))
    gain = pl.BlockSpec((None, 1, LANES), lambda b, p: (layer, 0, 0))
    const = pl.BlockSpec((seq, LANES), lambda b, p: (0, 0))
    cache = pl.BlockSpec((None, None, 2, HEAD_DIM, past), lambda b, p: (b, layer, p, 0, 0))
    return pl.pallas_call(
        functools.partial(_attn_dec_kernel, len(side)),
        grid=(n_batch, n_pairs),
        in_specs=[
            blk(0), blk(n_pairs), blk(2 * n_pairs), gain, gain, const, const, cache, cache,
            pl.BlockSpec((None, 2, GRID_W, 2 * seq), lambda b, p: (layer, p, 0, 0)),
        ] + side_in,
        out_specs=[pl.BlockSpec((seq, LANES), lambda b, p: (b, p))] + side_out,
        out_shape=[jax.ShapeDtypeStruct((n_batch * seq, n_pairs * LANES), BF16)] + side_shapes,
        scratch_shapes=[
            pltpu.VMEM((2, chunk, win), F32), pltpu.VMEM((2, chunk, past), F32),
            pltpu.VMEM((2, chunk, win), BF16), pltpu.VMEM((2, chunk, past), BF16),
            pltpu.VMEM((2, chunk, LANES), F32), pltpu.VMEM((2, GRID_W, 2 * seq), F32),
        ],
        compiler_params=_cparams(("parallel", "parallel")),
        name="attn_dec",
    )(p_arr, p_arr, p_arr, qg, kg, cos, sin, kc, vc, tab, *[arr for arr, _ in side])


def _mix_kernel(u_ref, vg_ref, xf_ref, xp_ref, gn_ref, ws_ref, bs_ref, dft_ref, cs_ref, band_ref,
                cnt_ref, wp_ref, ps_ref, o_ref, vgn_ref, ab_ref):
    n = u_ref.shape[0]
    bw = u_ref.shape[1]
    n_groups = bw // CHUNK
    for c in range(n // CHUNK):
        rows = pl.ds(c * CHUNK, CHUNK)
        vg = jax.nn.gelu(vg_ref[rows, :].astype(F32))
        ms = jnp.mean(vg * vg, axis=-1, keepdims=True)
        vgn_ref[rows, :] = (vg * lax.rsqrt(ms + EPS) * gn_ref[...]).astype(BF16)
    for c in range(n // CHUNK):
        rows = pl.ds(c * CHUNK, CHUNK)
        parts = [jnp.dot(ws_ref[g], vgn_ref[rows, g * CHUNK:(g + 1) * CHUNK], preferred_element_type=F32)
                 for g in range(n_groups)]
        mixed = jnp.concatenate(parts, axis=1) + bs_ref[...]
        o_ref[rows, 0:bw] = (jax.nn.gelu(u_ref[rows, :].astype(F32)) * mixed).astype(BF16)
    ab_ref[...] = jnp.dot(dft_ref[...], xf_ref[...], preferred_element_type=F32).astype(BF16)
    for g in range(n_groups):
        gc = slice(g * CHUNK, (g + 1) * CHUNK)
        lhs = jnp.concatenate([ab_ref[0:n, gc], ab_ref[n:2 * n, gc]], axis=1)
        o_ref[:, bw + g * CHUNK:bw + (g + 1) * CHUNK] = jnp.dot(
            lhs, cs_ref[...], preferred_element_type=F32).astype(BF16)
    for g in range(n_groups):
        gc = slice(g * CHUNK, (g + 1) * CHUNK)
        xg = xp_ref[:, gc]
        wsum = jnp.dot(band_ref[g], xg, preferred_element_type=F32)
        pooled = wsum / cnt_ref[:, gc] - xg.astype(F32)
        y = jnp.dot(pooled.astype(BF16), wp_ref[g], preferred_element_type=F32) * ps_ref[:, gc]
        o_ref[:, 2 * bw + g * CHUNK:2 * bw + (g + 1) * CHUNK] = y.astype(BF16)


def _mix(p_arr, gn, ws, bs, dft, cs, band, cnt, wp, ps, layer, n_ctx_rows):
    m = p_arr.shape[0]
    bw = gn.shape[-1]
    n = MIX_ROWS
    n_groups = bw // CHUNK
    kind = lambda i: jnp.where(i < n_ctx_rows // n, 0, 1)
    blk = lambda c: pl.BlockSpec((n, bw), lambda i: (i, c))
    return pl.pallas_call(
        _mix_kernel,
        grid=(m // n,),
        in_specs=[
            blk(3), blk(4), blk(5), blk(6),
            pl.BlockSpec((None, 1, bw), lambda i: (layer, 0, 0)),
            pl.BlockSpec((None, n_groups, CHUNK, CHUNK), lambda i: (layer, 0, 0, 0)),
            pl.BlockSpec((None, CHUNK, bw), lambda i: (layer, 0, 0)),
            pl.BlockSpec((None, 2 * n, n), lambda i: (kind(i), 0, 0)),
            pl.BlockSpec((2 * CHUNK, CHUNK), lambda i: (0, 0)),
            pl.BlockSpec((None, n_groups, n, n), lambda i: (kind(i), 0, 0, 0)),
            pl.BlockSpec((None, n, bw), lambda i: (kind(i), 0, 0)),
            pl.BlockSpec((None, n_groups, CHUNK, CHUNK), lambda i: (layer, 0, 0, 0)),
            pl.BlockSpec((None, 1, bw), lambda i: (layer, 0, 0)),
        ],
        out_specs=pl.BlockSpec((n, 3 * bw), lambda i: (i, 0)),
        out_shape=jax.ShapeDtypeStruct((m, 3 * bw), BF16),
        scratch_shapes=[pltpu.VMEM((n, bw), BF16), pltpu.VMEM((2 * n, bw), BF16)],
        compiler_params=_cparams(("parallel",)),
        name="mix",
    )(p_arr, p_arr, p_arr, p_arr, gn, ws, bs, dft, cs, band, cnt, wp, ps)


def _merge_kernel(n_ctx_tiles, xa_ref, xb_ref, mod_ref, g2_ref, gate_ref, actx_ref, adec_ref, bcd_ref,
                  wb_ref, wo_ref, o_ref, h_ref):
    d = xa_ref.shape[1]
    bw = actx_ref.shape[1]
    is_ctx = pl.program_id(0) < n_ctx_tiles
    merged = jnp.zeros(xa_ref.shape, F32)
    for b in range(d // bw):
        if b == 0:
            br = jnp.where(is_ctx, actx_ref[...], adec_ref[...])
        else:
            br = bcd_ref[:, (b - 1) * bw:b * bw]
        y = jnp.dot(br, wb_ref[b * bw:(b + 1) * bw, :], preferred_element_type=F32)
        merged = merged + gate_ref[:, b * d:(b + 1) * d].astype(F32) * y
    mix = jnp.dot(merged.astype(BF16), wo_ref[...], preferred_element_type=F32)
    x = jnp.where(is_ctx, xa_ref[...], xb_ref[...])
    x1 = x + mod_ref[2:3, :] * mix
    o_ref[...] = x1
    h_ref[...] = _modulated_norm(x1, g2_ref[...], mod_ref[3:4, :], mod_ref[4:5, :]).astype(BF16)


def _merge(xa, xb, mod, norm2_g, gates, a_ctx, a_dec, bcd, wb, wo, layer, dec_seq):
    d = xa.shape[1]
    n_ctx_rows = xa.shape[0]
    m = n_ctx_rows + xb.shape[0]
    bw = a_ctx.shape[1]
    tm = 256
    nct = n_ctx_rows // tm
    row = functools.partial(_mod_row, tm=tm, n_ctx_rows=n_ctx_rows, dec_seq=dec_seq)
    once = pl.Buffered(1)
    ctx_tile = lambda i: (jnp.minimum(i, nct - 1), 0)
    dec_tile = lambda i: (jnp.maximum(i - nct, 0), 0)
    return pl.pallas_call(
        functools.partial(_merge_kernel, nct),
        grid=(m // tm,),
        in_specs=[
            pl.BlockSpec((tm, d), ctx_tile),
            pl.BlockSpec((tm, d), dec_tile),
            pl.BlockSpec((None, None, 6, d), lambda i: (layer, row(i), 0, 0)),
            pl.BlockSpec((None, 1, d), lambda i: (layer, 0, 0)),
            pl.BlockSpec((tm, 4 * d), lambda i: (i, 0)),
            pl.BlockSpec((tm, bw), ctx_tile),
            pl.BlockSpec((tm, bw), dec_tile),
            pl.BlockSpec((tm, 3 * bw), lambda i: (i, 0)),
            pl.BlockSpec((d, d), lambda i: (0, 0), pipeline_mode=once),
            pl.BlockSpec((d, d), lambda i: (0, 0), pipeline_mode=once),
        ],
        out_specs=[pl.BlockSpec((tm, d), lambda i: (i, 0))] * 2,
        out_shape=[jax.ShapeDtypeStruct((m, d), F32), jax.ShapeDtypeStruct((m, d), BF16)],
        compiler_params=_cparams(("parallel",)),
        name="merge",
    )(xa, xb, mod, norm2_g, gates, a_ctx, a_dec, bcd, wb, wo)


def _mlp_kernel(n_ctx_tiles, n_up, n_side, h_ref, x_ref, mod_ref, w1_ref, w2_ref, *refs):
    side_in = refs[:n_side]
    oa_ref, ob_ref = refs[n_side:n_side + 2]
    side_out = refs[n_side + 2:2 * n_side + 2]
    a_ref = refs[2 * n_side + 2]
    s = pl.program_id(1)
    is_ctx = pl.program_id(0) < n_ctx_tiles
    tf = w1_ref.shape[1]

    @pl.when(s < n_up)
    def _():
        _cast_blocks(side_in, side_out)
        a = jnp.dot(h_ref[...], w1_ref[...], preferred_element_type=F32)
        a_ref[s] = jnp.square(jnp.maximum(a, 0.0)).astype(BF16)

    def down():
        _cast_blocks(side_in, side_out)
        acc = jnp.dot(a_ref[0], w2_ref[0:tf, :], preferred_element_type=F32)
        for c in range(1, n_up):
            acc = acc + jnp.dot(a_ref[c], w2_ref[c * tf:(c + 1) * tf, :], preferred_element_type=F32)
        return x_ref[...] + mod_ref[5:6, :] * acc

    @pl.when((s >= n_up) & is_ctx)
    def _():
        oa_ref[...] = down()

    @pl.when((s >= n_up) & jnp.logical_not(is_ctx))
    def _():
        ob_ref[...] = down()


def _mlp(h, x, mod, w1, w2, layer, n_ctx_rows, dec_seq, side):
    m, d = x.shape
    dff = w1.shape[1]
    tm, tf, tn = 1024, 1024, 256
    n_up, n_down = dff // tf, d // tn
    nct = n_ctx_rows // tm
    row = functools.partial(_mod_row, tm=tm, n_ctx_rows=n_ctx_rows, dec_seq=dec_seq)
    col = lambda s: jnp.maximum(s - n_up, 0)
    ctx_out = lambda i, s: (jnp.minimum(i, nct - 1), jnp.where(i < nct, col(s), n_down - 1))
    dec_out = lambda i, s: (jnp.maximum(i - nct, 0), jnp.where(i < nct, 0, col(s)))
    n_inner = n_up + n_down
    side_blocks = 128
    assert side_blocks <= (m // tm) * n_inner
    side_in, side_out, side_shapes = _side_specs(side, side_blocks, lambda i, s: i * n_inner + s)
    return pl.pallas_call(
        functools.partial(_mlp_kernel, nct, n_up, len(side)),
        grid=(m // tm, n_inner),
        in_specs=[
            pl.BlockSpec((tm, d), lambda i, s: (i, 0)),
            pl.BlockSpec((tm, tn), lambda i, s: (i, col(s))),
            pl.BlockSpec((None, None, 6, tn), lambda i, s: (layer, row(i), 0, col(s))),
            pl.BlockSpec((d, tf), lambda i, s: (0, jnp.minimum(s, n_up - 1))),
            pl.BlockSpec((dff, tn), lambda i, s: (0, col(s))),
        ] + side_in,
        out_specs=[pl.BlockSpec((tm, tn), ctx_out), pl.BlockSpec((tm, tn), dec_out)] + side_out,
        out_shape=[jax.ShapeDtypeStruct((n_ctx_rows, d), F32),
                   jax.ShapeDtypeStruct((m - n_ctx_rows, d), F32)] + side_shapes,
        scratch_shapes=[pltpu.VMEM((n_up, tm, tf), BF16)],
        compiler_params=_cparams(("arbitrary", "arbitrary")),
        name="mlp",
    )(h, x, mod, w1, w2, *[arr for arr, _ in side])


def _dft_tables(n_ctx_seq, n):
    def cs(size):
        k = np.arange(size)
        ang = 2.0 * np.pi * ((k[:, None] * k[None, :]) % size) / size
        return np.cos(ang) / np.sqrt(size), np.sin(ang) / np.sqrt(size)

    c_s, s_s = cs(n_ctx_seq)
    eye = np.eye(n // n_ctx_seq)
    c_l, s_l = cs(n)
    pos = np.stack([np.concatenate([np.kron(eye, c_s), np.kron(eye, s_s)], axis=0),
                    np.concatenate([c_l, s_l], axis=0)])
    c_c, s_c = cs(CHUNK)
    chan = np.concatenate([c_c, -s_c], axis=0)
    return jnp.asarray(pos, F32).astype(BF16), jnp.asarray(chan, F32).astype(BF16)


def _pool_tables(n_ctx_seq, n):
    def one(size):
        t = np.arange(size)
        bands, counts = [], []
        for w in POOL_WINDOWS:
            lo = np.clip(t - w // 2, 0, size - 1)
            hi = np.clip(t + w // 2 - 1, 0, size - 1)
            s = np.arange(size)[None, :]
            bands.append(((s >= lo[:, None]) & (s <= hi[:, None])).astype(np.float32))
            counts.append((hi - lo + 1).astype(np.float32))
        return np.stack(bands), np.stack(counts)

    b_s, c_s = one(n_ctx_seq)
    rep = n // n_ctx_seq
    b_l, c_l = one(n)
    band = np.stack([np.stack([np.kron(np.eye(rep), b) for b in b_s]), b_l])
    cnt = np.stack([np.tile(c_s, (1, rep)), c_l])
    cnt = np.repeat(cnt.transpose(0, 2, 1), CHUNK, axis=2)
    return jnp.asarray(band, BF16), jnp.asarray(cnt, F32)


def _rope_tables(n):
    t = jnp.arange(n)
    pos = jnp.stack([t // GRID_W, t % GRID_W], axis=-1).astype(F32)
    half = HEAD_DIM // 2
    inv = 1.0 / (ROPE_BASE ** (jnp.arange(0, half, 2, dtype=F32) / half))
    ang = pos[:, :, None] * inv
    cos, sin = jnp.cos(ang), jnp.sin(ang)
    cos_h = jnp.concatenate([cos[:, 0], cos[:, 0], cos[:, 1], cos[:, 1]], axis=-1)
    sin_h = jnp.concatenate([-sin[:, 0], sin[:, 0], -sin[:, 1], sin[:, 1]], axis=-1)
    return jnp.tile(cos_h, (1, 2)), jnp.tile(sin_h, (1, 2))


def _bias_table(rpb, n):
    n_layers, n_heads, n_d, n_e = rpb.shape
    cq = np.arange(GRID_W)
    cs = np.clip(cq - WIN_COLS // 2, 0, GRID_W - WIN_COLS)
    kc = np.arange(GRID_W)
    col_ok = (kc[None, :] >= cs[:, None]) & (kc[None, :] < cs[:, None] + WIN_COLS)
    col_mask = np.where(col_ok, 0.0, NEG_INF).astype(np.float32)
    dc_idx = np.clip(kc[None, :] - cq[:, None], -(WIN_COLS - 1), WIN_COLS - 1) + WIN_COLS - 1
    pick = (dc_idx[None, :, :] == np.arange(n_e)[:, None, None]).astype(np.float32)
    t = jnp.einsum("lhde,eck->lhcdk", rpb, jnp.asarray(pick), precision=lax.Precision.HIGHEST)
    t = t + jnp.asarray(col_mask)[:, None, :]
    flat = t.reshape(n_layers, n_heads, GRID_W, n_d * GRID_W)
    front = WIN_ROWS * GRID_W
    back = 2 * n - front - n_d * GRID_W
    return jnp.pad(flat, ((0, 0), (0, 0), (0, 0), (front, back)))


def kernel(x_prompt, x_sample, cache_k, cache_v, c, c_ctx, norm1_g, w_in, b_gate, q_norm_g, k_norm_g,
           rpb, gmlp_norm_g, w_spatial, b_spatial, w_pool, pool_scale, w_branch, w_out, norm2_g,
           w_mlp1, w_mlp2, w_ada, b_ada):
    n_b, seq, d = x_prompt.shape
    n_db, dec_seq, _ = x_sample.shape
    n_layers = w_in.shape[0]
    n_heads, past = cache_k.shape[2], cache_k.shape[3]
    bw = n_heads * HEAD_DIM
    n_ctx_rows = n_b * seq
    assert dec_seq == MIX_ROWS and MIX_ROWS % seq == 0 and n_ctx_rows % MIX_ROWS == 0
    assert dec_seq // GRID_W == 2 * WIN_ROWS and bw == 4 * LANES

    xa = x_prompt.reshape(n_ctx_rows, d)
    xb = x_sample.reshape(n_db * dec_seq, d)
    pad_rows = (-(1 + n_db)) % 8
    cond = jnp.concatenate([c_ctx[None, :], c, jnp.zeros((pad_rows, d), F32)], axis=0)
    mod = _ada(cond, w_ada, b_ada).reshape(n_layers, cond.shape[0], 6, d)

    w_in_l = w_in[0].astype(BF16)
    kct = cache_k.transpose(0, 1, 2, 4, 3)
    vct = cache_v.transpose(0, 1, 2, 4, 3)
    ws_b, wp_b = w_spatial.astype(BF16), w_pool.astype(BF16)
    bs_x = jnp.repeat(b_spatial.transpose(0, 2, 1), CHUNK, axis=2)
    qg = jnp.tile(q_norm_g, (1, 2))[:, None, :]
    kg = jnp.tile(k_norm_g, (1, 2))[:, None, :]
    dft, chan = _dft_tables(seq, MIX_ROWS)
    band, cnt = _pool_tables(seq, MIX_ROWS)
    cos, sin = _rope_tables(dec_seq)
    tab = _bias_table(rpb, dec_seq)
    n1 = norm1_g[:, None, :]
    n2 = norm2_g[:, None, :]
    gn = gmlp_norm_g[:, None, :]
    ps = pool_scale[:, None, :]
    bg = b_gate[:, None, :]

    new_kv = None
    for l in range(n_layers):
        p_arr, gates = _inproj(xa, xb, mod, n1, w_in_l, bg, l, 7 * bw, dec_seq, side=[])
        a_ctx, *new_kv = _attn_ctx(p_arr, qg, kg, l, n_layers, n_b, seq, new_kv)
        a_dec, w1_l, w2_l, wb_l, wo_l = _attn_dec(
            p_arr, qg, kg, cos, sin, kct, vct, tab, l, n_db, dec_seq, n_ctx_rows, past,
            side=[(w_mlp1, l), (w_mlp2, l), (w_branch, l), (w_out, l)])
        bcd = _mix(p_arr, gn, ws_b, bs_x, dft, chan, band, cnt, wp_b, ps, l, n_ctx_rows)
        x1, h2 = _merge(xa, xb, mod, n2, gates, a_ctx, a_dec, bcd, wb_l, wo_l, l, dec_seq)
        ahead = [(w_in, l + 1)] if l + 1 < n_layers else []
        xa, xb, *cast = _mlp(h2, x1, mod, w1_l, w2_l, l, n_ctx_rows, dec_seq, side=ahead)
        if cast:
            w_in_l = cast[0]

    return (xa.reshape(n_b, seq, d), xb.reshape(n_db, dec_seq, d),
            new_kv[0].swapaxes(3, 4), new_kv[1].swapaxes(3, 4))
```

```python
import functools

import numpy as np
import jax
import jax.numpy as jnp
from jax import lax
from jax.experimental import pallas as pl
from jax.experimental.pallas import tpu as pltpu

F32 = jnp.float32
BF16 = jnp.bfloat16

GRID_W = 64
HEAD_DIM = 64
WIN_ROWS = 8
WIN_COLS = 16
ROPE_BASE = 10000.0
CHUNK = 128
POOL_WINDOWS = (2, 4, 8, 16)
EPS = 1e-6
NEG_INF = -1e30

LANES = 128
MIX_ROWS = 1024
VMEM_LIMIT = 60 * 1024 * 1024


def _cparams(sem):
    return pltpu.CompilerParams(dimension_semantics=sem, vmem_limit_bytes=VMEM_LIMIT)


def _mod_row(i, tm, n_ctx_rows, dec_seq):
    nc = n_ctx_rows // tm
    per = dec_seq // tm
    return jnp.where(i < nc, 0, 1 + (i - nc) // per)


def _ada_kernel(c_ref, w_ref, b_ref, o_ref):
    s = jax.nn.silu(c_ref[...]).astype(BF16)
    o_ref[...] = jnp.dot(s, w_ref[...].astype(BF16), preferred_element_type=F32) + b_ref[...]


def _ada(cond, w_ada, b_ada):
    n_layers, d, n = w_ada.shape
    rows = cond.shape[0]
    tn = 1024
    return pl.pallas_call(
        _ada_kernel,
        grid=(n_layers, n // tn),
        in_specs=[
            pl.BlockSpec((rows, d), lambda l, j: (0, 0)),
            pl.BlockSpec((None, d, tn), lambda l, j: (l, 0, j)),
            pl.BlockSpec((None, 1, tn), lambda l, j: (l, 0, j)),
        ],
        out_specs=pl.BlockSpec((None, rows, tn), lambda l, j: (l, 0, j)),
        out_shape=jax.ShapeDtypeStruct((n_layers, rows, n), F32),
        compiler_params=_cparams(("parallel", "parallel")),
        name="ada",
    )(cond, w_ada, b_ada.reshape(n_layers, 1, n))


def _modulated_norm(x, g, shift, scale):
    ms = jnp.mean(x * x, axis=-1, keepdims=True)
    return (x * lax.rsqrt(ms + EPS) * g) * (1.0 + scale) + shift


def _cast_blocks(src_refs, dst_refs):
    for src, dst in zip(src_refs, dst_refs):
        dst[...] = src[...].astype(BF16)


def _side_specs(side, n_blocks, step_of):
    in_specs, out_specs, out_shapes = [], [], []
    for arr, layer in side:
        _, r, c = arr.shape
        assert r % (16 * n_blocks) == 0
        rb = r // n_blocks
        blk = lambda *ids: jnp.minimum(step_of(*ids), n_blocks - 1)
        in_specs.append(pl.BlockSpec((None, rb, c), lambda *ids, layer=layer: (layer, blk(*ids), 0)))
        out_specs.append(pl.BlockSpec((rb, c), lambda *ids: (blk(*ids), 0)))
        out_shapes.append(jax.ShapeDtypeStruct((r, c), BF16))
    return in_specs, out_specs, out_shapes


def _x_tile_copy(xa_hbm, xb_hbm, use_a, tile, nct, tm, xbuf, sem):
    if use_a:
        src = xa_hbm.at[pl.ds(tile * tm, tm), :]
    else:
        src = xb_hbm.at[pl.ds((tile - nct) * tm, tm), :]
    return pltpu.make_async_copy(src, xbuf, sem)


def _inproj_kernel(nct, n_a, n_g, n_parts, n_side, xa_hbm, xb_hbm, mod_ref, modn_ref, g_ref, wa_ref, *refs):
    wg_refs, bg_ref, refs = refs[:n_parts], refs[n_parts], refs[n_parts + 1:]
    side_in = refs[:n_side]
    os_ref, og_ref = refs[n_side:n_side + 2]
    side_out = refs[n_side + 2:2 * n_side + 2]
    xbuf, h0_ref, h1_ref, sem = refs[2 * n_side + 2:]
    i = pl.program_id(0)
    j = pl.program_id(1)
    n_tiles = pl.num_programs(0)
    tm = xbuf.shape[0]
    rc = tm // n_g
    slot = i % 2
    copy = functools.partial(_x_tile_copy, xa_hbm, xb_hbm, nct=nct, tm=tm, xbuf=xbuf, sem=sem)

    def start(tile):
        @pl.when(tile < nct)
        def _():
            copy(True, tile).start()

        @pl.when(tile >= nct)
        def _():
            copy(False, tile).start()

    @pl.when((i == 0) & (j == 0))
    def _():
        start(i)
        copy(True, 0).wait()
        h = _modulated_norm(xbuf[...], g_ref[...], mod_ref[0:1, :], mod_ref[1:2, :])
        h0_ref[...] = h.astype(BF16)

    @pl.when((j == 0) & (i + 1 < n_tiles))
    def _():
        start(i + 1)

    @pl.when((j == n_a) & (i + 1 < n_tiles))
    def _():
        copy(True, 0).wait()

    for cur, nxt in ((h0_ref, h1_ref), (h1_ref, h0_ref)):
        mine = slot == (0 if cur is h0_ref else 1)

        @pl.when((j < n_a) & mine)
        def _(cur=cur):
            _cast_blocks(side_in, side_out)
            os_ref[...] = jnp.dot(cur[...], wa_ref[...], preferred_element_type=F32).astype(BF16)

        @pl.when((j >= n_a) & mine)
        def _(cur=cur, nxt=nxt):
            _cast_blocks(side_in, side_out)
            rows = pl.ds(pl.multiple_of((j - n_a) * rc, rc), rc)
            hn = _modulated_norm(xbuf[rows, :], g_ref[...], modn_ref[0:1, :], modn_ref[1:2, :])
            nxt[rows, :] = hn.astype(BF16)
            wpart = wg_refs[0].shape[1]
            for part, w_ref in enumerate(wg_refs):
                cols = slice(part * wpart, (part + 1) * wpart)
                logits = jnp.dot(cur[...], w_ref[...], preferred_element_type=F32) + bg_ref[:, cols]
                og_ref[:, cols] = logits.astype(BF16)


def _inproj(xa, xb, mod, norm_g, w, b_gate, layer, n_mix, dec_seq, side):
    d = xa.shape[1]
    n_ctx_rows = xa.shape[0]
    m = n_ctx_rows + xb.shape[0]
    n_gate = w.shape[1] - n_mix
    tm, ta, tg, wpart = 1024, 512, 2048, 512
    n_parts = tg // wpart
    assert n_mix % ta == 0 and n_mix % wpart == 0 and n_gate % tg == 0
    n_a, n_g = n_mix // ta, n_gate // tg
    n_tiles = m // tm
    row = functools.partial(_mod_row, tm=tm, n_ctx_rows=n_ctx_rows, dec_seq=dec_seq)
    a_col = lambda j: jnp.minimum(j, n_a - 1)
    g_col = lambda j: jnp.maximum(j - n_a, 0)
    n_inner = n_a + n_g
    side_blocks = 128
    assert side_blocks <= n_tiles * n_inner
    side_in, side_out, side_shapes = _side_specs(side, side_blocks, lambda i, j: i * n_inner + j)
    return pl.pallas_call(
        functools.partial(_inproj_kernel, n_ctx_rows // tm, n_a, n_g, n_parts, len(side)),
        grid=(n_tiles, n_inner),
        in_specs=[
            pl.BlockSpec(memory_space=pl.ANY),
            pl.BlockSpec(memory_space=pl.ANY),
            pl.BlockSpec((None, None, 6, d), lambda i, j: (layer, row(i), 0, 0)),
            pl.BlockSpec((None, None, 6, d), lambda i, j: (layer, row(jnp.minimum(i + 1, n_tiles - 1)), 0, 0)),
            pl.BlockSpec((None, 1, d), lambda i, j: (layer, 0, 0)),
            pl.BlockSpec((d, ta), lambda i, j: (0, a_col(j))),
        ] + [
            pl.BlockSpec((d, wpart), lambda i, j, part=part: (0, n_mix // wpart + n_parts * g_col(j) + part))
            for part in range(n_parts)
        ] + [
            pl.BlockSpec((None, 1, tg), lambda i, j: (layer, 0, g_col(j))),
        ] + side_in,
        out_specs=[
            pl.BlockSpec((tm, ta), lambda i, j: (i, a_col(j))),
            pl.BlockSpec((tm, tg), lambda i, j: (i, g_col(j))),
        ] + side_out,
        out_shape=[jax.ShapeDtypeStruct((m, n_mix), BF16), jax.ShapeDtypeStruct((m, n_gate), BF16)] + side_shapes,
        scratch_shapes=[pltpu.VMEM((tm, d), F32), pltpu.VMEM((tm, d), BF16), pltpu.VMEM((tm, d), BF16),
                        pltpu.SemaphoreType.DMA(())],
        compiler_params=_cparams(("arbitrary", "arbitrary")),
        name="inproj",
    )(xa, xb, mod, mod, norm_g, w, *([w] * n_parts), b_gate, *[arr for arr, _ in side])


def _head_norm(x, g, lane):
    row = lax.broadcasted_iota(jnp.int32, (LANES, LANES), 0)
    same_head = (row >= HEAD_DIM) == (lane >= HEAD_DIM)
    ones = jnp.where(same_head, 1.0, 0.0).astype(BF16)
    x2 = x * x
    hi = x2.astype(BF16)
    lo = (x2 - hi.astype(F32)).astype(BF16)
    ss = (jnp.dot(hi, ones, preferred_element_type=F32) + jnp.dot(lo, ones, preferred_element_type=F32))
    return x * lax.rsqrt(ss * (1.0 / HEAD_DIM) + EPS) * g


def _head_norm_xlu(x, g, lane):
    low = lane < HEAD_DIM
    x2 = x * x
    s_lo = jnp.sum(jnp.where(low, x2, 0.0), axis=-1, keepdims=True)
    s_hi = jnp.sum(jnp.where(low, 0.0, x2), axis=-1, keepdims=True)
    ss = jnp.where(low, s_lo, s_hi)
    return x * lax.rsqrt(ss * (1.0 / HEAD_DIM) + EPS) * g


def _rope(x, cos, sin_signed, lane):
    first = (lane & 16) == 0
    partner = jnp.where(first, pltpu.roll(x, LANES - 16, 1), pltpu.roll(x, 16, 1))
    return x * cos + partner * sin_signed


def _nt_dot(a, b):
    return lax.dot_general(a, b, (((1,), (1,)), ((), ())), preferred_element_type=F32)


def _head_mask(lane, h, value):
    return jnp.where((lane >= HEAD_DIM) == (h == 1), value, 0.0).astype(F32)


def _attn_ctx_kernel(first, q_ref, k_ref, v_ref, qg_ref, kg_ref, *rest):
    a_ref, kn_ref, vn_ref = rest[-3:]
    if first:
        for later in range(1, kn_ref.shape[0]):
            kn_ref[later] = jnp.zeros(kn_ref.shape[1:], F32)
            vn_ref[later] = jnp.zeros(vn_ref.shape[1:], F32)
        kn_ref, vn_ref = kn_ref.at[0], vn_ref.at[0]
    lane = lax.broadcasted_iota(jnp.int32, (1, LANES), 1)
    scale = HEAD_DIM ** -0.5
    for p in range(q_ref.shape[1] // LANES):
        cols = slice(p * LANES, (p + 1) * LANES)
        q = _head_norm_xlu(q_ref[:, cols].astype(F32), qg_ref[...], lane)
        k = _head_norm_xlu(k_ref[:, cols].astype(F32), kg_ref[...], lane)
        v = v_ref[:, cols].astype(F32)
        kt, vt = k.T, v.T
        for h in range(2):
            kn_ref[2 * p + h] = kt[h * HEAD_DIM:(h + 1) * HEAD_DIM, :]
            vn_ref[2 * p + h] = vt[h * HEAD_DIM:(h + 1) * HEAD_DIM, :]
        kb = k.astype(BF16)
        acc = jnp.zeros(q.shape, F32)
        for h in range(2):
            s = _nt_dot((q * _head_mask(lane, h, scale)).astype(BF16), kb)
            e = jnp.exp(s - jnp.max(s, axis=-1, keepdims=True))
            o = jnp.dot(e.astype(BF16), (v * _head_mask(lane, h, 1.0)).astype(BF16),
                        preferred_element_type=F32)
            acc = acc + o / jnp.sum(e, axis=-1, keepdims=True)
        a_ref[:, cols] = acc.astype(BF16)


def _attn_ctx(p_arr, qg, kg, layer, n_layers, n_batch, seq, new_kv):
    w = 4 * LANES
    rows = n_batch * seq
    n_heads = w // HEAD_DIM
    blk = lambda c: pl.BlockSpec((seq, w), lambda b: (b, c))
    gain = pl.BlockSpec((None, 1, LANES), lambda b: (layer, 0, 0))
    first = new_kv is None
    assert first == (layer == 0)
    if first:
        cache = pl.BlockSpec((None, n_layers, n_heads, HEAD_DIM, seq), lambda b: (b, 0, 0, 0, 0))
    else:
        cache = pl.BlockSpec((None, None, n_heads, HEAD_DIM, seq), lambda b: (b, layer, 0, 0, 0))
    cache_shape = jax.ShapeDtypeStruct((n_batch, n_layers, n_heads, HEAD_DIM, seq), F32)
    carried = [] if first else list(new_kv)
    return pl.pallas_call(
        functools.partial(_attn_ctx_kernel, first),
        grid=(n_batch,),
        in_specs=[blk(0), blk(1), blk(2), gain, gain] + [pl.BlockSpec(memory_space=pl.ANY)] * len(carried),
        out_specs=[pl.BlockSpec((seq, w), lambda b: (b, 0)), cache, cache],
        out_shape=[jax.ShapeDtypeStruct((rows, w), BF16), cache_shape, cache_shape],
        input_output_aliases={5 + n: 1 + n for n in range(len(carried))},
        compiler_params=_cparams(("parallel",)),
        name="attn_ctx",
    )(p_arr, p_arr, p_arr, qg, kg, *carried)


def _row_window(r, n_rows):
    rs = min(max(r - WIN_ROWS // 2, 0), n_rows - WIN_ROWS)
    if rs % 2 == 0:
        return rs * GRID_W, WIN_ROWS * GRID_W, True
    return (rs - 1) * GRID_W, (WIN_ROWS + 2) * GRID_W, False


def _attn_dec_kernel(n_side, q_ref, k_ref, v_ref, qg_ref, kg_ref, cos_ref, sin_ref, kc_ref, vc_ref,
                     tab_ref, *refs):
    side_in = refs[:n_side]
    o_ref = refs[n_side]
    side_out = refs[n_side + 1:2 * n_side + 1]
    s_ref, sc_ref, p_ref, pc_ref, li_ref, tsh_ref = refs[2 * n_side + 1:]
    _cast_blocks(side_in, side_out)
    n = q_ref.shape[0]
    n_rows = n // GRID_W
    lane = lax.broadcasted_iota(jnp.int32, (1, LANES), 1)
    scale = HEAD_DIM ** -0.5
    q = _head_norm(q_ref[...].astype(F32), qg_ref[...], lane)
    k = _head_norm(k_ref[...].astype(F32), kg_ref[...], lane)
    qr = _rope(q, cos_ref[...], sin_ref[...], lane)
    kr = _rope(k, cos_ref[...], sin_ref[...], lane).astype(BF16)
    v = v_ref[...].astype(F32)
    qs = (q * scale).astype(BF16)

    def on_head_rows(xt, h):
        xb = xt.astype(BF16)
        zero = jnp.zeros_like(xb)
        return jnp.concatenate([xb, zero] if h == 0 else [zero, xb], axis=0)

    wide = (WIN_ROWS + 2) * GRID_W
    wl = lax.broadcasted_iota(jnp.int32, (1, wide), 1)
    edge_mask = jnp.where((wl < GRID_W) | (wl >= wide - GRID_W), NEG_INF, 0.0).astype(F32)
    heads = []
    for h in range(2):
        tsh_ref[h] = pltpu.roll(tab_ref[h], 2 * n - GRID_W, 1)
        heads.append(((qr * _head_mask(lane, h, scale)).astype(BF16),
                      (v * _head_mask(lane, h, 1.0)).astype(BF16),
                      on_head_rows(kc_ref[h], h), on_head_rows(vc_ref[h], h)))
    chunk_rows = s_ref.shape[1] // GRID_W
    for c in range(n_rows // chunk_rows):
        wins = [_row_window(r, n_rows) for r in range(c * chunk_rows, (c + 1) * chunk_rows)]
        a_c = min(a for a, _, _ in wins)
        e_c = max(a + w for a, w, _ in wins)
        w_c = e_c - a_c
        qrows = slice(c * chunk_rows * GRID_W, (c + 1) * chunk_rows * GRID_W)
        acc = jnp.zeros((chunk_rows * GRID_W, LANES), F32)
        for h in range(2):
            qh, vh, kch, vch = heads[h]
            slot = h
            s_ref[slot, :, 0:w_c] = _nt_dot(qh[qrows], kr[a_c:e_c])
            sc_ref[slot] = jnp.dot(qs[qrows], kch, preferred_element_type=F32)
            for rr in range(chunk_rows):
                r = c * chunk_rows + rr
                rows = pl.ds(rr * GRID_W, GRID_W)
                a, w, exact = wins[rr]
                if (n_rows - 1 - r) % 2 == 0:
                    bias = tab_ref[h, :, (n_rows - 1 - r) * GRID_W + a:(n_rows - 1 - r) * GRID_W + a + w]
                else:
                    bias = tsh_ref[h, :, (n_rows - 2 - r) * GRID_W + a:(n_rows - 2 - r) * GRID_W + a + w]
                sl = s_ref[slot, rows, a - a_c:a - a_c + w] + bias
                if not exact:
                    sl = sl + edge_mask
                sx = sc_ref[slot, rows, :]
                m = jnp.maximum(jnp.max(sl, axis=-1, keepdims=True), jnp.max(sx, axis=-1, keepdims=True))
                el = jnp.exp(sl - m)
                ex = jnp.exp(sx - m)
                den = jnp.sum(el, axis=-1, keepdims=True) + jnp.sum(ex, axis=-1, keepdims=True)
                pieces = [el.astype(BF16)]
                if a > a_c:
                    pieces.insert(0, jnp.zeros((GRID_W, a - a_c), BF16))
                if a + w < e_c:
                    pieces.append(jnp.zeros((GRID_W, e_c - a - w), BF16))
                p_ref[slot, rows, 0:w_c] = jnp.concatenate(pieces, axis=1) if len(pieces) > 1 else pieces[0]
                pc_ref[slot, rows, :] = ex.astype(BF16)
                li_ref[slot, rows, :] = jnp.broadcast_to(1.0 / den, (GRID_W, LANES))
            o = jnp.dot(p_ref[slot, :, 0:w_c], vh[a_c:e_c], preferred_element_type=F32)
            o = o + _nt_dot(pc_ref[slot], vch)
            acc = acc + o * li_ref[slot]
        o_ref[qrows, :] = acc.astype(BF16)


def _attn_dec(p_arr, qg, kg, cos, sin, kc, vc, tab, layer, n_batch, seq, row0, past, side):
    rb0 = row0 // seq
    n_pairs = 4
    side_in, side_out, side_shapes = _side_specs(side, n_batch * n_pairs, lambda b, p: b * n_pairs + p)
    chunk = 4 * GRID_W
    win = (WIN_ROWS + 4) * GRID_W
    blk = lambda c: pl.BlockSpec((seq, LANES), lambda b, p: (rb0 + b, c + p))
    gain = pl.BlockSpec((None, 1, LANES), lambda b, p: (layer, 0, 0))
    const = pl.BlockSpec((seq, LANES), lambda b, p: (0, 0))
    cache = pl.BlockSpec((None, None, 2, HEAD_DIM, past), lambda b, p: (b, layer, p, 0, 0))
    return pl.pallas_call(
        functools.partial(_attn_dec_kernel, len(side)),
        grid=(n_batch, n_pairs),
        in_specs=[
            blk(0), blk(n_pairs), blk(2 * n_pairs), gain, gain, const, const, cache, cache,
            pl.BlockSpec((None, 2, GRID_W, 2 * seq), lambda b, p: (layer, p, 0, 0)),
        ] + side_in,
        out_specs=[pl.BlockSpec((seq, LANES), lambda b, p: (b, p))] + side_out,
        out_shape=[jax.ShapeDtypeStruct((n_batch * seq, n_pairs * LANES), BF16)] + side_shapes,
        scratch_shapes=[
            pltpu.VMEM((2, chunk, win), F32), pltpu.VMEM((2, chunk, past), F32),
            pltpu.VMEM((2, chunk, win), BF16), pltpu.VMEM((2, chunk, past), BF16),
            pltpu.VMEM((2, chunk, LANES), F32), pltpu.VMEM((2, GRID_W, 2 * seq), F32),
        ],
        compiler_params=_cparams(("parallel", "parallel")),
        name="attn_dec",
    )(p_arr, p_arr, p_arr, qg, kg, cos, sin, kc, vc, tab, *[arr for arr, _ in side])


def _mix_kernel(u_ref, vg_ref, xf_ref, xp_ref, gn_ref, ws_ref, bs_ref, dft_ref, cs_ref, band_ref,
                cnt_ref, wp_ref, ps_ref, o_ref, vgn_ref, ab_ref):
    n = u_ref.shape[0]
    bw = u_ref.shape[1]
    n_groups = bw // CHUNK
    for c in range(n // CHUNK):
        rows = pl.ds(c * CHUNK, CHUNK)
        vg = jax.nn.gelu(vg_ref[rows, :].astype(F32))
        ms = jnp.mean(vg * vg, axis=-1, keepdims=True)
        vgn_ref[rows, :] = (vg * lax.rsqrt(ms + EPS) * gn_ref[...]).astype(BF16)
    for c in range(n // CHUNK):
        rows = pl.ds(c * CHUNK, CHUNK)
        parts = [jnp.dot(ws_ref[g], vgn_ref[rows, g * CHUNK:(g + 1) * CHUNK], preferred_element_type=F32)
                 for g in range(n_groups)]
        mixed = jnp.concatenate(parts, axis=1) + bs_ref[...]
        o_ref[rows, 0:bw] = (jax.nn.gelu(u_ref[rows, :].astype(F32)) * mixed).astype(BF16)
    ab_ref[...] = jnp.dot(dft_ref[...], xf_ref[...], preferred_element_type=F32).astype(BF16)
    for g in range(n_groups):
        gc = slice(g * CHUNK, (g + 1) * CHUNK)
        lhs = jnp.concatenate([ab_ref[0:n, gc], ab_ref[n:2 * n, gc]], axis=1)
        o_ref[:, bw + g * CHUNK:bw + (g + 1) * CHUNK] = jnp.dot(
            lhs, cs_ref[...], preferred_element_type=F32).astype(BF16)
    for g in range(n_groups):
        gc = slice(g * CHUNK, (g + 1) * CHUNK)
        xg = xp_ref[:, gc]
        wsum = jnp.dot(band_ref[g], xg, preferred_element_type=F32)
        pooled = wsum / cnt_ref[:, gc] - xg.astype(F32)
        y = jnp.dot(pooled.astype(BF16), wp_ref[g], preferred_element_type=F32) * ps_ref[:, gc]
        o_ref[:, 2 * bw + g * CHUNK:2 * bw + (g + 1) * CHUNK] = y.astype(BF16)


def _mix(p_arr, gn, ws, bs, dft, cs, band, cnt, wp, ps, layer, n_ctx_rows):
    m = p_arr.shape[0]
    bw = gn.shape[-1]
    n = MIX_ROWS
    n_groups = bw // CHUNK
    kind = lambda i: jnp.where(i < n_ctx_rows // n, 0, 1)
    blk = lambda c: pl.BlockSpec((n, bw), lambda i: (i, c))
    return pl.pallas_call(
        _mix_kernel,
        grid=(m // n,),
        in_specs=[
            blk(3), blk(4), blk(5), blk(6),
            pl.BlockSpec((None, 1, bw), lambda i: (layer, 0, 0)),
            pl.BlockSpec((None, n_groups, CHUNK, CHUNK), lambda i: (layer, 0, 0, 0)),
            pl.BlockSpec((None, CHUNK, bw), lambda i: (layer, 0, 0)),
            pl.BlockSpec((None, 2 * n, n), lambda i: (kind(i), 0, 0)),
            pl.BlockSpec((2 * CHUNK, CHUNK), lambda i: (0, 0)),
            pl.BlockSpec((None, n_groups, n, n), lambda i: (kind(i), 0, 0, 0)),
            pl.BlockSpec((None, n, bw), lambda i: (kind(i), 0, 0)),
            pl.BlockSpec((None, n_groups, CHUNK, CHUNK), lambda i: (layer, 0, 0, 0)),
            pl.BlockSpec((None, 1, bw), lambda i: (layer, 0, 0)),
        ],
        out_specs=pl.BlockSpec((n, 3 * bw), lambda i: (i, 0)),
        out_shape=jax.ShapeDtypeStruct((m, 3 * bw), BF16),
        scratch_shapes=[pltpu.VMEM((n, bw), BF16), pltpu.VMEM((2 * n, bw), BF16)],
        compiler_params=_cparams(("parallel",)),
        name="mix",
    )(p_arr, p_arr, p_arr, p_arr, gn, ws, bs, dft, cs, band, cnt, wp, ps)


def _merge_kernel(n_ctx_tiles, xa_ref, xb_ref, mod_ref, g2_ref, gate_ref, actx_ref, adec_ref, bcd_ref,
                  wb_ref, wo_ref, o_ref, h_ref):
    d = xa_ref.shape[1]
    bw = actx_ref.shape[1]
    is_ctx = pl.program_id(0) < n_ctx_tiles
    merged = jnp.zeros(xa_ref.shape, F32)
    for b in range(d // bw):
        if b == 0:
            br = jnp.where(is_ctx, actx_ref[...], adec_ref[...])
        else:
            br = bcd_ref[:, (b - 1) * bw:b * bw]
        y = jnp.dot(br, wb_ref[b * bw:(b + 1) * bw, :], preferred_element_type=F32)
        gate = 0.5 * jnp.tanh(0.5 * gate_ref[:, b * d:(b + 1) * d].astype(F32)) + 0.5
        merged = merged + gate * y
    mix = jnp.dot(merged.astype(BF16), wo_ref[...], preferred_element_type=F32)
    x = jnp.where(is_ctx, xa_ref[...], xb_ref[...])
    x1 = x + mod_ref[2:3, :] * mix
    o_ref[...] = x1
    h_ref[...] = _modulated_norm(x1, g2_ref[...], mod_ref[3:4, :], mod_ref[4:5, :]).astype(BF16)


def _merge(xa, xb, mod, norm2_g, gates, a_ctx, a_dec, bcd, wb, wo, layer, dec_seq):
    d = xa.shape[1]
    n_ctx_rows = xa.shape[0]
    m = n_ctx_rows + xb.shape[0]
    bw = a_ctx.shape[1]
    tm = 256
    nct = n_ctx_rows // tm
    row = functools.partial(_mod_row, tm=tm, n_ctx_rows=n_ctx_rows, dec_seq=dec_seq)
    once = pl.Buffered(1)
    ctx_tile = lambda i: (jnp.minimum(i, nct - 1), 0)
    dec_tile = lambda i: (jnp.maximum(i - nct, 0), 0)
    return pl.pallas_call(
        functools.partial(_merge_kernel, nct),
        grid=(m // tm,),
        in_specs=[
            pl.BlockSpec((tm, d), ctx_tile),
            pl.BlockSpec((tm, d), dec_tile),
            pl.BlockSpec((None, None, 6, d), lambda i: (layer, row(i), 0, 0)),
            pl.BlockSpec((None, 1, d), lambda i: (layer, 0, 0)),
            pl.BlockSpec((tm, 4 * d), lambda i: (i, 0)),
            pl.BlockSpec((tm, bw), ctx_tile),
            pl.BlockSpec((tm, bw), dec_tile),
            pl.BlockSpec((tm, 3 * bw), lambda i: (i, 0)),
            pl.BlockSpec((d, d), lambda i: (0, 0), pipeline_mode=once),
            pl.BlockSpec((d, d), lambda i: (0, 0), pipeline_mode=once),
        ],
        out_specs=[pl.BlockSpec((tm, d), lambda i: (i, 0))] * 2,
        out_shape=[jax.ShapeDtypeStruct((m, d), F32), jax.ShapeDtypeStruct((m, d), BF16)],
        compiler_params=_cparams(("parallel",)),
        name="merge",
    )(xa, xb, mod, norm2_g, gates, a_ctx, a_dec, bcd, wb, wo)


def _mlp_kernel(n_ctx_tiles, n_up, n_side, h_ref, x_ref, mod_ref, w1_ref, w2_ref, *refs):
    side_in = refs[:n_side]
    oa_ref, ob_ref = refs[n_side:n_side + 2]
    side_out = refs[n_side + 2:2 * n_side + 2]
    a_ref = refs[2 * n_side + 2]
    s = pl.program_id(1)
    is_ctx = pl.program_id(0) < n_ctx_tiles
    tf = w1_ref.shape[1]

    @pl.when(s < n_up)
    def _():
        _cast_blocks(side_in, side_out)
        a = jnp.dot(h_ref[...], w1_ref[...], preferred_element_type=F32)
        a_ref[s] = jnp.square(jnp.maximum(a, 0.0)).astype(BF16)

    def down():
        _cast_blocks(side_in, side_out)
        acc = jnp.dot(a_ref[0], w2_ref[0:tf, :], preferred_element_type=F32)
        for c in range(1, n_up):
            acc = acc + jnp.dot(a_ref[c], w2_ref[c * tf:(c + 1) * tf, :], preferred_element_type=F32)
        return x_ref[...] + mod_ref[5:6, :] * acc

    @pl.when((s >= n_up) & is_ctx)
    def _():
        oa_ref[...] = down()

    @pl.when((s >= n_up) & jnp.logical_not(is_ctx))
    def _():
        ob_ref[...] = down()


def _mlp(h, x, mod, w1, w2, layer, n_ctx_rows, dec_seq, side):
    m, d = x.shape
    dff = w1.shape[1]
    tm, tf, tn = 1024, 1024, 256
    n_up, n_down = dff // tf, d // tn
    nct = n_ctx_rows // tm
    row = functools.partial(_mod_row, tm=tm, n_ctx_rows=n_ctx_rows, dec_seq=dec_seq)
    col = lambda s: jnp.maximum(s - n_up, 0)
    ctx_out = lambda i, s: (jnp.minimum(i, nct - 1), jnp.where(i < nct, col(s), n_down - 1))
    dec_out = lambda i, s: (jnp.maximum(i - nct, 0), jnp.where(i < nct, 0, col(s)))
    n_inner = n_up + n_down
    side_blocks = 128
    assert side_blocks <= (m // tm) * n_inner
    side_in, side_out, side_shapes = _side_specs(side, side_blocks, lambda i, s: i * n_inner + s)
    return pl.pallas_call(
        functools.partial(_mlp_kernel, nct, n_up, len(side)),
        grid=(m // tm, n_inner),
        in_specs=[
            pl.BlockSpec((tm, d), lambda i, s: (i, 0)),
            pl.BlockSpec((tm, tn), lambda i, s: (i, col(s))),
            pl.BlockSpec((None, None, 6, tn), lambda i, s: (layer, row(i), 0, col(s))),
            pl.BlockSpec((d, tf), lambda i, s: (0, jnp.minimum(s, n_up - 1))),
            pl.BlockSpec((dff, tn), lambda i, s: (0, col(s))),
        ] + side_in,
        out_specs=[pl.BlockSpec((tm, tn), ctx_out), pl.BlockSpec((tm, tn), dec_out)] + side_out,
        out_shape=[jax.ShapeDtypeStruct((n_ctx_rows, d), F32),
                   jax.ShapeDtypeStruct((m - n_ctx_rows, d), F32)] + side_shapes,
        scratch_shapes=[pltpu.VMEM((n_up, tm, tf), BF16)],
        compiler_params=_cparams(("arbitrary", "arbitrary")),
        name="mlp",
    )(h, x, mod, w1, w2, *[arr for arr, _ in side])


def _dft_tables(n_ctx_seq, n):
    def cs(size):
        k = np.arange(size)
        ang = 2.0 * np.pi * ((k[:, None] * k[None, :]) % size) / size
        return np.cos(ang) / np.sqrt(size), np.sin(ang) / np.sqrt(size)

    c_s, s_s = cs(n_ctx_seq)
    eye = np.eye(n // n_ctx_seq)
    c_l, s_l = cs(n)
    pos = np.stack([np.concatenate([np.kron(eye, c_s), np.kron(eye, s_s)], axis=0),
                    np.concatenate([c_l, s_l], axis=0)])
    c_c, s_c = cs(CHUNK)
    chan = np.concatenate([c_c, -s_c], axis=0)
    return jnp.asarray(pos, F32).astype(BF16), jnp.asarray(chan, F32).astype(BF16)


def _pool_tables(n_ctx_seq, n):
    def one(size):
        t = np.arange(size)
        bands, counts = [], []
        for w in POOL_WINDOWS:
            lo = np.clip(t - w // 2, 0, size - 1)
            hi = np.clip(t + w // 2 - 1, 0, size - 1)
            s = np.arange(size)[None, :]
            bands.append(((s >= lo[:, None]) & (s <= hi[:, None])).astype(np.float32))
            counts.append((hi - lo + 1).astype(np.float32))
        return np.stack(bands), np.stack(counts)

    b_s, c_s = one(n_ctx_seq)
    rep = n // n_ctx_seq
    b_l, c_l = one(n)
    band = np.stack([np.stack([np.kron(np.eye(rep), b) for b in b_s]), b_l])
    cnt = np.stack([np.tile(c_s, (1, rep)), c_l])
    cnt = np.repeat(cnt.transpose(0, 2, 1), CHUNK, axis=2)
    return jnp.asarray(band, BF16), jnp.asarray(cnt, F32)


def _rope_tables(n):
    t = jnp.arange(n)
    pos = jnp.stack([t // GRID_W, t % GRID_W], axis=-1).astype(F32)
    half = HEAD_DIM // 2
    inv = 1.0 / (ROPE_BASE ** (jnp.arange(0, half, 2, dtype=F32) / half))
    ang = pos[:, :, None] * inv
    cos, sin = jnp.cos(ang), jnp.sin(ang)
    cos_h = jnp.concatenate([cos[:, 0], cos[:, 0], cos[:, 1], cos[:, 1]], axis=-1)
    sin_h = jnp.concatenate([-sin[:, 0], sin[:, 0], -sin[:, 1], sin[:, 1]], axis=-1)
    return jnp.tile(cos_h, (1, 2)), jnp.tile(sin_h, (1, 2))


def _bias_table(rpb, n):
    n_layers, n_heads, n_d, n_e = rpb.shape
    cq = np.arange(GRID_W)
    cs = np.clip(cq - WIN_COLS // 2, 0, GRID_W - WIN_COLS)
    kc = np.arange(GRID_W)
    col_ok = (kc[None, :] >= cs[:, None]) & (kc[None, :] < cs[:, None] + WIN_COLS)
    col_mask = np.where(col_ok, 0.0, NEG_INF).astype(np.float32)
    dc_idx = np.clip(kc[None, :] - cq[:, None], -(WIN_COLS - 1), WIN_COLS - 1) + WIN_COLS - 1
    pick = (dc_idx[None, :, :] == np.arange(n_e)[:, None, None]).astype(np.float32)
    t = jnp.einsum("lhde,eck->lhcdk", rpb, jnp.asarray(pick), precision=lax.Precision.HIGHEST)
    t = t + jnp.asarray(col_mask)[:, None, :]
    flat = t.reshape(n_layers, n_heads, GRID_W, n_d * GRID_W)
    front = WIN_ROWS * GRID_W
    back = 2 * n - front - n_d * GRID_W
    return jnp.pad(flat, ((0, 0), (0, 0), (0, 0), (front, back)))


def kernel(x_prompt, x_sample, cache_k, cache_v, c, c_ctx, norm1_g, w_in, b_gate, q_norm_g, k_norm_g,
           rpb, gmlp_norm_g, w_spatial, b_spatial, w_pool, pool_scale, w_branch, w_out, norm2_g,
           w_mlp1, w_mlp2, w_ada, b_ada):
    n_b, seq, d = x_prompt.shape
    n_db, dec_seq, _ = x_sample.shape
    n_layers = w_in.shape[0]
    n_heads, past = cache_k.shape[2], cache_k.shape[3]
    bw = n_heads * HEAD_DIM
    n_ctx_rows = n_b * seq
    assert dec_seq == MIX_ROWS and MIX_ROWS % seq == 0 and n_ctx_rows % MIX_ROWS == 0
    assert dec_seq // GRID_W == 2 * WIN_ROWS and bw == 4 * LANES

    xa = x_prompt.reshape(n_ctx_rows, d)
    xb = x_sample.reshape(n_db * dec_seq, d)
    pad_rows = (-(1 + n_db)) % 8
    cond = jnp.concatenate([c_ctx[None, :], c, jnp.zeros((pad_rows, d), F32)], axis=0)
    mod = _ada(cond, w_ada, b_ada).reshape(n_layers, cond.shape[0], 6, d)

    w_in_l = w_in[0].astype(BF16)
    kct = cache_k.transpose(0, 1, 2, 4, 3)
    vct = cache_v.transpose(0, 1, 2, 4, 3)
    ws_b, wp_b = w_spatial.astype(BF16), w_pool.astype(BF16)
    bs_x = jnp.repeat(b_spatial.transpose(0, 2, 1), CHUNK, axis=2)
    qg = jnp.tile(q_norm_g, (1, 2))[:, None, :]
    kg = jnp.tile(k_norm_g, (1, 2))[:, None, :]
    dft, chan = _dft_tables(seq, MIX_ROWS)
    band, cnt = _pool_tables(seq, MIX_ROWS)
    cos, sin = _rope_tables(dec_seq)
    tab = _bias_table(rpb, dec_seq)
    n1 = norm1_g[:, None, :]
    n2 = norm2_g[:, None, :]
    gn = gmlp_norm_g[:, None, :]
    ps = pool_scale[:, None, :]
    bg = b_gate[:, None, :]

    new_kv = None
    for l in range(n_layers):
        p_arr, gates = _inproj(xa, xb, mod, n1, w_in_l, bg, l, 7 * bw, dec_seq, side=[])
        a_ctx, *new_kv = _attn_ctx(p_arr, qg, kg, l, n_layers, n_b, seq, new_kv)
        a_dec, w1_l, w2_l, wb_l, wo_l = _attn_dec(
            p_arr, qg, kg, cos, sin, kct, vct, tab, l, n_db, dec_seq, n_ctx_rows, past,
            side=[(w_mlp1, l), (w_mlp2, l), (w_branch, l), (w_out, l)])
        bcd = _mix(p_arr, gn, ws_b, bs_x, dft, chan, band, cnt, wp_b, ps, l, n_ctx_rows)
        x1, h2 = _merge(xa, xb, mod, n2, gates, a_ctx, a_dec, bcd, wb_l, wo_l, l, dec_seq)
        ahead = [(w_in, l + 1)] if l + 1 < n_layers else []
        xa, xb, *cast = _mlp(h2, x1, mod, w1_l, w2_l, l, n_ctx_rows, dec_seq, side=ahead)
        if cast:
            w_in_l = cast[0]

    return (xa.reshape(n_b, seq, d), xb.reshape(n_db, dec_seq, d),
            new_kv[0].swapaxes(3, 4), new_kv[1].swapaxes(3, 4))
```

```python
import functools

import numpy as np
import jax
import jax.numpy as jnp
from jax import lax
from jax.experimental import pallas as pl
from jax.experimental.pallas import tpu as pltpu

F32 = jnp.float32
BF16 = jnp.bfloat16

GRID_W = 64
HEAD_DIM = 64
WIN_ROWS = 8
WIN_COLS = 16
ROPE_BASE = 10000.0
CHUNK = 128
POOL_WINDOWS = (2, 4, 8, 16)
EPS = 1e-6
NEG_INF = -1e30

LANES = 128
MIX_ROWS = 1024
VMEM_LIMIT = 60 * 1024 * 1024


def _cparams(sem):
    return pltpu.CompilerParams(dimension_semantics=sem, vmem_limit_bytes=VMEM_LIMIT)


def _mod_row(i, tm, n_ctx_rows, dec_seq):
    nc = n_ctx_rows // tm
    per = dec_seq // tm
    return jnp.where(i < nc, 0, 1 + (i - nc) // per)


def _ada_kernel(c_ref, w_ref, b_ref, o_ref):
    s = jax.nn.silu(c_ref[...]).astype(BF16)
    o_ref[...] = jnp.dot(s, w_ref[...].astype(BF16), preferred_element_type=F32) + b_ref[...]


def _ada(cond, w_ada, b_ada):
    n_layers, d, n = w_ada.shape
    rows = cond.shape[0]
    tn = 1024
    return pl.pallas_call(
        _ada_kernel,
        grid=(n_layers, n // tn),
        in_specs=[
            pl.BlockSpec((rows, d), lambda l, j: (0, 0)),
            pl.BlockSpec((None, d, tn), lambda l, j: (l, 0, j)),
            pl.BlockSpec((None, 1, tn), lambda l, j: (l, 0, j)),
        ],
        out_specs=pl.BlockSpec((None, rows, tn), lambda l, j: (l, 0, j)),
        out_shape=jax.ShapeDtypeStruct((n_layers, rows, n), F32),
        compiler_params=_cparams(("parallel", "parallel")),
        name="ada",
    )(cond, w_ada, b_ada.reshape(n_layers, 1, n))


def _modulated_norm(x, g, shift, scale):
    ms = jnp.mean(x * x, axis=-1, keepdims=True)
    return (x * lax.rsqrt(ms + EPS) * g) * (1.0 + scale) + shift


def _cast_blocks(src_refs, dst_refs):
    for src, dst in zip(src_refs, dst_refs):
        dst[...] = src[...].astype(BF16)


def _side_specs(side, n_blocks, step_of):
    in_specs, out_specs, out_shapes = [], [], []
    for arr, layer in side:
        _, r, c = arr.shape
        assert r % (16 * n_blocks) == 0
        rb = r // n_blocks
        blk = lambda *ids: jnp.minimum(step_of(*ids), n_blocks - 1)
        in_specs.append(pl.BlockSpec((None, rb, c), lambda *ids, layer=layer: (layer, blk(*ids), 0)))
        out_specs.append(pl.BlockSpec((rb, c), lambda *ids: (blk(*ids), 0)))
        out_shapes.append(jax.ShapeDtypeStruct((r, c), BF16))
    return in_specs, out_specs, out_shapes


def _x_tile_copy(xa_hbm, xb_hbm, use_a, tile, nct, tm, xbuf, sem):
    if use_a:
        src = xa_hbm.at[pl.ds(tile * tm, tm), :]
    else:
        src = xb_hbm.at[pl.ds((tile - nct) * tm, tm), :]
    return pltpu.make_async_copy(src, xbuf, sem)


def _inproj_kernel(nct, n_a, n_g, n_parts, n_side, xa_hbm, xb_hbm, mod_ref, modn_ref, g_ref, wa_ref, *refs):
    wg_refs, bg_ref, refs = refs[:n_parts], refs[n_parts], refs[n_parts + 1:]
    side_in = refs[:n_side]
    os_ref, og_ref = refs[n_side:n_side + 2]
    side_out = refs[n_side + 2:2 * n_side + 2]
    xbuf, h0_ref, h1_ref, sem = refs[2 * n_side + 2:]
    i = pl.program_id(0)
    j = pl.program_id(1)
    n_tiles = pl.num_programs(0)
    tm = xbuf.shape[0]
    rc = tm // n_g
    slot = i % 2
    copy = functools.partial(_x_tile_copy, xa_hbm, xb_hbm, nct=nct, tm=tm, xbuf=xbuf, sem=sem)

    def start(tile):
        @pl.when(tile < nct)
        def _():
            copy(True, tile).start()

        @pl.when(tile >= nct)
        def _():
            copy(False, tile).start()

    @pl.when((i == 0) & (j == 0))
    def _():
        start(i)
        copy(True, 0).wait()
        h = _modulated_norm(xbuf[...], g_ref[...], mod_ref[0:1, :], mod_ref[1:2, :])
        h0_ref[...] = h.astype(BF16)

    @pl.when((j == 0) & (i + 1 < n_tiles))
    def _():
        start(i + 1)

    @pl.when((j == n_a) & (i + 1 < n_tiles))
    def _():
        copy(True, 0).wait()

    for cur, nxt in ((h0_ref, h1_ref), (h1_ref, h0_ref)):
        mine = slot == (0 if cur is h0_ref else 1)

        @pl.when((j < n_a) & mine)
        def _(cur=cur):
            _cast_blocks(side_in, side_out)
            os_ref[...] = jnp.dot(cur[...], wa_ref[...], preferred_element_type=F32).astype(BF16)

        @pl.when((j >= n_a) & mine)
        def _(cur=cur, nxt=nxt):
            _cast_blocks(side_in, side_out)
            rows = pl.ds(pl.multiple_of((j - n_a) * rc, rc), rc)
            hn = _modulated_norm(xbuf[rows, :], g_ref[...], modn_ref[0:1, :], modn_ref[1:2, :])
            nxt[rows, :] = hn.astype(BF16)
            wpart = wg_refs[0].shape[1]
            for part, w_ref in enumerate(wg_refs):
                cols = slice(part * wpart, (part + 1) * wpart)
                logits = jnp.dot(cur[...], w_ref[...], preferred_element_type=F32) + bg_ref[:, cols]
                og_ref[:, cols] = logits.astype(BF16)


def _inproj(xa, xb, mod, norm_g, w, b_gate, layer, n_mix, dec_seq, side):
    d = xa.shape[1]
    n_ctx_rows = xa.shape[0]
    m = n_ctx_rows + xb.shape[0]
    n_gate = w.shape[1] - n_mix
    tm, ta, tg, wpart = 1024, 512, 2048, 512
    n_parts = tg // wpart
    assert n_mix % ta == 0 and n_mix % wpart == 0 and n_gate % tg == 0
    n_a, n_g = n_mix // ta, n_gate // tg
    n_tiles = m // tm
    row = functools.partial(_mod_row, tm=tm, n_ctx_rows=n_ctx_rows, dec_seq=dec_seq)
    a_col = lambda j: jnp.minimum(j, n_a - 1)
    g_col = lambda j: jnp.maximum(j - n_a, 0)
    n_inner = n_a + n_g
    side_blocks = 128
    assert side_blocks <= n_tiles * n_inner
    side_in, side_out, side_shapes = _side_specs(side, side_blocks, lambda i, j: i * n_inner + j)
    return pl.pallas_call(
        functools.partial(_inproj_kernel, n_ctx_rows // tm, n_a, n_g, n_parts, len(side)),
        grid=(n_tiles, n_inner),
        in_specs=[
            pl.BlockSpec(memory_space=pl.ANY),
            pl.BlockSpec(memory_space=pl.ANY),
            pl.BlockSpec((None, None, 6, d), lambda i, j: (layer, row(i), 0, 0)),
            pl.BlockSpec((None, None, 6, d), lambda i, j: (layer, row(jnp.minimum(i + 1, n_tiles - 1)), 0, 0)),
            pl.BlockSpec((None, 1, d), lambda i, j: (layer, 0, 0)),
            pl.BlockSpec((d, ta), lambda i, j: (0, a_col(j))),
        ] + [
            pl.BlockSpec((d, wpart), lambda i, j, part=part: (0, n_mix // wpart + n_parts * g_col(j) + part))
            for part in range(n_parts)
        ] + [
            pl.BlockSpec((None, 1, tg), lambda i, j: (layer, 0, g_col(j))),
        ] + side_in,
        out_specs=[
            pl.BlockSpec((tm, ta), lambda i, j: (i, a_col(j))),
            pl.BlockSpec((tm, tg), lambda i, j: (i, g_col(j))),
        ] + side_out,
        out_shape=[jax.ShapeDtypeStruct((m, n_mix), BF16), jax.ShapeDtypeStruct((m, n_gate), BF16)] + side_shapes,
        scratch_shapes=[pltpu.VMEM((tm, d), F32), pltpu.VMEM((tm, d), BF16), pltpu.VMEM((tm, d), BF16),
                        pltpu.SemaphoreType.DMA(())],
        compiler_params=_cparams(("arbitrary", "arbitrary")),
        name="inproj",
    )(xa, xb, mod, mod, norm_g, w, *([w] * n_parts), b_gate, *[arr for arr, _ in side])


def _head_norm(x, g, lane):
    row = lax.broadcasted_iota(jnp.int32, (LANES, LANES), 0)
    same_head = (row >= HEAD_DIM) == (lane >= HEAD_DIM)
    ones = jnp.where(same_head, 1.0, 0.0).astype(BF16)
    x2 = x * x
    hi = x2.astype(BF16)
    lo = (x2 - hi.astype(F32)).astype(BF16)
    ss = (jnp.dot(hi, ones, preferred_element_type=F32) + jnp.dot(lo, ones, preferred_element_type=F32))
    return x * lax.rsqrt(ss * (1.0 / HEAD_DIM) + EPS) * g


def _head_norm_xlu(x, g, lane):
    low = lane < HEAD_DIM
    x2 = x * x
    s_lo = jnp.sum(jnp.where(low, x2, 0.0), axis=-1, keepdims=True)
    s_hi = jnp.sum(jnp.where(low, 0.0, x2), axis=-1, keepdims=True)
    ss = jnp.where(low, s_lo, s_hi)
    return x * lax.rsqrt(ss * (1.0 / HEAD_DIM) + EPS) * g


def _rope(x, cos, sin_signed, lane):
    first = (lane & 16) == 0
    partner = jnp.where(first, pltpu.roll(x, LANES - 16, 1), pltpu.roll(x, 16, 1))
    return x * cos + partner * sin_signed


def _nt_dot(a, b):
    return lax.dot_general(a, b, (((1,), (1,)), ((), ())), preferred_element_type=F32)


def _head_mask(lane, h, value):
    return jnp.where((lane >= HEAD_DIM) == (h == 1), value, 0.0).astype(F32)


def _attn_ctx_kernel(first, q_ref, k_ref, v_ref, qg_ref, kg_ref, *rest):
    a_ref, kn_ref, vn_ref = rest[-3:]
    if first:
        for later in range(1, kn_ref.shape[0]):
            kn_ref[later] = jnp.zeros(kn_ref.shape[1:], F32)
            vn_ref[later] = jnp.zeros(vn_ref.shape[1:], F32)
        kn_ref, vn_ref = kn_ref.at[0], vn_ref.at[0]
    lane = lax.broadcasted_iota(jnp.int32, (1, LANES), 1)
    scale = HEAD_DIM ** -0.5
    for p in range(q_ref.shape[1] // LANES):
        cols = slice(p * LANES, (p + 1) * LANES)
        q = _head_norm_xlu(q_ref[:, cols].astype(F32), qg_ref[...], lane)
        k = _head_norm_xlu(k_ref[:, cols].astype(F32), kg_ref[...], lane)
        v = v_ref[:, cols].astype(F32)
        kt, vt = k.T, v.T
        for h in range(2):
            kn_ref[2 * p + h] = kt[h * HEAD_DIM:(h + 1) * HEAD_DIM, :]
            vn_ref[2 * p + h] = vt[h * HEAD_DIM:(h + 1) * HEAD_DIM, :]
        kb = k.astype(BF16)
        acc = jnp.zeros(q.shape, F32)
        for h in range(2):
            s = _nt_dot((q * _head_mask(lane, h, scale)).astype(BF16), kb)
            e = jnp.exp(s - jnp.max(s, axis=-1, keepdims=True))
            o = jnp.dot(e.astype(BF16), (v * _head_mask(lane, h, 1.0)).astype(BF16),
                        preferred_element_type=F32)
            acc = acc + o / jnp.sum(e, axis=-1, keepdims=True)
        a_ref[:, cols] = acc.astype(BF16)


def _attn_ctx(p_arr, qg, kg, layer, n_layers, n_batch, seq, new_kv):
    w = 4 * LANES
    rows = n_batch * seq
    n_heads = w // HEAD_DIM
    blk = lambda c: pl.BlockSpec((seq, w), lambda b: (b, c))
    gain = pl.BlockSpec((None, 1, LANES), lambda b: (layer, 0, 0))
    first = new_kv is None
    assert first == (layer == 0)
    if first:
        cache = pl.BlockSpec((None, n_layers, n_heads, HEAD_DIM, seq), lambda b: (b, 0, 0, 0, 0))
    else:
        cache = pl.BlockSpec((None, None, n_heads, HEAD_DIM, seq), lambda b: (b, layer, 0, 0, 0))
    cache_shape = jax.ShapeDtypeStruct((n_batch, n_layers, n_heads, HEAD_DIM, seq), F32)
    carried = [] if first else list(new_kv)
    return pl.pallas_call(
        functools.partial(_attn_ctx_kernel, first),
        grid=(n_batch,),
        in_specs=[blk(0), blk(1), blk(2), gain, gain] + [pl.BlockSpec(memory_space=pl.ANY)] * len(carried),
        out_specs=[pl.BlockSpec((seq, w), lambda b: (b, 0)), cache, cache],
        out_shape=[jax.ShapeDtypeStruct((rows, w), BF16), cache_shape, cache_shape],
        input_output_aliases={5 + n: 1 + n for n in range(len(carried))},
        compiler_params=_cparams(("parallel",)),
        name="attn_ctx",
    )(p_arr, p_arr, p_arr, qg, kg, *carried)


def _row_window(r, n_rows):
    rs = min(max(r - WIN_ROWS // 2, 0), n_rows - WIN_ROWS)
    if rs % 2 == 0:
        return rs * GRID_W, WIN_ROWS * GRID_W, True
    return (rs - 1) * GRID_W, (WIN_ROWS + 2) * GRID_W, False


def _attn_dec_kernel(n_side, q_ref, k_ref, v_ref, qg_ref, kg_ref, cos_ref, sin_ref, kc_ref, vc_ref,
                     tab_ref, *refs):
    side_in = refs[:n_side]
    o_ref = refs[n_side]
    side_out = refs[n_side + 1:2 * n_side + 1]
    s_ref, sc_ref, p_ref, pc_ref, li_ref, tsh_ref = refs[2 * n_side + 1:]
    _cast_blocks(side_in, side_out)
    n = q_ref.shape[0]
    n_rows = n // GRID_W
    lane = lax.broadcasted_iota(jnp.int32, (1, LANES), 1)
    scale = HEAD_DIM ** -0.5
    n_pp = q_ref.shape[1] // LANES

    def on_head_rows(xt, h):
        xb = xt.astype(BF16)
        zero = jnp.zeros_like(xb)
        return jnp.concatenate([xb, zero] if h == 0 else [zero, xb], axis=0)

    wide = (WIN_ROWS + 2) * GRID_W
    wl = lax.broadcasted_iota(jnp.int32, (1, wide), 1)
    edge_mask = jnp.where((wl < GRID_W) | (wl >= wide - GRID_W), NEG_INF, 0.0).astype(F32)
    pairs = []
    for pp in range(n_pp):
        cols = slice(pp * LANES, (pp + 1) * LANES)
        q = _head_norm(q_ref[:, cols].astype(F32), qg_ref[...], lane)
        k = _head_norm(k_ref[:, cols].astype(F32), kg_ref[...], lane)
        qr = _rope(q, cos_ref[...], sin_ref[...], lane)
        kr = _rope(k, cos_ref[...], sin_ref[...], lane).astype(BF16)
        v = v_ref[:, cols].astype(F32)
        qs = (q * scale).astype(BF16)
        heads = []
        for h in range(2):
            hh = 2 * pp + h
            tsh_ref[hh] = pltpu.roll(tab_ref[hh], 2 * n - GRID_W, 1)
            heads.append(((qr * _head_mask(lane, h, scale)).astype(BF16),
                          (v * _head_mask(lane, h, 1.0)).astype(BF16),
                          on_head_rows(kc_ref[hh], h), on_head_rows(vc_ref[hh], h)))
        pairs.append((cols, qs, kr, heads))
    chunk_rows = s_ref.shape[1] // GRID_W
    for c, pp in [(c, pp) for c in range(n_rows // chunk_rows) for pp in range(n_pp)]:
        wins = [_row_window(r, n_rows) for r in range(c * chunk_rows, (c + 1) * chunk_rows)]
        a_c = min(a for a, _, _ in wins)
        e_c = max(a + w for a, w, _ in wins)
        w_c = e_c - a_c
        qrows = slice(c * chunk_rows * GRID_W, (c + 1) * chunk_rows * GRID_W)
        cols, qs, kr, heads = pairs[pp]
        acc = jnp.zeros((chunk_rows * GRID_W, LANES), F32)
        for h in range(2):
            qh, vh, kch, vch = heads[h]
            slot = 2 * pp + h
            hh = slot
            s_ref[slot, :, 0:w_c] = _nt_dot(qh[qrows], kr[a_c:e_c])
            sc_ref[slot] = jnp.dot(qs[qrows], kch, preferred_element_type=F32)
            for rr in range(chunk_rows):
                r = c * chunk_rows + rr
                rows = pl.ds(rr * GRID_W, GRID_W)
                a, w, exact = wins[rr]
                if (n_rows - 1 - r) % 2 == 0:
                    bias = tab_ref[hh, :, (n_rows - 1 - r) * GRID_W + a:(n_rows - 1 - r) * GRID_W + a + w]
                else:
                    bias = tsh_ref[hh, :, (n_rows - 2 - r) * GRID_W + a:(n_rows - 2 - r) * GRID_W + a + w]
                sl = s_ref[slot, rows, a - a_c:a - a_c + w] + bias
                if not exact:
                    sl = sl + edge_mask
                sx = sc_ref[slot, rows, :]
                m = jnp.maximum(jnp.max(sl, axis=-1, keepdims=True), jnp.max(sx, axis=-1, keepdims=True))
                el = jnp.exp(sl - m)
                ex = jnp.exp(sx - m)
                den = jnp.sum(el, axis=-1, keepdims=True) + jnp.sum(ex, axis=-1, keepdims=True)
                pieces = [el.astype(BF16)]
                if a > a_c:
                    pieces.insert(0, jnp.zeros((GRID_W, a - a_c), BF16))
                if a + w < e_c:
                    pieces.append(jnp.zeros((GRID_W, e_c - a - w), BF16))
                p_ref[slot, rows, 0:w_c] = jnp.concatenate(pieces, axis=1) if len(pieces) > 1 else pieces[0]
                pc_ref[slot, rows, :] = ex.astype(BF16)
                li_ref[slot, rows, :] = jnp.broadcast_to(1.0 / den, (GRID_W, LANES))
            o = jnp.dot(p_ref[slot, :, 0:w_c], vh[a_c:e_c], preferred_element_type=F32)
            o = o + _nt_dot(pc_ref[slot], vch)
            acc = acc + o * li_ref[slot]
        o_ref[qrows, cols] = acc.astype(BF16)


def _attn_dec(p_arr, qg, kg, cos, sin, kc, vc, tab, layer, n_batch, seq, row0, past, side):
    rb0 = row0 // seq
    n_pairs = 4
    g = 2
    nh = 2 * g
    n_steps = n_pairs // g
    side_in, side_out, side_shapes = _side_specs(side, n_batch * n_steps, lambda b, p: b * n_steps + p)
    chunk = 4 * GRID_W
    win = (WIN_ROWS + 4) * GRID_W
    blk = lambda c: pl.BlockSpec((seq, g * LANES), lambda b, p: (rb0 + b, c + p))
    gain = pl.BlockSpec((None, 1, LANES), lambda b, p: (layer, 0, 0))
    const = pl.BlockSpec((seq, LANES), lambda b, p: (0, 0))
    cache = pl.BlockSpec((None, None, nh, HEAD_DIM, past), lambda b, p: (b, layer, p, 0, 0))
    return pl.pallas_call(
        functools.partial(_attn_dec_kernel, len(side)),
        grid=(n_batch, n_steps),
        in_specs=[
            blk(0), blk(n_steps), blk(2 * n_steps), gain, gain, const, const, cache, cache,
            pl.BlockSpec((None, nh, GRID_W, 2 * seq), lambda b, p: (layer, p, 0, 0)),
        ] + side_in,
        out_specs=[pl.BlockSpec((seq, g * LANES), lambda b, p: (b, p))] + side_out,
        out_shape=[jax.ShapeDtypeStruct((n_batch * seq, n_pairs * LANES), BF16)] + side_shapes,
        scratch_shapes=[
            pltpu.VMEM((nh, chunk, win), F32), pltpu.VMEM((nh, chunk, past), F32),
            pltpu.VMEM((nh, chunk, win), BF16), pltpu.VMEM((nh, chunk, past), BF16),
            pltpu.VMEM((nh, chunk, LANES), F32), pltpu.VMEM((nh, GRID_W, 2 * seq), F32),
        ],
        compiler_params=_cparams(("parallel", "parallel")),
        name="attn_dec",
    )(p_arr, p_arr, p_arr, qg, kg, cos, sin, kc, vc, tab, *[arr for arr, _ in side])


def _mix_kernel(u_ref, vg_ref, xf_ref, xp_ref, gn_ref, ws_ref, bs_ref, dft_ref, cs_ref, band_ref,
                cnt_ref, wp_ref, ps_ref, o_ref, vgn_ref, ab_ref):
    n = u_ref.shape[0]
    bw = u_ref.shape[1]
    n_groups = bw // CHUNK
    for c in range(n // CHUNK):
        rows = pl.ds(c * CHUNK, CHUNK)
        vg = jax.nn.gelu(vg_ref[rows, :].astype(F32))
        ms = jnp.mean(vg * vg, axis=-1, keepdims=True)
        vgn_ref[rows, :] = (vg * lax.rsqrt(ms + EPS) * gn_ref[...]).astype(BF16)
    for c in range(n // CHUNK):
        rows = pl.ds(c * CHUNK, CHUNK)
        parts = [jnp.dot(ws_ref[g], vgn_ref[rows, g * CHUNK:(g + 1) * CHUNK], preferred_element_type=F32)
                 for g in range(n_groups)]
        mixed = jnp.concatenate(parts, axis=1) + bs_ref[...]
        o_ref[rows, 0:bw] = (jax.nn.gelu(u_ref[rows, :].astype(F32)) * mixed).astype(BF16)
    ab_ref[...] = jnp.dot(dft_ref[...], xf_ref[...], preferred_element_type=F32).astype(BF16)
    for g in range(n_groups):
        gc = slice(g * CHUNK, (g + 1) * CHUNK)
        lhs = jnp.concatenate([ab_ref[0:n, gc], ab_ref[n:2 * n, gc]], axis=1)
        o_ref[:, bw + g * CHUNK:bw + (g + 1) * CHUNK] = jnp.dot(
            lhs, cs_ref[...], preferred_element_type=F32).astype(BF16)
    for g in range(n_groups):
        gc = slice(g * CHUNK, (g + 1) * CHUNK)
        xg = xp_ref[:, gc]
        wsum = jnp.dot(band_ref[g], xg, preferred_element_type=F32)
        pooled = wsum / cnt_ref[:, gc] - xg.astype(F32)
        y = jnp.dot(pooled.astype(BF16), wp_ref[g], preferred_element_type=F32) * ps_ref[:, gc]
        o_ref[:, 2 * bw + g * CHUNK:2 * bw + (g + 1) * CHUNK] = y.astype(BF16)


def _mix(p_arr, gn, ws, bs, dft, cs, band, cnt, wp, ps, layer, n_ctx_rows):
    m = p_arr.shape[0]
    bw = gn.shape[-1]
    n = MIX_ROWS
    n_groups = bw // CHUNK
    kind = lambda i: jnp.where(i < n_ctx_rows // n, 0, 1)
    blk = lambda c: pl.BlockSpec((n, bw), lambda i: (i, c))
    return pl.pallas_call(
        _mix_kernel,
        grid=(m // n,),
        in_specs=[
            blk(3), blk(4), blk(5), blk(6),
            pl.BlockSpec((None, 1, bw), lambda i: (layer, 0, 0)),
            pl.BlockSpec((None, n_groups, CHUNK, CHUNK), lambda i: (layer, 0, 0, 0)),
            pl.BlockSpec((None, CHUNK, bw), lambda i: (layer, 0, 0)),
            pl.BlockSpec((None, 2 * n, n), lambda i: (kind(i), 0, 0)),
            pl.BlockSpec((2 * CHUNK, CHUNK), lambda i: (0, 0)),
            pl.BlockSpec((None, n_groups, n, n), lambda i: (kind(i), 0, 0, 0)),
            pl.BlockSpec((None, n, bw), lambda i: (kind(i), 0, 0)),
            pl.BlockSpec((None, n_groups, CHUNK, CHUNK), lambda i: (layer, 0, 0, 0)),
            pl.BlockSpec((None, 1, bw), lambda i: (layer, 0, 0)),
        ],
        out_specs=pl.BlockSpec((n, 3 * bw), lambda i: (i, 0)),
        out_shape=jax.ShapeDtypeStruct((m, 3 * bw), BF16),
        scratch_shapes=[pltpu.VMEM((n, bw), BF16), pltpu.VMEM((2 * n, bw), BF16)],
        compiler_params=_cparams(("parallel",)),
        name="mix",
    )(p_arr, p_arr, p_arr, p_arr, gn, ws, bs, dft, cs, band, cnt, wp, ps)


def _merge_kernel(n_ctx_tiles, xa_ref, xb_ref, mod_ref, g2_ref, gate_ref, actx_ref, adec_ref, bcd_ref,
                  wb_ref, wo_ref, o_ref, h_ref):
    d = xa_ref.shape[1]
    bw = actx_ref.shape[1]
    is_ctx = pl.program_id(0) < n_ctx_tiles
    merged = jnp.zeros(xa_ref.shape, F32)
    for b in range(d // bw):
        if b == 0:
            br = jnp.where(is_ctx, actx_ref[...], adec_ref[...])
        else:
            br = bcd_ref[:, (b - 1) * bw:b * bw]
        y = jnp.dot(br, wb_ref[b * bw:(b + 1) * bw, :], preferred_element_type=F32)
        gate = 0.5 * jnp.tanh(0.5 * gate_ref[:, b * d:(b + 1) * d].astype(F32)) + 0.5
        merged = merged + gate * y
    mix = jnp.dot(merged.astype(BF16), wo_ref[...], preferred_element_type=F32)
    x = jnp.where(is_ctx, xa_ref[...], xb_ref[...])
    x1 = x + mod_ref[2:3, :] * mix
    o_ref[...] = x1
    h_ref[...] = _modulated_norm(x1, g2_ref[...], mod_ref[3:4, :], mod_ref[4:5, :]).astype(BF16)


def _merge(xa, xb, mod, norm2_g, gates, a_ctx, a_dec, bcd, wb, wo, layer, dec_seq):
    d = xa.shape[1]
    n_ctx_rows = xa.shape[0]
    m = n_ctx_rows + xb.shape[0]
    bw = a_ctx.shape[1]
    tm = 256
    nct = n_ctx_rows // tm
    row = functools.partial(_mod_row, tm=tm, n_ctx_rows=n_ctx_rows, dec_seq=dec_seq)
    once = pl.Buffered(1)
    ctx_tile = lambda i: (jnp.minimum(i, nct - 1), 0)
    dec_tile = lambda i: (jnp.maximum(i - nct, 0), 0)
    return pl.pallas_call(
        functools.partial(_merge_kernel, nct),
        grid=(m // tm,),
        in_specs=[
            pl.BlockSpec((tm, d), ctx_tile),
            pl.BlockSpec((tm, d), dec_tile),
            pl.BlockSpec((None, None, 6, d), lambda i: (layer, row(i), 0, 0)),
            pl.BlockSpec((None, 1, d), lambda i: (layer, 0, 0)),
            pl.BlockSpec((tm, 4 * d), lambda i: (i, 0)),
            pl.BlockSpec((tm, bw), ctx_tile),
            pl.BlockSpec((tm, bw), dec_tile),
            pl.BlockSpec((tm, 3 * bw), lambda i: (i, 0)),
            pl.BlockSpec((d, d), lambda i: (0, 0), pipeline_mode=once),
            pl.BlockSpec((d, d), lambda i: (0, 0), pipeline_mode=once),
        ],
        out_specs=[pl.BlockSpec((tm, d), lambda i: (i, 0))] * 2,
        out_shape=[jax.ShapeDtypeStruct((m, d), F32), jax.ShapeDtypeStruct((m, d), BF16)],
        compiler_params=_cparams(("parallel",)),
        name="merge",
    )(xa, xb, mod, norm2_g, gates, a_ctx, a_dec, bcd, wb, wo)


def _mlp_kernel(n_ctx_tiles, n_up, n_side, h_ref, x_ref, mod_ref, w1_ref, w2_ref, *refs):
    side_in = refs[:n_side]
    oa_ref, ob_ref = refs[n_side:n_side + 2]
    side_out = refs[n_side + 2:2 * n_side + 2]
    a_ref = refs[2 * n_side + 2]
    s = pl.program_id(1)
    is_ctx = pl.program_id(0) < n_ctx_tiles
    tf = w1_ref.shape[1]

    @pl.when(s < n_up)
    def _():
        _cast_blocks(side_in, side_out)
        a = jnp.dot(h_ref[...], w1_ref[...], preferred_element_type=F32)
        a_ref[s] = jnp.square(jnp.maximum(a, 0.0)).astype(BF16)

    def down():
        _cast_blocks(side_in, side_out)
        acc = jnp.dot(a_ref[0], w2_ref[0:tf, :], preferred_element_type=F32)
        for c in range(1, n_up):
            acc = acc + jnp.dot(a_ref[c], w2_ref[c * tf:(c + 1) * tf, :], preferred_element_type=F32)
        return x_ref[...] + mod_ref[5:6, :] * acc

    @pl.when((s >= n_up) & is_ctx)
    def _():
        oa_ref[...] = down()

    @pl.when((s >= n_up) & jnp.logical_not(is_ctx))
    def _():
        ob_ref[...] = down()


def _mlp(h, x, mod, w1, w2, layer, n_ctx_rows, dec_seq, side):
    m, d = x.shape
    dff = w1.shape[1]
    tm, tf, tn = 1024, 1024, 256
    n_up, n_down = dff // tf, d // tn
    nct = n_ctx_rows // tm
    row = functools.partial(_mod_row, tm=tm, n_ctx_rows=n_ctx_rows, dec_seq=dec_seq)
    col = lambda s: jnp.maximum(s - n_up, 0)
    ctx_out = lambda i, s: (jnp.minimum(i, nct - 1), jnp.where(i < nct, col(s), n_down - 1))
    dec_out = lambda i, s: (jnp.maximum(i - nct, 0), jnp.where(i < nct, 0, col(s)))
    n_inner = n_up + n_down
    side_blocks = 128
    assert side_blocks <= (m // tm) * n_inner
    side_in, side_out, side_shapes = _side_specs(side, side_blocks, lambda i, s: i * n_inner + s)
    return pl.pallas_call(
        functools.partial(_mlp_kernel, nct, n_up, len(side)),
        grid=(m // tm, n_inner),
        in_specs=[
            pl.BlockSpec((tm, d), lambda i, s: (i, 0)),
            pl.BlockSpec((tm, tn), lambda i, s: (i, col(s))),
            pl.BlockSpec((None, None, 6, tn), lambda i, s: (layer, row(i), 0, col(s))),
            pl.BlockSpec((d, tf), lambda i, s: (0, jnp.minimum(s, n_up - 1))),
            pl.BlockSpec((dff, tn), lambda i, s: (0, col(s))),
        ] + side_in,
        out_specs=[pl.BlockSpec((tm, tn), ctx_out), pl.BlockSpec((tm, tn), dec_out)] + side_out,
        out_shape=[jax.ShapeDtypeStruct((n_ctx_rows, d), F32),
                   jax.ShapeDtypeStruct((m - n_ctx_rows, d), F32)] + side_shapes,
        scratch_shapes=[pltpu.VMEM((n_up, tm, tf), BF16)],
        compiler_params=_cparams(("arbitrary", "arbitrary")),
        name="mlp",
    )(h, x, mod, w1, w2, *[arr for arr, _ in side])


def _dft_tables(n_ctx_seq, n):
    def cs(size):
        k = np.arange(size)
        ang = 2.0 * np.pi * ((k[:, None] * k[None, :]) % size) / size
        return np.cos(ang) / np.sqrt(size), np.sin(ang) / np.sqrt(size)

    c_s, s_s = cs(n_ctx_seq)
    eye = np.eye(n // n_ctx_seq)
    c_l, s_l = cs(n)
    pos = np.stack([np.concatenate([np.kron(eye, c_s), np.kron(eye, s_s)], axis=0),
                    np.concatenate([c_l, s_l], axis=0)])
    c_c, s_c = cs(CHUNK)
    chan = np.concatenate([c_c, -s_c], axis=0)
    return jnp.asarray(pos, F32).astype(BF16), jnp.asarray(chan, F32).astype(BF16)


def _pool_tables(n_ctx_seq, n):
    def one(size):
        t = np.arange(size)
        bands, counts = [], []
        for w in POOL_WINDOWS:
            lo = np.clip(t - w // 2, 0, size - 1)
            hi = np.clip(t + w // 2 - 1, 0, size - 1)
            s = np.arange(size)[None, :]
            bands.append(((s >= lo[:, None]) & (s <= hi[:, None])).astype(np.float32))
            counts.append((hi - lo + 1).astype(np.float32))
        return np.stack(bands), np.stack(counts)

    b_s, c_s = one(n_ctx_seq)
    rep = n // n_ctx_seq
    b_l, c_l = one(n)
    band = np.stack([np.stack([np.kron(np.eye(rep), b) for b in b_s]), b_l])
    cnt = np.stack([np.tile(c_s, (1, rep)), c_l])
    cnt = np.repeat(cnt.transpose(0, 2, 1), CHUNK, axis=2)
    return jnp.asarray(band, BF16), jnp.asarray(cnt, F32)


def _rope_tables(n):
    t = jnp.arange(n)
    pos = jnp.stack([t // GRID_W, t % GRID_W], axis=-1).astype(F32)
    half = HEAD_DIM // 2
    inv = 1.0 / (ROPE_BASE ** (jnp.arange(0, half, 2, dtype=F32) / half))
    ang = pos[:, :, None] * inv
    cos, sin = jnp.cos(ang), jnp.sin(ang)
    cos_h = jnp.concatenate([cos[:, 0], cos[:, 0], cos[:, 1], cos[:, 1]], axis=-1)
    sin_h = jnp.concatenate([-sin[:, 0], sin[:, 0], -sin[:, 1], sin[:, 1]], axis=-1)
    return jnp.tile(cos_h, (1, 2)), jnp.tile(sin_h, (1, 2))


def _bias_table(rpb, n):
    n_layers, n_heads, n_d, n_e = rpb.shape
    cq = np.arange(GRID_W)
    cs = np.clip(cq - WIN_COLS // 2, 0, GRID_W - WIN_COLS)
    kc = np.arange(GRID_W)
    col_ok = (kc[None, :] >= cs[:, None]) & (kc[None, :] < cs[:, None] + WIN_COLS)
    col_mask = np.where(col_ok, 0.0, NEG_INF).astype(np.float32)
    dc_idx = np.clip(kc[None, :] - cq[:, None], -(WIN_COLS - 1), WIN_COLS - 1) + WIN_COLS - 1
    pick = (dc_idx[None, :, :] == np.arange(n_e)[:, None, None]).astype(np.float32)
    t = jnp.einsum("lhde,eck->lhcdk", rpb, jnp.asarray(pick), precision=lax.Precision.HIGHEST)
    t = t + jnp.asarray(col_mask)[:, None, :]
    flat = t.reshape(n_layers, n_heads, GRID_W, n_d * GRID_W)
    front = WIN_ROWS * GRID_W
    back = 2 * n - front - n_d * GRID_W
    return jnp.pad(flat, ((0, 0), (0, 0), (0, 0), (front, back)))


def kernel(x_prompt, x_sample, cache_k, cache_v, c, c_ctx, norm1_g, w_in, b_gate, q_norm_g, k_norm_g,
           rpb, gmlp_norm_g, w_spatial, b_spatial, w_pool, pool_scale, w_branch, w_out, norm2_g,
           w_mlp1, w_mlp2, w_ada, b_ada):
    n_b, seq, d = x_prompt.shape
    n_db, dec_seq, _ = x_sample.shape
    n_layers = w_in.shape[0]
    n_heads, past = cache_k.shape[2], cache_k.shape[3]
    bw = n_heads * HEAD_DIM
    n_ctx_rows = n_b * seq
    assert dec_seq == MIX_ROWS and MIX_ROWS % seq == 0 and n_ctx_rows % MIX_ROWS == 0
    assert dec_seq // GRID_W == 2 * WIN_ROWS and bw == 4 * LANES

    xa = x_prompt.reshape(n_ctx_rows, d)
    xb = x_sample.reshape(n_db * dec_seq, d)
    pad_rows = (-(1 + n_db)) % 8
    cond = jnp.concatenate([c_ctx[None, :], c, jnp.zeros((pad_rows, d), F32)], axis=0)
    mod = _ada(cond, w_ada, b_ada).reshape(n_layers, cond.shape[0], 6, d)

    w_in_l = w_in[0].astype(BF16)
    kct = cache_k.transpose(0, 1, 2, 4, 3)
    vct = cache_v.transpose(0, 1, 2, 4, 3)
    ws_b, wp_b = w_spatial.astype(BF16), w_pool.astype(BF16)
    bs_x = jnp.repeat(b_spatial.transpose(0, 2, 1), CHUNK, axis=2)
    qg = jnp.tile(q_norm_g, (1, 2))[:, None, :]
    kg = jnp.tile(k_norm_g, (1, 2))[:, None, :]
    dft, chan = _dft_tables(seq, MIX_ROWS)
    band, cnt = _pool_tables(seq, MIX_ROWS)
    cos, sin = _rope_tables(dec_seq)
    tab = _bias_table(rpb, dec_seq)
    n1 = norm1_g[:, None, :]
    n2 = norm2_g[:, None, :]
    gn = gmlp_norm_g[:, None, :]
    ps = pool_scale[:, None, :]
    bg = b_gate[:, None, :]

    new_kv = None
    for l in range(n_layers):
        p_arr, gates = _inproj(xa, xb, mod, n1, w_in_l, bg, l, 7 * bw, dec_seq, side=[])
        a_ctx, *new_kv = _attn_ctx(p_arr, qg, kg, l, n_layers, n_b, seq, new_kv)
        a_dec, w1_l, w2_l, wb_l, wo_l = _attn_dec(
            p_arr, qg, kg, cos, sin, kct, vct, tab, l, n_db, dec_seq, n_ctx_rows, past,
            side=[(w_mlp1, l), (w_mlp2, l), (w_branch, l), (w_out, l)])
        bcd = _mix(p_arr, gn, ws_b, bs_x, dft, chan, band, cnt, wp_b, ps, l, n_ctx_rows)
        x1, h2 = _merge(xa, xb, mod, n2, gates, a_ctx, a_dec, bcd, wb_l, wo_l, l, dec_seq)
        ahead = [(w_in, l + 1)] if l + 1 < n_layers else []
        xa, xb, *cast = _mlp(h2, x1, mod, w1_l, w2_l, l, n_ctx_rows, dec_seq, side=ahead)
        if cast:
            w_in_l = cast[0]

    return (xa.reshape(n_b, seq, d), xb.reshape(n_db, dec_seq, d),
            new_kv[0].swapaxes(3, 4), new_kv[1].swapaxes(3, 4))
```
